```python
import math
import jax
import jax.numpy as jnp
from jax import lax
import numpy as np

D_MODEL = 2048
BATCH = 2
SEQ = 4096
DEPTH = 4
DEC_BATCH = 8
DEC_SEQ = 1
PAST_LEN = 16384
PAGE_SIZE = 128

H_A = 8
DK_A = 64
DV_A = 128
H_B = 8
HD_B = 128
MOBA_BLOCK = 256
MOBA_TOPK = 3
MOBA_Q_CHUNK = 32
ROPE_DIMS = HD_B // 4
ROPE_THETA = 500000.0
H_QK_C = 16
H_V_C = 32
DK_C = 128
DV_C = 128
CONV_C = 4
CONV_DIM_C = 2 * H_QK_C * DK_C + H_V_C * DV_C
D_FF = 5504
FFN_CONV = 3
CHUNK = 64
EPS = 1e-6
NEG = -1e30
N_AB = (DEPTH + 1) // 2
N_C = DEPTH // 2
AB_SIZES = (H_A * DK_A, H_A * DK_A, H_A * DV_A, H_A * DV_A, H_A, H_A, H_B * HD_B, H_B * HD_B, H_B * HD_B)
D_IN_AB = sum(AB_SIZES)
AB_SPLIT = tuple(int(s) for s in np.cumsum(AB_SIZES)[:-1])
D_OUT_AB = H_A * DV_A + H_B * HD_B
C_SIZES = (CONV_DIM_C, H_V_C * DV_C, H_V_C, H_V_C)
D_IN_C = sum(C_SIZES)
C_SPLIT = tuple(int(s) for s in np.cumsum(C_SIZES)[:-1])

kernel_name = 'hybrid_mlstm_moba_gdn_convffn_step'


def _rmsnorm(x, w):
    xf = x.astype(jnp.float32)
    y = xf * lax.rsqrt(jnp.mean(xf * xf, axis=-1, keepdims=True) + EPS)
    return (y * w.astype(jnp.float32)).astype(x.dtype)


def _l2norm(x):
    xf = x.astype(jnp.float32)
    return xf * lax.rsqrt(jnp.sum(xf * xf, axis=-1, keepdims=True) + EPS)


def _causal_dwconv(u, buf, w, b=None):
    width, t = w.shape[0], u.shape[1]
    ext = jnp.concatenate([buf.astype(u.dtype), u], axis=1)
    y = ext[:, 0:t] * w[0]
    for j in range(1, width):
        y = y + ext[:, j:j + t] * w[j]
    if b is not None:
        y = y + b
    return y, ext[:, t:]


def _partial_rope(x, pos):
    half = ROPE_DIMS // 2
    inv = ROPE_THETA ** (-jnp.arange(half, dtype=jnp.float32) / half)
    ang = pos.astype(jnp.float32)[:, None] * inv[None, :]
    cos = jnp.cos(ang)[None, :, None, :]
    sin = jnp.sin(ang)[None, :, None, :]
    xr = x[..., :ROPE_DIMS].astype(jnp.float32)
    x1, x2 = xr[..., :half], xr[..., half:]
    rot = jnp.concatenate([x1 * cos - x2 * sin, x2 * cos + x1 * sin], axis=-1)
    return jnp.concatenate([rot.astype(x.dtype), x[..., ROPE_DIMS:]], axis=-1)


def _chunk_len(t):
    return CHUNK if t % CHUNK == 0 else t


def _to_chunks(a, L):
    b, t, h = a.shape[:3]
    a = a.astype(jnp.float32).reshape((b, t // L, L, h) + a.shape[3:])
    return jnp.moveaxis(a, (1, 3), (0, 2))


def _from_chunks(a):
    a = jnp.moveaxis(a, (0, 2), (1, 3))
    b, n, L, h = a.shape[:4]
    return a.reshape((b, n * L, h) + a.shape[4:])


def _mlstm_chunkwise(q, k, v, i_raw, f_raw, c0, n0, m0):
    t = q.shape[1]
    L = _chunk_len(t)
    qc = _to_chunks(q, L)
    kc = _to_chunks(k, L) * (DK_A ** -0.5)
    vc = _to_chunks(v, L)
    li = _to_chunks(i_raw, L)
    lf = jax.nn.log_sigmoid(_to_chunks(f_raw, L))
    tril = jnp.tril(jnp.ones((L, L), bool))

    def step(carry, xs):
        C, n, m = carry
        qj, kj, vj, lij, lfj = xs
        b = jnp.cumsum(lfj, axis=-1)
        D = jnp.where(tril, b[..., :, None] - b[..., None, :] + lij[..., None, :], NEG)
        inter = b + m[..., None]
        mt = jnp.maximum(inter, D.max(-1))
        s = jnp.einsum('bhtd,bhsd->bhts', qj, kj) * jnp.exp(D - mt[..., None])
        wi = jnp.exp(inter - mt)
        num = wi[..., None] * jnp.einsum('bhtd,bhde->bhte', qj, C) + jnp.einsum('bhts,bhse->bhte', s, vj)
        den = wi * jnp.einsum('bhtd,bhd->bht', qj, n) + s.sum(-1)
        h = num / jnp.maximum(jnp.abs(den), jnp.exp(-mt))[..., None]
        bL = b[..., -1]
        g = bL[..., None] - b + lij
        m_new = jnp.maximum(bL + m, g.max(-1))
        wc = jnp.exp(bL + m - m_new)
        ws = jnp.exp(g - m_new[..., None])
        C_new = wc[..., None, None] * C + jnp.einsum('bhs,bhsd,bhse->bhde', ws, kj, vj)
        n_new = wc[..., None] * n + jnp.einsum('bhs,bhsd->bhd', ws, kj)
        return (C_new, n_new, m_new), h

    init = (c0.astype(jnp.float32), n0.astype(jnp.float32), m0.astype(jnp.float32))
    (C, n, m), h = lax.scan(step, init, (qc, kc, vc, li, lf))
    return _from_chunks(h), C, n, m


def _gated_delta_chunked(q, k, v, g, beta, s0):
    t, dv = q.shape[1], v.shape[-1]
    L = _chunk_len(t)
    qc, kc, vc = _to_chunks(q, L), _to_chunks(k, L), _to_chunks(v, L)
    gc, bc = _to_chunks(g, L), _to_chunks(beta, L)
    decay = jnp.cumsum(gc, axis=-1)
    diff = decay[..., :, None] - decay[..., None, :]
    strict = jnp.tril(jnp.ones((L, L), bool), -1)
    incl = jnp.tril(jnp.ones((L, L), bool))
    kb = kc * bc[..., None]
    A = jnp.where(strict, jnp.einsum('nbhtd,nbhsd->nbhts', kb, kc) * jnp.exp(jnp.where(strict, diff, 0.0)), 0.0)
    rhs = jnp.concatenate([vc * bc[..., None], kb * jnp.exp(decay)[..., None]], axis=-1)
    sol = lax.linalg.triangular_solve(jnp.eye(L, dtype=jnp.float32) + A, rhs,
                                      left_side=True, lower=True, unit_diagonal=True)
    u, w = sol[..., :dv], sol[..., dv:]
    attn = jnp.where(incl, jnp.einsum('nbhtd,nbhsd->nbhts', qc, kc) * jnp.exp(jnp.where(incl, diff, 0.0)), 0.0)
    q_dec = qc * jnp.exp(decay)[..., None]
    k_dec = kc * jnp.exp(decay[..., -1:] - decay)[..., None]
    g_last = jnp.exp(decay[..., -1])

    def step(S, xs):
        u_j, w_j, attn_j, qd_j, kd_j, gl_j = xs
        v_new = u_j - jnp.einsum('bhtd,bhde->bhte', w_j, S)
        o = jnp.einsum('bhtd,bhde->bhte', qd_j, S) + jnp.einsum('bhts,bhse->bhte', attn_j, v_new)
        S = gl_j[..., None, None] * S + jnp.einsum('bhtd,bhte->bhde', kd_j, v_new)
        return S, o

    S, o = lax.scan(step, s0.astype(jnp.float32), (u, w, attn, q_dec, k_dec, g_last))
    return _from_chunks(o), S


def _moba_attention(q, k_all, v_all, pos0):
    b, tq, h, hd = q.shape
    L = k_all.shape[1]
    nb = -(-L // MOBA_BLOCK)
    pad = nb * MOBA_BLOCK - L
    kb = jnp.pad(k_all, ((0, 0), (0, pad), (0, 0), (0, 0))).reshape(b, nb, MOBA_BLOCK, h, hd)
    vb = jnp.pad(v_all, ((0, 0), (0, pad), (0, 0), (0, 0))).reshape(b, nb, MOBA_BLOCK, h, hd)
    kmean = jnp.mean(kb.astype(jnp.float32), axis=2)
    topk = min(MOBA_TOPK, nb)
    qc = MOBA_Q_CHUNK if tq % MOBA_Q_CHUNK == 0 else tq
    nq = tq // qc
    q_chunks = jnp.moveaxis(q.reshape(b, nq, qc, h, hd), 1, 0)
    pos = (pos0 + jnp.arange(tq, dtype=jnp.int32)).reshape(nq, qc)
    bi = jnp.arange(b)[:, None, None, None]
    hi = jnp.arange(h)[None, :, None, None]
    blk_ids = jnp.arange(nb, dtype=jnp.int32)
    offs = jnp.arange(MOBA_BLOCK, dtype=jnp.int32)
    scale = hd ** -0.5

    def attend(args):
        qj, pj = args
        own = pj // MOBA_BLOCK
        qf = qj.astype(jnp.float32)
        gate = jnp.einsum('bqhd,bnhd->bhqn', qf, kmean)
        past = blk_ids[None, :] < own[:, None]
        gate = jnp.where(past[None, None], gate, NEG)
        _, top = lax.top_k(gate, topk)
        top_ok = top < own[None, None, :, None]
        own_b = jnp.broadcast_to(own[None, None, :, None], (b, h, qc, 1))
        idx = jnp.concatenate([top, own_b], axis=-1)
        ok_blk = jnp.concatenate([top_ok, jnp.ones((b, h, qc, 1), bool)], axis=-1)
        k_sel = kb[bi, idx, :, hi, :].astype(jnp.float32)
        v_sel = vb[bi, idx, :, hi, :].astype(jnp.float32)
        logits = jnp.einsum('bqhd,bhqsjd->bhqsj', qf, k_sel) * scale
        key_pos = idx[..., None] * MOBA_BLOCK + offs
        ok = ok_blk[..., None] & (key_pos <= pj[None, None, :, None, None])
        logits = jnp.where(ok, logits, NEG).reshape(b, h, qc, -1)
        p = jax.nn.softmax(logits, axis=-1).reshape(b, h, qc, -1, MOBA_BLOCK)
        return jnp.einsum('bhqsj,bhqsjd->bqhd', p, v_sel).astype(q.dtype)

    out = lax.map(attend, (q_chunks, pos))
    return jnp.moveaxis(out, 0, 1).reshape(b, tq, h, hd)


def _gather_pages(pool, layer, page_table):
    rows = pool[layer, page_table]
    db, npg = page_table.shape
    return rows.reshape(db, npg * PAGE_SIZE, rows.shape[-2], rows.shape[-1])


def _mixer_ab(xn, pos0, w_in, b_i, b_f, w_mh, w_out, c0, n0, m0, k_past, v_past):
    b, t, _ = xn.shape
    proj = xn @ w_in
    qa, ka, va, oa, ia, fa, qb, kb, vb = jnp.split(proj, AB_SPLIT, axis=-1)
    h, c, n, m = _mlstm_chunkwise(qa.reshape(b, t, H_A, DK_A), ka.reshape(b, t, H_A, DK_A),
                                  va.reshape(b, t, H_A, DV_A), ia + b_i, fa + b_f, c0, n0, m0)
    h = _rmsnorm(h, w_mh.reshape(H_A, DV_A)).reshape(b, t, H_A * DV_A) * jax.nn.sigmoid(oa.astype(jnp.float32))
    pos = pos0 + jnp.arange(t, dtype=jnp.int32)
    qb = _partial_rope(qb.reshape(b, t, H_B, HD_B), pos)
    kb = _partial_rope(kb.reshape(b, t, H_B, HD_B), pos)
    vb = vb.reshape(b, t, H_B, HD_B)
    if k_past is None:
        k_all, v_all = kb, vb
    else:
        k_all = jnp.concatenate([k_past.astype(kb.dtype), kb], axis=1)
        v_all = jnp.concatenate([v_past.astype(vb.dtype), vb], axis=1)
    o_b = _moba_attention(qb, k_all, v_all, pos0).reshape(b, t, H_B * HD_B)
    out = jnp.concatenate([h.astype(xn.dtype), o_b.astype(xn.dtype)], axis=-1) @ w_out
    return out, c, n, m, kb, vb


def _mixer_c(xn, w_in, conv_w, a_log, dt_bias, w_norm, w_out, s0, conv_buf):
    b, t, _ = xn.shape
    proj = xn @ w_in
    qkv, z, bt, a = jnp.split(proj, C_SPLIT, axis=-1)
    qkv, conv_new = _causal_dwconv(qkv, conv_buf, conv_w)
    qkv = jax.nn.silu(qkv)
    q, k, v = jnp.split(qkv, (H_QK_C * DK_C, 2 * H_QK_C * DK_C), axis=-1)
    rep = H_V_C // H_QK_C
    q = jnp.repeat(_l2norm(q.reshape(b, t, H_QK_C, DK_C)) * (DK_C ** -0.5), rep, axis=2)
    k = jnp.repeat(_l2norm(k.reshape(b, t, H_QK_C, DK_C)), rep, axis=2)
    v = v.reshape(b, t, H_V_C, DV_C)
    beta = jax.nn.sigmoid(bt.astype(jnp.float32))
    g = -jnp.exp(a_log.astype(jnp.float32)) * jax.nn.softplus(a.astype(jnp.float32) + dt_bias.astype(jnp.float32))
    o, s = _gated_delta_chunked(q, k, v, g, beta, s0)
    o = _rmsnorm(o, w_norm) * jax.nn.silu(z.reshape(b, t, H_V_C, DV_C).astype(jnp.float32))
    out = o.reshape(b, t, H_V_C * DV_C).astype(xn.dtype) @ w_out
    return out, s, conv_new


def _conv_ffn(xn, w_up, conv_w, conv_b, w_down, buf):
    u = xn @ w_up
    u, buf_new = _causal_dwconv(u, buf, conv_w, conv_b)
    gt, up = jnp.split(u, 2, axis=-1)
    return (jax.nn.silu(gt) * up) @ w_down, buf_new


def _trunk(x, pos0, c0, n0, m0, s0, gconv0, fconv0, kv_pages, params):
    (norm_mix, norm_ffn, norm_final, ab_w_in, ab_b_i, ab_b_f, ab_w_mh, ab_w_out,
     c_w_in, c_conv_w, c_a_log, c_dt_bias, c_w_norm, c_w_out,
     ffn_w_up, ffn_conv_w, ffn_conv_b, ffn_w_down) = params
    rows_k, rows_v, cs, ns, ms, ss, gcs, fcs = [], [], [], [], [], [], [], []
    h = x
    for l in range(DEPTH):
        j = l // 2
        xn = _rmsnorm(h, norm_mix[l])
        if l % 2 == 0:
            if kv_pages is None:
                k_past, v_past = None, None
            else:
                cache_k, cache_v, page_table = kv_pages
                k_past = _gather_pages(cache_k, j, page_table)
                v_past = _gather_pages(cache_v, j, page_table)
            out, c, n, m, kn, vn = _mixer_ab(xn, pos0, ab_w_in[j], ab_b_i[j], ab_b_f[j], ab_w_mh[j], ab_w_out[j],
                                             c0[j], n0[j], m0[j], k_past, v_past)
            rows_k.append(kn)
            rows_v.append(vn)
            cs.append(c)
            ns.append(n)
            ms.append(m)
        else:
            out, s, gc = _mixer_c(xn, c_w_in[j], c_conv_w[j], c_a_log[j], c_dt_bias[j], c_w_norm[j], c_w_out[j],
                                  s0[j], gconv0[j])
            ss.append(s)
            gcs.append(gc)
        h = h + out
        f, fc = _conv_ffn(_rmsnorm(h, norm_ffn[l]), ffn_w_up[l], ffn_conv_w[l], ffn_conv_b[l], ffn_w_down[l], fconv0[l])
        fcs.append(fc)
        h = h + f
    y = _rmsnorm(h, norm_final)
    return (y, jnp.stack(rows_k), jnp.stack(rows_v), jnp.stack(cs), jnp.stack(ns), jnp.stack(ms),
            jnp.stack(ss), jnp.stack(gcs), jnp.stack(fcs))


def setup_inputs(seed: int = 0) -> dict:
    key = jax.random.key(seed)
    ks = iter(jax.random.split(key, 48))
    f32 = jnp.float32

    def nrm(shape, scale):
        return jax.random.normal(next(ks), shape, f32) * scale

    n_pages = PAST_LEN // PAGE_SIZE
    n_pool = (DEC_BATCH * n_pages * 5) // 4
    page_table = jax.random.permutation(next(ks), n_pool)[:DEC_BATCH * n_pages]
    page_table = page_table.reshape(DEC_BATCH, n_pages).astype(jnp.int32)
    a_log = jnp.log(jax.random.uniform(next(ks), (N_C, H_V_C), f32, 1.0, 16.0))
    dt = jnp.exp(jax.random.uniform(next(ks), (N_C, H_V_C), f32, math.log(1e-3), math.log(0.1)))
    dt_bias = dt + jnp.log(-jnp.expm1(-dt))
    return {
        'x_prompt': nrm((BATCH, SEQ, D_MODEL), 1.0),
        'x_sample': nrm((DEC_BATCH, DEC_SEQ, D_MODEL), 1.0),
        'cache_k': nrm((N_AB, n_pool, PAGE_SIZE, H_B, HD_B), 1.0),
        'cache_v': nrm((N_AB, n_pool, PAGE_SIZE, H_B, HD_B), 1.0),
        'page_table': page_table,
        'state_mlstm_c': nrm((N_AB, DEC_BATCH, H_A, DK_A, DV_A), 0.5),
        'state_mlstm_n': nrm((N_AB, DEC_BATCH, H_A, DK_A), 0.5),
        'state_mlstm_m': nrm((N_AB, DEC_BATCH, H_A), 1.0),
        'state_gdn_s': nrm((N_C, DEC_BATCH, H_V_C, DK_C, DV_C), 0.1),
        'state_gdn_conv': nrm((N_C, DEC_BATCH, CONV_C - 1, CONV_DIM_C), 1.0),
        'state_ffn_conv': nrm((DEPTH, DEC_BATCH, FFN_CONV - 1, 2 * D_FF), 1.0),
        'norm_mix': 1.0 + nrm((DEPTH, D_MODEL), 0.02),
        'norm_ffn': 1.0 + nrm((DEPTH, D_MODEL), 0.02),
        'norm_final': 1.0 + nrm((D_MODEL,), 0.02),
        'ab_w_in': nrm((N_AB, D_MODEL, D_IN_AB), D_MODEL ** -0.5),
        'ab_b_i': nrm((N_AB, H_A), 0.1),
        'ab_b_f': 3.0 + nrm((N_AB, H_A), 0.1),
        'ab_w_mh': 1.0 + nrm((N_AB, H_A * DV_A), 0.02),
        'ab_w_out': nrm((N_AB, D_OUT_AB, D_MODEL), D_OUT_AB ** -0.5),
        'c_w_in': nrm((N_C, D_MODEL, D_IN_C), D_MODEL ** -0.5),
        'c_conv_w': nrm((N_C, CONV_C, CONV_DIM_C), CONV_C ** -0.5),
        'c_a_log': a_log,
        'c_dt_bias': dt_bias,
        'c_w_norm': 1.0 + nrm((N_C, DV_C), 0.02),
        'c_w_out': nrm((N_C, H_V_C * DV_C, D_MODEL), (H_V_C * DV_C) ** -0.5),
        'ffn_w_up': nrm((DEPTH, D_MODEL, 2 * D_FF), D_MODEL ** -0.5),
        'ffn_conv_w': nrm((DEPTH, FFN_CONV, 2 * D_FF), FFN_CONV ** -0.5),
        'ffn_conv_b': nrm((DEPTH, 2 * D_FF), 0.02),
        'ffn_w_down': nrm((DEPTH, D_FF, D_MODEL), D_FF ** -0.5),
    }


def reference(x_prompt, x_sample, cache_k, cache_v, page_table, state_mlstm_c, state_mlstm_n, state_mlstm_m,
              state_gdn_s, state_gdn_conv, state_ffn_conv, norm_mix, norm_ffn, norm_final,
              ab_w_in, ab_b_i, ab_b_f, ab_w_mh, ab_w_out, c_w_in, c_conv_w, c_a_log, c_dt_bias, c_w_norm, c_w_out,
              ffn_w_up, ffn_conv_w, ffn_conv_b, ffn_w_down):
    params = (norm_mix, norm_ffn, norm_final, ab_w_in, ab_b_i, ab_b_f, ab_w_mh, ab_w_out,
              c_w_in, c_conv_w, c_a_log, c_dt_bias, c_w_norm, c_w_out,
              ffn_w_up, ffn_conv_w, ffn_conv_b, ffn_w_down)
    f32 = jnp.float32
    zc = jnp.zeros((N_AB, BATCH, H_A, DK_A, DV_A), f32)
    zn = jnp.zeros((N_AB, BATCH, H_A, DK_A), f32)
    zm = jnp.zeros((N_AB, BATCH, H_A), f32)
    zs = jnp.zeros((N_C, BATCH, H_V_C, DK_C, DV_C), f32)
    zg = jnp.zeros((N_C, BATCH, CONV_C - 1, CONV_DIM_C), x_prompt.dtype)
    zf = jnp.zeros((DEPTH, BATCH, FFN_CONV - 1, 2 * D_FF), x_prompt.dtype)
    (y_prompt, k_rows_p, v_rows_p, mlstm_c_p, mlstm_n_p, mlstm_m_p,
     gdn_s_p, gdn_conv_p, ffn_conv_p) = _trunk(x_prompt, 0, zc, zn, zm, zs, zg, zf, None, params)
    (y_sample, k_rows_s, v_rows_s, mlstm_c_s, mlstm_n_s, mlstm_m_s,
     gdn_s_s, gdn_conv_s, ffn_conv_s) = _trunk(x_sample, PAST_LEN, state_mlstm_c, state_mlstm_n, state_mlstm_m,
                                               state_gdn_s, state_gdn_conv, state_ffn_conv,
                                               (cache_k, cache_v, page_table), params)
    return (y_prompt, y_sample,
            k_rows_p, v_rows_p, mlstm_c_p, mlstm_n_p, mlstm_m_p, gdn_s_p, gdn_conv_p, ffn_conv_p,
            k_rows_s, v_rows_s, mlstm_c_s, mlstm_n_s, mlstm_m_s, gdn_s_s, gdn_conv_s, ffn_conv_s)
```

```python
import functools
import math

import jax
import jax.numpy as jnp
import numpy as np
from jax import lax
from jax.experimental import pallas as pl
from jax.experimental.pallas import tpu as pltpu

D_MODEL = 2048
DEPTH = 4
PAGE_SIZE = 128
H_A, DK_A, DV_A = 8, 64, 128
H_B, HD_B = 8, 128
MOBA_BLOCK, MOBA_TOPK, MOBA_Q_CHUNK = 256, 3, 32
ROPE_DIMS = HD_B // 4
ROPE_THETA = 500000.0
H_QK_C, H_V_C, DK_C, DV_C, CONV_C = 16, 32, 128, 128, 4
CONV_DIM_C = 2 * H_QK_C * DK_C + H_V_C * DV_C
D_FF = 5504
FFN_CONV = 3
CHUNK = 64
EPS = 1e-6
NEG = -1e30
A_COLS = 2 * H_A * DK_A + 2 * H_A * DV_A
B_COLS = 3 * H_B * HD_B
GATE_COLS = 2 * H_A

F32 = jnp.float32
BF16 = jnp.bfloat16
VMEM_LIMIT_BYTES = 56 * 1024 * 1024
HALO = 16


def _params(*sem):
    return pltpu.CompilerParams(dimension_semantics=sem, vmem_limit_bytes=VMEM_LIMIT_BYTES)


def _rms(x, w):
    return x * lax.rsqrt(jnp.mean(x * x, axis=-1, keepdims=True) + EPS) * w


def _bdot(a, b):
    return jnp.dot(a, b, preferred_element_type=F32)


def _nmm_kernel(x_ref, nw_ref, w_ref, wt_ref, o_ref, ot_ref, xn_ref):
    @pl.when(pl.program_id(1) == 0)
    def _():
        xn = _rms(x_ref[...], nw_ref[...]).astype(BF16)
        xn_ref[...] = xn
        ot_ref[...] = _bdot(xn, wt_ref[...].astype(BF16))

    o_ref[...] = _bdot(xn_ref[...], w_ref[...].astype(BF16))


def _norm_matmul(x, nw, w, n_main, w_tail, *, tm, tn):
    m, k = x.shape
    nt = w_tail.shape[1]
    return pl.pallas_call(
        _nmm_kernel,
        grid=(m // tm, n_main // tn),
        in_specs=[pl.BlockSpec((tm, k), lambda i, j: (i, 0)),
                  pl.BlockSpec((1, k), lambda i, j: (0, 0)),
                  pl.BlockSpec((k, tn), lambda i, j: (0, j)),
                  pl.BlockSpec((k, nt), lambda i, j: (0, 0))],
        out_specs=[pl.BlockSpec((tm, tn), lambda i, j: (i, j)),
                   pl.BlockSpec((tm, nt), lambda i, j: (i, 0))],
        out_shape=[jax.ShapeDtypeStruct((m, n_main), F32), jax.ShapeDtypeStruct((m, nt), F32)],
        scratch_shapes=[pltpu.VMEM((tm, k), BF16)],
        compiler_params=_params("parallel", "arbitrary"),
        name="norm_matmul",
    )(x, nw.reshape(1, k), w, w_tail)


def _mmr_kernel(a_ref, w_ref, r_ref, o_ref):
    o_ref[...] = r_ref[...] + _bdot(a_ref[...], w_ref[...].astype(BF16))


def _matmul_res(a, w, res, *, tm, tn):
    m, k = a.shape
    n = w.shape[1]
    return pl.pallas_call(
        _mmr_kernel,
        grid=(m // tm, n // tn),
        in_specs=[pl.BlockSpec((tm, k), lambda i, j: (i, 0)),
                  pl.BlockSpec((k, tn), lambda i, j: (0, j)),
                  pl.BlockSpec((tm, tn), lambda i, j: (i, j))],
        out_specs=pl.BlockSpec((tm, tn), lambda i, j: (i, j)),
        out_shape=jax.ShapeDtypeStruct((m, n), F32),
        compiler_params=_params("parallel", "arbitrary"),
        name="matmul_res",
    )(a, w, res)


def _ffn_up_kernel(xh_ref, x_ref, nw_ref, wg_ref, wu_ref, cwg_ref, cwu_ref, cbg_ref, cbu_ref,
                   act_ref, lastg_ref, lastu_ref, xn_ref, *, tiles_per_seq):
    i = pl.program_id(0)

    @pl.when(pl.program_id(1) == 0)
    def _():
        nw = nw_ref[...]
        xn_ref[HALO:, :] = _rms(x_ref[...], nw).astype(BF16)
        halo = jnp.where(i % tiles_per_seq == 0, 0.0, _rms(xh_ref[...], nw))
        xn_ref[:HALO, :] = halo.astype(BF16)

    xn = xn_ref[...]

    def branch(w_ref, cw_ref, cb_ref, last_ref):
        v = _bdot(xn, w_ref[...].astype(BF16))
        last_ref[...] = v[v.shape[0] - 8:, :]
        cw = cw_ref[...]
        y = pltpu.roll(v, 2, 0) * cw[0:1] + pltpu.roll(v, 1, 0) * cw[1:2] + v * cw[2:3] + cb_ref[...]
        return y[HALO:, :]

    g = branch(wg_ref, cwg_ref, cbg_ref, lastg_ref)
    u = branch(wu_ref, cwu_ref, cbu_ref, lastu_ref)
    act_ref[...] = (g / (1.0 + jnp.exp(-g)) * u).astype(BF16)


def _ffn_up(x, nw, w_up, conv_w, conv_b, seq_len, *, tm, tn):
    m, k = x.shape
    wg, wu = w_up[:, :D_FF], w_up[:, D_FF:]
    cwg, cwu = conv_w[:, :D_FF], conv_w[:, D_FF:]
    cbg, cbu = conv_b[:D_FF].reshape(1, D_FF), conv_b[D_FF:].reshape(1, D_FF)
    n_i, n_j = m // tm, pl.cdiv(D_FF, tn)
    hb = tm // HALO
    wspec = pl.BlockSpec((k, tn), lambda i, j: (0, j))
    cwspec = pl.BlockSpec((FFN_CONV, tn), lambda i, j: (0, j))
    cbspec = pl.BlockSpec((1, tn), lambda i, j: (0, j))
    lastspec = pl.BlockSpec((None, 8, tn), lambda i, j: (i, 0, j))
    act, lastg, lastu = pl.pallas_call(
        functools.partial(_ffn_up_kernel, tiles_per_seq=seq_len // tm),
        grid=(n_i, n_j),
        in_specs=[pl.BlockSpec((HALO, k), lambda i, j: (jnp.maximum(i * hb - 1, 0), 0)),
                  pl.BlockSpec((tm, k), lambda i, j: (i, 0)),
                  pl.BlockSpec((1, k), lambda i, j: (0, 0)),
                  wspec, wspec, cwspec, cwspec, cbspec, cbspec],
        out_specs=[pl.BlockSpec((tm, tn), lambda i, j: (i, j)), lastspec, lastspec],
        out_shape=[jax.ShapeDtypeStruct((m, D_FF), BF16),
                   jax.ShapeDtypeStruct((n_i, 8, D_FF), F32),
                   jax.ShapeDtypeStruct((n_i, 8, D_FF), F32)],
        scratch_shapes=[pltpu.VMEM((HALO + tm, k), BF16)],
        compiler_params=_params("parallel", "arbitrary"),
        name="ffn_up",
    )(x, x, nw.reshape(1, k), wg, wu, cwg, cwu, cbg, cbu)
    return act, lastg, lastu


def _l2norm(x):
    xf = x.astype(jnp.float32)
    return xf * lax.rsqrt(jnp.sum(xf * xf, axis=-1, keepdims=True) + EPS)


def _rmsnorm(x, w):
    xf = x.astype(jnp.float32)
    y = xf * lax.rsqrt(jnp.mean(xf * xf, axis=-1, keepdims=True) + EPS)
    return (y * w.astype(jnp.float32)).astype(x.dtype)


def _causal_dwconv(u, buf, w, b=None):
    width, t = w.shape[0], u.shape[1]
    ext = jnp.concatenate([buf.astype(u.dtype), u], axis=1)
    y = ext[:, 0:t] * w[0]
    for j in range(1, width):
        y = y + ext[:, j:j + t] * w[j]
    if b is not None:
        y = y + b
    return y, ext[:, t:]


def _partial_rope(x, pos):
    half = ROPE_DIMS // 2
    inv = ROPE_THETA ** (-jnp.arange(half, dtype=jnp.float32) / half)
    ang = pos.astype(jnp.float32)[:, None] * inv[None, :]
    cos = jnp.cos(ang)[None, :, None, :]
    sin = jnp.sin(ang)[None, :, None, :]
    xr = x[..., :ROPE_DIMS].astype(jnp.float32)
    x1, x2 = xr[..., :half], xr[..., half:]
    rot = jnp.concatenate([x1 * cos - x2 * sin, x2 * cos + x1 * sin], axis=-1)
    return jnp.concatenate([rot.astype(x.dtype), x[..., ROPE_DIMS:]], axis=-1)


def _chunk_len(t):
    return CHUNK if t % CHUNK == 0 else t


def _to_chunks(a, L):
    b, t, h = a.shape[:3]
    a = a.astype(jnp.float32).reshape((b, t // L, L, h) + a.shape[3:])
    return jnp.moveaxis(a, (1, 3), (0, 2))


def _from_chunks(a):
    a = jnp.moveaxis(a, (0, 2), (1, 3))
    b, n, L, h = a.shape[:4]
    return a.reshape((b, n * L, h) + a.shape[4:])


def _mlstm_chunkwise(q, k, v, i_raw, f_raw, c0, n0, m0):
    t = q.shape[1]
    L = _chunk_len(t)
    qc = _to_chunks(q, L)
    kc = _to_chunks(k, L) * (DK_A ** -0.5)
    vc = _to_chunks(v, L)
    li = _to_chunks(i_raw, L)
    lf = jax.nn.log_sigmoid(_to_chunks(f_raw, L))
    tril = jnp.tril(jnp.ones((L, L), bool))

    def step(carry, xs):
        C, n, m = carry
        qj, kj, vj, lij, lfj = xs
        b = jnp.cumsum(lfj, axis=-1)
        D = jnp.where(tril, b[..., :, None] - b[..., None, :] + lij[..., None, :], NEG)
        inter = b + m[..., None]
        mt = jnp.maximum(inter, D.max(-1))
        s = jnp.einsum('bhtd,bhsd->bhts', qj, kj) * jnp.exp(D - mt[..., None])
        wi = jnp.exp(inter - mt)
        num = wi[..., None] * jnp.einsum('bhtd,bhde->bhte', qj, C) + jnp.einsum('bhts,bhse->bhte', s, vj)
        den = wi * jnp.einsum('bhtd,bhd->bht', qj, n) + s.sum(-1)
        h = num / jnp.maximum(jnp.abs(den), jnp.exp(-mt))[..., None]
        bL = b[..., -1]
        g = bL[..., None] - b + lij
        m_new = jnp.maximum(bL + m, g.max(-1))
        wc = jnp.exp(bL + m - m_new)
        ws = jnp.exp(g - m_new[..., None])
        C_new = wc[..., None, None] * C + jnp.einsum('bhs,bhsd,bhse->bhde', ws, kj, vj)
        n_new = wc[..., None] * n + jnp.einsum('bhs,bhsd->bhd', ws, kj)
        return (C_new, n_new, m_new), h

    init = (c0.astype(jnp.float32), n0.astype(jnp.float32), m0.astype(jnp.float32))
    (C, n, m), h = lax.scan(step, init, (qc, kc, vc, li, lf))
    return _from_chunks(h), C, n, m


def _gated_delta_chunked(q, k, v, g, beta, s0):
    t, dv = q.shape[1], v.shape[-1]
    L = _chunk_len(t)
    qc, kc, vc = _to_chunks(q, L), _to_chunks(k, L), _to_chunks(v, L)
    gc, bc = _to_chunks(g, L), _to_chunks(beta, L)
    decay = jnp.cumsum(gc, axis=-1)
    diff = decay[..., :, None] - decay[..., None, :]
    strict = jnp.tril(jnp.ones((L, L), bool), -1)
    incl = jnp.tril(jnp.ones((L, L), bool))
    kb = kc * bc[..., None]
    A = jnp.where(strict, jnp.einsum('nbhtd,nbhsd->nbhts', kb, kc) * jnp.exp(jnp.where(strict, diff, 0.0)), 0.0)
    rhs = jnp.concatenate([vc * bc[..., None], kb * jnp.exp(decay)[..., None]], axis=-1)
    sol = lax.linalg.triangular_solve(jnp.eye(L, dtype=jnp.float32) + A, rhs,
                                      left_side=True, lower=True, unit_diagonal=True)
    u, w = sol[..., :dv], sol[..., dv:]
    attn = jnp.where(incl, jnp.einsum('nbhtd,nbhsd->nbhts', qc, kc) * jnp.exp(jnp.where(incl, diff, 0.0)), 0.0)
    q_dec = qc * jnp.exp(decay)[..., None]
    k_dec = kc * jnp.exp(decay[..., -1:] - decay)[..., None]
    g_last = jnp.exp(decay[..., -1])

    def step(S, xs):
        u_j, w_j, attn_j, qd_j, kd_j, gl_j = xs
        v_new = u_j - jnp.einsum('bhtd,bhde->bhte', w_j, S)
        o = jnp.einsum('bhtd,bhde->bhte', qd_j, S) + jnp.einsum('bhts,bhse->bhte', attn_j, v_new)
        S = gl_j[..., None, None] * S + jnp.einsum('bhtd,bhte->bhde', kd_j, v_new)
        return S, o

    S, o = lax.scan(step, s0.astype(jnp.float32), (u, w, attn, q_dec, k_dec, g_last))
    return _from_chunks(o), S


def _moba_attention(q, k_all, v_all, pos0):
    b, tq, h, hd = q.shape
    L = k_all.shape[1]
    nb = -(-L // MOBA_BLOCK)
    pad = nb * MOBA_BLOCK - L
    kb = jnp.pad(k_all, ((0, 0), (0, pad), (0, 0), (0, 0))).reshape(b, nb, MOBA_BLOCK, h, hd)
    vb = jnp.pad(v_all, ((0, 0), (0, pad), (0, 0), (0, 0))).reshape(b, nb, MOBA_BLOCK, h, hd)
    kmean = jnp.mean(kb.astype(jnp.float32), axis=2)
    topk = min(MOBA_TOPK, nb)
    qc = MOBA_Q_CHUNK if tq % MOBA_Q_CHUNK == 0 else tq
    nq = tq // qc
    q_chunks = jnp.moveaxis(q.reshape(b, nq, qc, h, hd), 1, 0)
    pos = (pos0 + jnp.arange(tq, dtype=jnp.int32)).reshape(nq, qc)
    bi = jnp.arange(b)[:, None, None, None]
    hi = jnp.arange(h)[None, :, None, None]
    blk_ids = jnp.arange(nb, dtype=jnp.int32)
    offs = jnp.arange(MOBA_BLOCK, dtype=jnp.int32)
    scale = hd ** -0.5

    def attend(args):
        qj, pj = args
        own = pj // MOBA_BLOCK
        qf = qj.astype(jnp.float32)
        gate = jnp.einsum('bqhd,bnhd->bhqn', qf, kmean)
        past = blk_ids[None, :] < own[:, None]
        gate = jnp.where(past[None, None], gate, NEG)
        _, top = lax.top_k(gate, topk)
        top_ok = top < own[None, None, :, None]
        own_b = jnp.broadcast_to(own[None, None, :, None], (b, h, qc, 1))
        idx = jnp.concatenate([top, own_b], axis=-1)
        ok_blk = jnp.concatenate([top_ok, jnp.ones((b, h, qc, 1), bool)], axis=-1)
        k_sel = kb[bi, idx, :, hi, :].astype(jnp.float32)
        v_sel = vb[bi, idx, :, hi, :].astype(jnp.float32)
        logits = jnp.einsum('bqhd,bhqsjd->bhqsj', qf, k_sel) * scale
        key_pos = idx[..., None] * MOBA_BLOCK + offs
        ok = ok_blk[..., None] & (key_pos <= pj[None, None, :, None, None])
        logits = jnp.where(ok, logits, NEG).reshape(b, h, qc, -1)
        p = jax.nn.softmax(logits, axis=-1).reshape(b, h, qc, -1, MOBA_BLOCK)
        return jnp.einsum('bhqsj,bhqsjd->bqhd', p, v_sel).astype(q.dtype)

    out = lax.map(attend, (q_chunks, pos))
    return jnp.moveaxis(out, 0, 1).reshape(b, tq, h, hd)


def _gather_pages(pool, layer, page_table):
    rows = pool[layer, page_table]
    db, npg = page_table.shape
    return rows.reshape(db, npg * PAGE_SIZE, rows.shape[-2], rows.shape[-1])


def _mixer_ab_core(main, gates, b, t, pos0, b_i, b_f, w_mh, c0, n0, m0, k_past, v_past):
    qa = main[:, 0:512].reshape(b, t, H_A, DK_A)
    ka = main[:, 512:1024].reshape(b, t, H_A, DK_A)
    va = main[:, 1024:2048].reshape(b, t, H_A, DV_A)
    oa = main[:, 2048:3072].reshape(b, t, H_A * DV_A)
    qb = main[:, A_COLS:A_COLS + 1024].reshape(b, t, H_B, HD_B)
    kb = main[:, A_COLS + 1024:A_COLS + 2048].reshape(b, t, H_B, HD_B)
    vb = main[:, A_COLS + 2048:A_COLS + 3072].reshape(b, t, H_B, HD_B)
    ia = gates[:, :H_A].reshape(b, t, H_A)
    fa = gates[:, H_A:].reshape(b, t, H_A)
    h, c, n, m = _mlstm_chunkwise(qa, ka, va, ia + b_i, fa + b_f, c0, n0, m0)
    h = _rmsnorm(h, w_mh.reshape(H_A, DV_A)).reshape(b, t, H_A * DV_A) * jax.nn.sigmoid(oa)
    pos = pos0 + jnp.arange(t, dtype=jnp.int32)
    qb = _partial_rope(qb, pos)
    kb = _partial_rope(kb, pos)
    if k_past is None:
        k_all, v_all = kb, vb
    else:
        k_all = jnp.concatenate([k_past, kb], axis=1)
        v_all = jnp.concatenate([v_past, vb], axis=1)
    o_b = _moba_attention(qb, k_all, v_all, pos0).reshape(b, t, H_B * HD_B)
    cat = jnp.concatenate([h, o_b], axis=-1).reshape(b * t, 2 * H_B * HD_B).astype(BF16)
    return cat, c, n, m, kb, vb


def _mixer_c_core(qkv, z, tail, b, t, conv_w, a_log, dt_bias, w_norm, s0, conv_buf):
    qkv = qkv.reshape(b, t, CONV_DIM_C)
    bt = tail[:, :H_V_C].reshape(b, t, H_V_C)
    a = tail[:, H_V_C:].reshape(b, t, H_V_C)
    qkv, conv_new = _causal_dwconv(qkv, conv_buf, conv_w)
    qkv = jax.nn.silu(qkv)
    q, k, v = jnp.split(qkv, (H_QK_C * DK_C, 2 * H_QK_C * DK_C), axis=-1)
    rep = H_V_C // H_QK_C
    q = jnp.repeat(_l2norm(q.reshape(b, t, H_QK_C, DK_C)) * (DK_C ** -0.5), rep, axis=2)
    k = jnp.repeat(_l2norm(k.reshape(b, t, H_QK_C, DK_C)), rep, axis=2)
    v = v.reshape(b, t, H_V_C, DV_C)
    beta = jax.nn.sigmoid(bt)
    g = -jnp.exp(a_log) * jax.nn.softplus(a + dt_bias)
    o, s = _gated_delta_chunked(q, k, v, g, beta, s0)
    o = _rmsnorm(o, w_norm) * jax.nn.silu(z.reshape(b, t, H_V_C, DV_C))
    return o.reshape(b * t, H_V_C * DV_C).astype(BF16), s, conv_new


def _trunk(x, pos0, c0, n0, m0, s0, gconv0, fconv0, kv_pages, params, *, tm):
    (norm_mix, norm_ffn, norm_final, ab_w_in, ab_b_i, ab_b_f, ab_w_mh, ab_w_out,
     c_w_in, c_conv_w, c_a_log, c_dt_bias, c_w_norm, c_w_out,
     ffn_w_up, ffn_conv_w, ffn_conv_b, ffn_w_down) = params
    b, t, d = x.shape
    prompt = kv_pages is None
    rows_k, rows_v, cs, ns, ms, ss, gcs, fcs = [], [], [], [], [], [], [], []
    h = x.reshape(b * t, d)
    for l in range(DEPTH):
        j = l // 2
        if l % 2 == 0:
            w = ab_w_in[j]
            w_main = jnp.concatenate([w[:, :A_COLS], w[:, A_COLS + GATE_COLS:]], axis=1)
            w_gate = w[:, A_COLS:A_COLS + GATE_COLS]
            main, gates = _norm_matmul(h, norm_mix[l], w_main, A_COLS + B_COLS, w_gate, tm=tm, tn=512)
            if prompt:
                k_past, v_past = None, None
            else:
                cache_k, cache_v, page_table = kv_pages
                k_past = _gather_pages(cache_k, j, page_table)
                v_past = _gather_pages(cache_v, j, page_table)
            cat, c, n, m, kn, vn = _mixer_ab_core(main, gates, b, t, pos0, ab_b_i[j], ab_b_f[j], ab_w_mh[j],
                                                  c0[j], n0[j], m0[j], k_past, v_past)
            rows_k.append(kn)
            rows_v.append(vn)
            cs.append(c)
            ns.append(n)
            ms.append(m)
            h = _matmul_res(cat, ab_w_out[j], h, tm=tm, tn=512)
        else:
            w = c_w_in[j]
            n_main = CONV_DIM_C + H_V_C * DV_C
            main, tail = _norm_matmul(h, norm_mix[l], w, n_main, w[:, n_main:], tm=tm, tn=512)
            o, s, gc = _mixer_c_core(main[:, :CONV_DIM_C], main[:, CONV_DIM_C:], tail, b, t, c_conv_w[j], c_a_log[j],
                                     c_dt_bias[j], c_w_norm[j], s0[j], gconv0[j])
            ss.append(s)
            gcs.append(gc)
            h = _matmul_res(o, c_w_out[j], h, tm=tm, tn=256)
        if prompt:
            act, lastg, lastu = _ffn_up(h, norm_ffn[l], ffn_w_up[l], ffn_conv_w[l], ffn_conv_b[l], t, tm=tm, tn=512)
            tps = t // tm
            last = jnp.concatenate([lastg[tps - 1::tps, 8 - (FFN_CONV - 1):], lastu[tps - 1::tps, 8 - (FFN_CONV - 1):]],
                                   axis=-1)
            fcs.append(last)
        else:
            w = ffn_w_up[l]
            n_main = (2 * D_FF) // 512 * 512
            main, tail = _norm_matmul(h, norm_ffn[l], w, n_main, w[:, n_main:], tm=tm, tn=512)
            u = jnp.concatenate([main, tail], axis=-1).reshape(b, t, 2 * D_FF)
            u, fc = _causal_dwconv(u, fconv0[l], ffn_conv_w[l], ffn_conv_b[l])
            fcs.append(fc)
            gt, up = jnp.split(u.reshape(b * t, 2 * D_FF), 2, axis=-1)
            act = (jax.nn.silu(gt) * up).astype(BF16)
        h = _matmul_res(act, ffn_w_down[l], h, tm=tm, tn=256)
    y = _rmsnorm(h, norm_final).reshape(b, t, d)
    return (y, jnp.stack(rows_k), jnp.stack(rows_v), jnp.stack(cs), jnp.stack(ns), jnp.stack(ms),
            jnp.stack(ss), jnp.stack(gcs), jnp.stack(fcs))


def kernel(x_prompt, x_sample, cache_k, cache_v, page_table, state_mlstm_c, state_mlstm_n, state_mlstm_m,
           state_gdn_s, state_gdn_conv, state_ffn_conv, norm_mix, norm_ffn, norm_final,
           ab_w_in, ab_b_i, ab_b_f, ab_w_mh, ab_w_out, c_w_in, c_conv_w, c_a_log, c_dt_bias, c_w_norm, c_w_out,
           ffn_w_up, ffn_conv_w, ffn_conv_b, ffn_w_down):
    params = (norm_mix, norm_ffn, norm_final, ab_w_in, ab_b_i, ab_b_f, ab_w_mh, ab_w_out,
              c_w_in, c_conv_w, c_a_log, c_dt_bias, c_w_norm, c_w_out,
              ffn_w_up, ffn_conv_w, ffn_conv_b, ffn_w_down)
    batch = x_prompt.shape[0]
    n_ab, n_c = ab_w_in.shape[0], c_w_in.shape[0]
    zc = jnp.zeros((n_ab, batch, H_A, DK_A, DV_A), F32)
    zn = jnp.zeros((n_ab, batch, H_A, DK_A), F32)
    zm = jnp.zeros((n_ab, batch, H_A), F32)
    zs = jnp.zeros((n_c, batch, H_V_C, DK_C, DV_C), F32)
    zg = jnp.zeros((n_c, batch, CONV_C - 1, CONV_DIM_C), F32)
    outs_p = _trunk(x_prompt, 0, zc, zn, zm, zs, zg, None, None, params, tm=1024)
    past_len = page_table.shape[1] * PAGE_SIZE
    outs_s = _trunk(x_sample, past_len, state_mlstm_c, state_mlstm_n, state_mlstm_m,
                    state_gdn_s, state_gdn_conv, state_ffn_conv, (cache_k, cache_v, page_table), params,
                    tm=x_sample.shape[0] * x_sample.shape[1])
    return (outs_p[0], outs_s[0]) + tuple(outs_p[1:]) + tuple(outs_s[1:])
```

```python
import functools
import math

import jax
import jax.numpy as jnp
import numpy as np
from jax import lax
from jax.experimental import pallas as pl
from jax.experimental.pallas import tpu as pltpu

D_MODEL = 2048
DEPTH = 4
PAGE_SIZE = 128
H_A, DK_A, DV_A = 8, 64, 128
H_B, HD_B = 8, 128
MOBA_BLOCK, MOBA_TOPK, MOBA_Q_CHUNK = 256, 3, 32
ROPE_DIMS = HD_B // 4
ROPE_THETA = 500000.0
H_QK_C, H_V_C, DK_C, DV_C, CONV_C = 16, 32, 128, 128, 4
CONV_DIM_C = 2 * H_QK_C * DK_C + H_V_C * DV_C
D_FF = 5504
FFN_CONV = 3
CHUNK = 64
EPS = 1e-6
NEG = -1e30
A_COLS = 2 * H_A * DK_A + 2 * H_A * DV_A
B_COLS = 3 * H_B * HD_B
GATE_COLS = 2 * H_A

F32 = jnp.float32
BF16 = jnp.bfloat16
VMEM_LIMIT_BYTES = 56 * 1024 * 1024
HALO = 16


def _params(*sem):
    return pltpu.CompilerParams(dimension_semantics=sem, vmem_limit_bytes=VMEM_LIMIT_BYTES)


def _rms(x, w):
    return x * lax.rsqrt(jnp.mean(x * x, axis=-1, keepdims=True) + EPS) * w


def _bdot(a, b):
    return jnp.dot(a, b, preferred_element_type=F32)


def _nmm_kernel(x_ref, nw_ref, w_ref, wt_ref, o_ref, ot_ref, xn_ref):
    @pl.when(pl.program_id(1) == 0)
    def _():
        xn = _rms(x_ref[...], nw_ref[...]).astype(BF16)
        xn_ref[...] = xn
        ot_ref[...] = _bdot(xn, wt_ref[...].astype(BF16))

    o_ref[...] = _bdot(xn_ref[...], w_ref[...].astype(BF16))


def _norm_matmul(x, nw, w, n_main, w_tail, *, tm, tn):
    m, k = x.shape
    nt = w_tail.shape[1]
    return pl.pallas_call(
        _nmm_kernel,
        grid=(m // tm, n_main // tn),
        in_specs=[pl.BlockSpec((tm, k), lambda i, j: (i, 0)),
                  pl.BlockSpec((1, k), lambda i, j: (0, 0)),
                  pl.BlockSpec((k, tn), lambda i, j: (0, j)),
                  pl.BlockSpec((k, nt), lambda i, j: (0, 0))],
        out_specs=[pl.BlockSpec((tm, tn), lambda i, j: (i, j)),
                   pl.BlockSpec((tm, nt), lambda i, j: (i, 0))],
        out_shape=[jax.ShapeDtypeStruct((m, n_main), F32), jax.ShapeDtypeStruct((m, nt), F32)],
        scratch_shapes=[pltpu.VMEM((tm, k), BF16)],
        compiler_params=_params("parallel", "arbitrary"),
        name="norm_matmul",
    )(x, nw.reshape(1, k), w, w_tail)


def _mmr_kernel(a_ref, w_ref, r_ref, o_ref):
    o_ref[...] = r_ref[...] + _bdot(a_ref[...].astype(BF16), w_ref[...].astype(BF16))


def _matmul_res(a, w, res, *, tm, tn):
    m, k = a.shape
    n = w.shape[1]
    return pl.pallas_call(
        _mmr_kernel,
        grid=(m // tm, n // tn),
        in_specs=[pl.BlockSpec((tm, k), lambda i, j: (i, 0)),
                  pl.BlockSpec((k, tn), lambda i, j: (0, j)),
                  pl.BlockSpec((tm, tn), lambda i, j: (i, j))],
        out_specs=pl.BlockSpec((tm, tn), lambda i, j: (i, j)),
        out_shape=jax.ShapeDtypeStruct((m, n), F32),
        compiler_params=_params("parallel", "arbitrary"),
        name="matmul_res",
    )(a, w, res)


def _mm2r_kernel(a1_ref, a2_ref, w1_ref, w2_ref, r_ref, o_ref):
    o_ref[...] = (r_ref[...] + _bdot(a1_ref[...].astype(BF16), w1_ref[...].astype(BF16))
                  + _bdot(a2_ref[...].astype(BF16), w2_ref[...].astype(BF16)))


def _matmul2_res(a1, a2, w, res, *, tm, tn):
    m, k = a1.shape
    n = w.shape[1]
    return pl.pallas_call(
        _mm2r_kernel,
        grid=(m // tm, n // tn),
        in_specs=[pl.BlockSpec((tm, k), lambda i, j: (i, 0)),
                  pl.BlockSpec((tm, k), lambda i, j: (i, 0)),
                  pl.BlockSpec((k, tn), lambda i, j: (0, j)),
                  pl.BlockSpec((k, tn), lambda i, j: (1, j)),
                  pl.BlockSpec((tm, tn), lambda i, j: (i, j))],
        out_specs=pl.BlockSpec((tm, tn), lambda i, j: (i, j)),
        out_shape=jax.ShapeDtypeStruct((m, n), F32),
        compiler_params=_params("parallel", "arbitrary"),
        name="matmul2_res",
    )(a1, a2, w, w, res)


def _final_norm_kernel(x_ref, w_ref, o_ref):
    o_ref[...] = _rms(x_ref[...], w_ref[...])


def _final_norm(x, w, *, tm):
    m, k = x.shape
    return pl.pallas_call(
        _final_norm_kernel,
        grid=(m // tm,),
        in_specs=[pl.BlockSpec((tm, k), lambda i: (i, 0)), pl.BlockSpec((1, k), lambda i: (0, 0))],
        out_specs=pl.BlockSpec((tm, k), lambda i: (i, 0)),
        out_shape=jax.ShapeDtypeStruct((m, k), F32),
        compiler_params=_params("parallel"),
        name="final_norm",
    )(x, w.reshape(1, k))


def _ffn_up_kernel(xh_ref, x_ref, nw_ref, wg_ref, wu_ref, cwg_ref, cwu_ref, cbg_ref, cbu_ref,
                   act_ref, lastg_ref, lastu_ref, xn_ref, *, tiles_per_seq):
    i = pl.program_id(0)

    @pl.when(pl.program_id(1) == 0)
    def _():
        nw = nw_ref[...]
        xn_ref[HALO:, :] = _rms(x_ref[...], nw).astype(BF16)
        halo = jnp.where(i % tiles_per_seq == 0, 0.0, _rms(xh_ref[...], nw))
        xn_ref[:HALO, :] = halo.astype(BF16)

    xn = xn_ref[...]

    def branch(w_ref, cw_ref, cb_ref, last_ref):
        v = _bdot(xn, w_ref[...].astype(BF16))
        last_ref[...] = v[v.shape[0] - 8:, :]
        cw = cw_ref[...]
        y = pltpu.roll(v, 2, 0) * cw[0:1] + pltpu.roll(v, 1, 0) * cw[1:2] + v * cw[2:3] + cb_ref[...]
        return y[HALO:, :]

    g = branch(wg_ref, cwg_ref, cbg_ref, lastg_ref)
    u = branch(wu_ref, cwu_ref, cbu_ref, lastu_ref)
    act_ref[...] = (g / (1.0 + jnp.exp(-g)) * u).astype(BF16)


def _ffn_up(x, nw, w_up, conv_w, conv_b, seq_len, *, tm, tn):
    m, k = x.shape
    wg, wu = w_up[:, :D_FF], w_up[:, D_FF:]
    cwg, cwu = conv_w[:, :D_FF], conv_w[:, D_FF:]
    cbg, cbu = conv_b[:D_FF].reshape(1, D_FF), conv_b[D_FF:].reshape(1, D_FF)
    n_i, n_j = m // tm, pl.cdiv(D_FF, tn)
    hb = tm // HALO
    wspec = pl.BlockSpec((k, tn), lambda i, j: (0, j))
    cwspec = pl.BlockSpec((FFN_CONV, tn), lambda i, j: (0, j))
    cbspec = pl.BlockSpec((1, tn), lambda i, j: (0, j))
    lastspec = pl.BlockSpec((None, 8, tn), lambda i, j: (i, 0, j))
    act, lastg, lastu = pl.pallas_call(
        functools.partial(_ffn_up_kernel, tiles_per_seq=seq_len // tm),
        grid=(n_i, n_j),
        in_specs=[pl.BlockSpec((HALO, k), lambda i, j: (jnp.maximum(i * hb - 1, 0), 0)),
                  pl.BlockSpec((tm, k), lambda i, j: (i, 0)),
                  pl.BlockSpec((1, k), lambda i, j: (0, 0)),
                  wspec, wspec, cwspec, cwspec, cbspec, cbspec],
        out_specs=[pl.BlockSpec((tm, tn), lambda i, j: (i, j)), lastspec, lastspec],
        out_shape=[jax.ShapeDtypeStruct((m, D_FF), BF16),
                   jax.ShapeDtypeStruct((n_i, 8, D_FF), F32),
                   jax.ShapeDtypeStruct((n_i, 8, D_FF), F32)],
        scratch_shapes=[pltpu.VMEM((HALO + tm, k), BF16)],
        compiler_params=_params("parallel", "arbitrary"),
        name="ffn_up",
    )(x, x, nw.reshape(1, k), wg, wu, cwg, cwu, cbg, cbu)
    return act, lastg, lastu


def _l2norm(x):
    xf = x.astype(jnp.float32)
    return xf * lax.rsqrt(jnp.sum(xf * xf, axis=-1, keepdims=True) + EPS)


def _rmsnorm(x, w):
    xf = x.astype(jnp.float32)
    y = xf * lax.rsqrt(jnp.mean(xf * xf, axis=-1, keepdims=True) + EPS)
    return (y * w.astype(jnp.float32)).astype(x.dtype)


def _causal_dwconv(u, buf, w, b=None):
    width, t = w.shape[0], u.shape[1]
    ext = jnp.concatenate([buf.astype(u.dtype), u], axis=1)
    y = ext[:, 0:t] * w[0]
    for j in range(1, width):
        y = y + ext[:, j:j + t] * w[j]
    if b is not None:
        y = y + b
    return y, ext[:, t:]


def _partial_rope(x, pos):
    half = ROPE_DIMS // 2
    inv = ROPE_THETA ** (-jnp.arange(half, dtype=jnp.float32) / half)
    ang = pos.astype(jnp.float32)[:, None] * inv[None, :]
    cos = jnp.cos(ang)[None, :, None, :]
    sin = jnp.sin(ang)[None, :, None, :]
    xr = x[..., :ROPE_DIMS].astype(jnp.float32)
    x1, x2 = xr[..., :half], xr[..., half:]
    rot = jnp.concatenate([x1 * cos - x2 * sin, x2 * cos + x1 * sin], axis=-1)
    return jnp.concatenate([rot.astype(x.dtype), x[..., ROPE_DIMS:]], axis=-1)


def _chunk_len(t):
    return CHUNK if t % CHUNK == 0 else t


def _to_chunks(a, L):
    b, t, h = a.shape[:3]
    a = a.astype(jnp.float32).reshape((b, t // L, L, h) + a.shape[3:])
    return jnp.moveaxis(a, (1, 3), (0, 2))


def _from_chunks(a):
    a = jnp.moveaxis(a, (0, 2), (1, 3))
    b, n, L, h = a.shape[:4]
    return a.reshape((b, n * L, h) + a.shape[4:])


def _mlstm_chunkwise(q, k, v, i_raw, f_raw, c0, n0, m0):
    t = q.shape[1]
    L = _chunk_len(t)
    qc = _to_chunks(q, L)
    kc = _to_chunks(k, L) * (DK_A ** -0.5)
    vc = _to_chunks(v, L)
    li = _to_chunks(i_raw, L)
    lf = jax.nn.log_sigmoid(_to_chunks(f_raw, L))
    tril = jnp.tril(jnp.ones((L, L), bool))

    def step(carry, xs):
        C, n, m = carry
        qj, kj, vj, lij, lfj = xs
        b = jnp.cumsum(lfj, axis=-1)
        D = jnp.where(tril, b[..., :, None] - b[..., None, :] + lij[..., None, :], NEG)
        inter = b + m[..., None]
        mt = jnp.maximum(inter, D.max(-1))
        s = jnp.einsum('bhtd,bhsd->bhts', qj, kj) * jnp.exp(D - mt[..., None])
        wi = jnp.exp(inter - mt)
        num = wi[..., None] * jnp.einsum('bhtd,bhde->bhte', qj, C) + jnp.einsum('bhts,bhse->bhte', s, vj)
        den = wi * jnp.einsum('bhtd,bhd->bht', qj, n) + s.sum(-1)
        h = num / jnp.maximum(jnp.abs(den), jnp.exp(-mt))[..., None]
        bL = b[..., -1]
        g = bL[..., None] - b + lij
        m_new = jnp.maximum(bL + m, g.max(-1))
        wc = jnp.exp(bL + m - m_new)
        ws = jnp.exp(g - m_new[..., None])
        C_new = wc[..., None, None] * C + jnp.einsum('bhs,bhsd,bhse->bhde', ws, kj, vj)
        n_new = wc[..., None] * n + jnp.einsum('bhs,bhsd->bhd', ws, kj)
        return (C_new, n_new, m_new), h

    init = (c0.astype(jnp.float32), n0.astype(jnp.float32), m0.astype(jnp.float32))
    (C, n, m), h = lax.scan(step, init, (qc, kc, vc, li, lf))
    return _from_chunks(h), C, n, m


def _gated_delta_chunked(q, k, v, g, beta, s0):
    t, dv = q.shape[1], v.shape[-1]
    L = _chunk_len(t)
    qc, kc, vc = _to_chunks(q, L), _to_chunks(k, L), _to_chunks(v, L)
    gc, bc = _to_chunks(g, L), _to_chunks(beta, L)
    decay = jnp.cumsum(gc, axis=-1)
    diff = decay[..., :, None] - decay[..., None, :]
    strict = jnp.tril(jnp.ones((L, L), bool), -1)
    incl = jnp.tril(jnp.ones((L, L), bool))
    kb = kc * bc[..., None]
    A = jnp.where(strict, jnp.einsum('nbhtd,nbhsd->nbhts', kb, kc) * jnp.exp(jnp.where(strict, diff, 0.0)), 0.0)
    rhs = jnp.concatenate([vc * bc[..., None], kb * jnp.exp(decay)[..., None]], axis=-1)
    sol = lax.linalg.triangular_solve(jnp.eye(L, dtype=jnp.float32) + A, rhs,
                                      left_side=True, lower=True, unit_diagonal=True)
    u, w = sol[..., :dv], sol[..., dv:]
    attn = jnp.where(incl, jnp.einsum('nbhtd,nbhsd->nbhts', qc, kc) * jnp.exp(jnp.where(incl, diff, 0.0)), 0.0)
    q_dec = qc * jnp.exp(decay)[..., None]
    k_dec = kc * jnp.exp(decay[..., -1:] - decay)[..., None]
    g_last = jnp.exp(decay[..., -1])

    def step(S, xs):
        u_j, w_j, attn_j, qd_j, kd_j, gl_j = xs
        v_new = u_j - jnp.einsum('bhtd,bhde->bhte', w_j, S)
        o = jnp.einsum('bhtd,bhde->bhte', qd_j, S) + jnp.einsum('bhts,bhse->bhte', attn_j, v_new)
        S = gl_j[..., None, None] * S + jnp.einsum('bhtd,bhte->bhde', kd_j, v_new)
        return S, o

    S, o = lax.scan(step, s0.astype(jnp.float32), (u, w, attn, q_dec, k_dec, g_last))
    return _from_chunks(o), S


def _moba_attention(q, k_all, v_all, pos0):
    b, tq, h, hd = q.shape
    L = k_all.shape[1]
    nb = -(-L // MOBA_BLOCK)
    pad = nb * MOBA_BLOCK - L
    kb = jnp.pad(k_all, ((0, 0), (0, pad), (0, 0), (0, 0))).reshape(b, nb, MOBA_BLOCK, h, hd)
    vb = jnp.pad(v_all, ((0, 0), (0, pad), (0, 0), (0, 0))).reshape(b, nb, MOBA_BLOCK, h, hd)
    kmean = jnp.mean(kb.astype(jnp.float32), axis=2)
    topk = min(MOBA_TOPK, nb)
    qc = MOBA_Q_CHUNK if tq % MOBA_Q_CHUNK == 0 else tq
    nq = tq // qc
    q_chunks = jnp.moveaxis(q.reshape(b, nq, qc, h, hd), 1, 0)
    pos = (pos0 + jnp.arange(tq, dtype=jnp.int32)).reshape(nq, qc)
    bi = jnp.arange(b)[:, None, None, None]
    hi = jnp.arange(h)[None, :, None, None]
    blk_ids = jnp.arange(nb, dtype=jnp.int32)
    offs = jnp.arange(MOBA_BLOCK, dtype=jnp.int32)
    scale = hd ** -0.5

    def attend(args):
        qj, pj = args
        own = pj // MOBA_BLOCK
        qf = qj.astype(jnp.float32)
        gate = jnp.einsum('bqhd,bnhd->bhqn', qf, kmean)
        past = blk_ids[None, :] < own[:, None]
        gate = jnp.where(past[None, None], gate, NEG)
        _, top = lax.top_k(gate, topk)
        top_ok = top < own[None, None, :, None]
        own_b = jnp.broadcast_to(own[None, None, :, None], (b, h, qc, 1))
        idx = jnp.concatenate([top, own_b], axis=-1)
        ok_blk = jnp.concatenate([top_ok, jnp.ones((b, h, qc, 1), bool)], axis=-1)
        k_sel = kb[bi, idx, :, hi, :].astype(jnp.float32)
        v_sel = vb[bi, idx, :, hi, :].astype(jnp.float32)
        logits = jnp.einsum('bqhd,bhqsjd->bhqsj', qf, k_sel) * scale
        key_pos = idx[..., None] * MOBA_BLOCK + offs
        ok = ok_blk[..., None] & (key_pos <= pj[None, None, :, None, None])
        logits = jnp.where(ok, logits, NEG).reshape(b, h, qc, -1)
        p = jax.nn.softmax(logits, axis=-1).reshape(b, h, qc, -1, MOBA_BLOCK)
        return jnp.einsum('bhqsj,bhqsjd->bqhd', p, v_sel).astype(q.dtype)

    out = lax.map(attend, (q_chunks, pos))
    return jnp.moveaxis(out, 0, 1).reshape(b, tq, h, hd)


def _gather_pages(pool, layer, page_table):
    rows = pool[layer, page_table]
    db, npg = page_table.shape
    return rows.reshape(db, npg * PAGE_SIZE, rows.shape[-2], rows.shape[-1])


def _mixer_ab_core(main, gates, b, t, pos0, b_i, b_f, w_mh, c0, n0, m0, k_past, v_past):
    qa = main[:, 0:512].reshape(b, t, H_A, DK_A)
    ka = main[:, 512:1024].reshape(b, t, H_A, DK_A)
    va = main[:, 1024:2048].reshape(b, t, H_A, DV_A)
    oa = main[:, 2048:3072].reshape(b, t, H_A * DV_A)
    qb = main[:, A_COLS:A_COLS + 1024].reshape(b, t, H_B, HD_B)
    kb = main[:, A_COLS + 1024:A_COLS + 2048].reshape(b, t, H_B, HD_B)
    vb = main[:, A_COLS + 2048:A_COLS + 3072].reshape(b, t, H_B, HD_B)
    ia = gates[:, :H_A].reshape(b, t, H_A)
    fa = gates[:, H_A:2 * H_A].reshape(b, t, H_A)
    h, c, n, m = _mlstm_chunkwise(qa, ka, va, ia + b_i, fa + b_f, c0, n0, m0)
    h = _rmsnorm(h, w_mh.reshape(H_A, DV_A)).reshape(b, t, H_A * DV_A) * jax.nn.sigmoid(oa)
    pos = pos0 + jnp.arange(t, dtype=jnp.int32)
    qb = _partial_rope(qb, pos)
    kb = _partial_rope(kb, pos)
    if k_past is None:
        k_all, v_all = kb, vb
    else:
        k_all = jnp.concatenate([k_past, kb], axis=1)
        v_all = jnp.concatenate([v_past, vb], axis=1)
    o_b = _moba_attention(qb, k_all, v_all, pos0).reshape(b, t, H_B * HD_B)
    cat = jnp.concatenate([h, o_b], axis=-1).reshape(b * t, 2 * H_B * HD_B).astype(BF16)
    return cat, c, n, m, kb, vb


def _mixer_c_core(qkv, z, tail, b, t, conv_w, a_log, dt_bias, w_norm, s0, conv_buf):
    qkv = qkv.reshape(b, t, CONV_DIM_C)
    bt = tail[:, :H_V_C].reshape(b, t, H_V_C)
    a = tail[:, H_V_C:].reshape(b, t, H_V_C)
    qkv, conv_new = _causal_dwconv(qkv, conv_buf, conv_w)
    qkv = jax.nn.silu(qkv)
    q, k, v = jnp.split(qkv, (H_QK_C * DK_C, 2 * H_QK_C * DK_C), axis=-1)
    rep = H_V_C // H_QK_C
    q = jnp.repeat(_l2norm(q.reshape(b, t, H_QK_C, DK_C)) * (DK_C ** -0.5), rep, axis=2)
    k = jnp.repeat(_l2norm(k.reshape(b, t, H_QK_C, DK_C)), rep, axis=2)
    v = v.reshape(b, t, H_V_C, DV_C)
    beta = jax.nn.sigmoid(bt)
    g = -jnp.exp(a_log) * jax.nn.softplus(a + dt_bias)
    o, s = _gated_delta_chunked(q, k, v, g, beta, s0)
    o = _rmsnorm(o, w_norm) * jax.nn.silu(z.reshape(b, t, H_V_C, DV_C))
    return o.reshape(b * t, H_V_C * DV_C).astype(BF16), s, conv_new


def _trunk(x, pos0, c0, n0, m0, s0, gconv0, fconv0, kv_pages, params, *, tm):
    (norm_mix, norm_ffn, norm_final, ab_w_in, ab_b_i, ab_b_f, ab_w_mh, ab_w_out,
     c_w_in, c_conv_w, c_a_log, c_dt_bias, c_w_norm, c_w_out,
     ffn_w_up, ffn_conv_w, ffn_conv_b, ffn_w_down) = params
    b, t, d = x.shape
    prompt = kv_pages is None
    rows_k, rows_v, cs, ns, ms, ss, gcs, fcs = [], [], [], [], [], [], [], []
    h = x.reshape(b * t, d)
    for l in range(DEPTH):
        j = l // 2
        if l % 2 == 0:
            w = ab_w_in[j]
            w_main = jnp.concatenate([w[:, :A_COLS], w[:, A_COLS + GATE_COLS:]], axis=1)
            w_gate = jnp.pad(w[:, A_COLS:A_COLS + GATE_COLS], ((0, 0), (0, GATE_LANES - GATE_COLS)))
            main, gates = _norm_matmul(h, norm_mix[l], w_main, A_COLS + B_COLS, w_gate, tm=tm, tn=512)
            if prompt:
                ha, c, n, m = _mlstm_prompt(main, gates, ab_b_i[j], ab_b_f[j], ab_w_mh[j], b, t)
                q_rot, kn, vn, k_bf, v_bf, km = _moba_prep(main, t)
                kmean = km[:, 0, :].reshape(b, t // MOBA_BLOCK, H_B * HD_B)
                ob = _moba_attention_prompt(q_rot, k_bf, v_bf, kmean, b, t)
                m = m[:, :, 0]
                kn = kn.reshape(b, t, H_B, HD_B)
                vn = vn.reshape(b, t, H_B, HD_B)
                h = _matmul2_res(ha, ob, ab_w_out[j], h, tm=tm, tn=512)
            else:
                cache_k, cache_v, page_table = kv_pages
                ha, c, n, m = _mlstm_step(main, gates, ab_b_i[j], ab_b_f[j], ab_w_mh[j], c0[j], n0[j], m0[j])
                m = m[:, :H_A]
                q_rot, kn, vn = _rope_rows(main, pos0)
                sel = _moba_scan(q_rot, cache_k, j, page_table)[:, :, :MOBA_TOPK]
                page_idx = sel[..., None] * PAGES_PER_BLOCK + jnp.arange(PAGES_PER_BLOCK, dtype=jnp.int32)
                pages = jnp.take_along_axis(page_table[:, None, :], page_idx.reshape(b, H_B, SEL_PAGES), axis=2)
                ob = _moba_decode(q_rot, kn, vn, cache_k, cache_v, j, pages.reshape(-1))
                kn = kn.reshape(b, t, H_B, HD_B)
                vn = vn.reshape(b, t, H_B, HD_B)
                h = _matmul2_res(ha, ob, ab_w_out[j], h, tm=tm, tn=512)
            rows_k.append(kn)
            rows_v.append(vn)
            cs.append(c)
            ns.append(n)
            ms.append(m)
        else:
            w = c_w_in[j]
            n_main = CONV_DIM_C + H_V_C * DV_C
            main, tail = _norm_matmul(h, norm_mix[l], w, n_main, w[:, n_main:], tm=tm, tn=512)
            if prompt:
                o, s = _gdn_prompt(main, tail, c_conv_w[j], c_a_log[j], c_dt_bias[j], c_w_norm[j], b, t)
                gc = main.reshape(b, t, n_main)[:, t - (CONV_C - 1):, :CONV_DIM_C]
            else:
                o, s, gc = _gdn_step(main, tail, gconv0[j], c_conv_w[j], c_a_log[j], c_dt_bias[j], c_w_norm[j], s0[j])
            ss.append(s)
            gcs.append(gc)
            h = _matmul_res(o, c_w_out[j], h, tm=tm, tn=256)
        if prompt:
            act, lastg, lastu = _ffn_up(h, norm_ffn[l], ffn_w_up[l], ffn_conv_w[l], ffn_conv_b[l], t, tm=tm, tn=512)
            tps = t // tm
            last = jnp.concatenate([lastg[tps - 1::tps, 8 - (FFN_CONV - 1):], lastu[tps - 1::tps, 8 - (FFN_CONV - 1):]],
                                   axis=-1)
            fcs.append(last)
        else:
            w = ffn_w_up[l]
            n_main = (2 * D_FF) // 512 * 512
            main, tail = _norm_matmul(h, norm_ffn[l], w, n_main, w[:, n_main:], tm=tm, tn=512)
            act, fc = _ffn_mid_step(jnp.concatenate([main, tail], axis=-1), fconv0[l], ffn_conv_w[l], ffn_conv_b[l])
            fcs.append(fc)
        h = _matmul_res(act, ffn_w_down[l], h, tm=tm, tn=256)
    y = _final_norm(h, norm_final, tm=tm).reshape(b, t, d)
    return (y, jnp.stack(rows_k), jnp.stack(rows_v), jnp.stack(cs), jnp.stack(ns), jnp.stack(ms),
            jnp.stack(ss), jnp.stack(gcs), jnp.stack(fcs))


def kernel(x_prompt, x_sample, cache_k, cache_v, page_table, state_mlstm_c, state_mlstm_n, state_mlstm_m,
           state_gdn_s, state_gdn_conv, state_ffn_conv, norm_mix, norm_ffn, norm_final,
           ab_w_in, ab_b_i, ab_b_f, ab_w_mh, ab_w_out, c_w_in, c_conv_w, c_a_log, c_dt_bias, c_w_norm, c_w_out,
           ffn_w_up, ffn_conv_w, ffn_conv_b, ffn_w_down):
    params = (norm_mix, norm_ffn, norm_final, ab_w_in, ab_b_i, ab_b_f, ab_w_mh, ab_w_out,
              c_w_in, c_conv_w, c_a_log, c_dt_bias, c_w_norm, c_w_out,
              ffn_w_up, ffn_conv_w, ffn_conv_b, ffn_w_down)
    batch = x_prompt.shape[0]
    n_ab, n_c = ab_w_in.shape[0], c_w_in.shape[0]
    zc = jnp.zeros((n_ab, batch, H_A, DK_A, DV_A), F32)
    zn = jnp.zeros((n_ab, batch, H_A, DK_A), F32)
    zm = jnp.zeros((n_ab, batch, H_A), F32)
    zs = jnp.zeros((n_c, batch, H_V_C, DK_C, DV_C), F32)
    zg = jnp.zeros((n_c, batch, CONV_C - 1, CONV_DIM_C), F32)
    outs_p = _trunk(x_prompt, 0, zc, zn, zm, zs, zg, None, None, params, tm=1024)
    past_len = page_table.shape[1] * PAGE_SIZE
    outs_s = _trunk(x_sample, past_len, state_mlstm_c, state_mlstm_n, state_mlstm_m,
                    state_gdn_s, state_gdn_conv, state_ffn_conv, (cache_k, cache_v, page_table), params,
                    tm=x_sample.shape[0] * x_sample.shape[1])
    return (outs_p[0], outs_s[0]) + tuple(outs_p[1:]) + tuple(outs_s[1:])


def _rope_tables(pos):
    half = ROPE_DIMS // 2
    inv = ROPE_THETA ** (-jnp.arange(half, dtype=F32) / half)
    ang = pos.astype(F32)[:, None] * inv[None, :]
    cos, sin = jnp.cos(ang), jnp.sin(ang)
    t = pos.shape[0]
    cosf = jnp.concatenate([cos, cos, jnp.ones((t, HD_B - ROPE_DIMS), F32)], axis=-1)
    sina = jnp.concatenate([-sin, jnp.zeros((t, HD_B - half), F32)], axis=-1)
    sinb = jnp.concatenate([jnp.zeros((t, half), F32), sin, jnp.zeros((t, HD_B - ROPE_DIMS), F32)], axis=-1)
    return cosf, sina, sinb


def _rope_head(x, cosf, sina, sinb):
    half = ROPE_DIMS // 2
    return x * cosf + pltpu.roll(x, HD_B - half, 1) * sina + pltpu.roll(x, half, 1) * sinb


def _moba_prep_kernel(q_ref, k_ref, v_ref, cos_ref, sina_ref, sinb_ref,
                      qr_ref, kr_ref, vr_ref, kb_ref, vb_ref, km_ref):
    cosf, sina, sinb = cos_ref[...], sina_ref[...], sinb_ref[...]
    for h in range(H_B):
        sl = slice(h * HD_B, (h + 1) * HD_B)
        qr_ref[:, sl] = _rope_head(q_ref[:, sl], cosf, sina, sinb)
        kr = _rope_head(k_ref[:, sl], cosf, sina, sinb)
        kr_ref[:, sl] = kr
        kb_ref[:, sl] = kr.astype(BF16)
        km_ref[:, sl] = jnp.broadcast_to(jnp.mean(kr, axis=0, keepdims=True), (8, HD_B))
    v = v_ref[...]
    vr_ref[...] = v
    vb_ref[...] = v.astype(BF16)


def _moba_prep(main, seq_len):
    m = main.shape[0]
    hw = H_B * HD_B
    nblk = seq_len // MOBA_BLOCK
    cosf, sina, sinb = _rope_tables(jnp.arange(seq_len, dtype=jnp.int32))
    c0 = A_COLS // hw
    row = pl.BlockSpec((MOBA_BLOCK, hw), lambda i: (i, 0))
    tab = pl.BlockSpec((MOBA_BLOCK, HD_B), lambda i: (i % nblk, 0))
    return pl.pallas_call(
        _moba_prep_kernel,
        grid=(m // MOBA_BLOCK,),
        in_specs=[pl.BlockSpec((MOBA_BLOCK, hw), lambda i: (i, c0)),
                  pl.BlockSpec((MOBA_BLOCK, hw), lambda i: (i, c0 + 1)),
                  pl.BlockSpec((MOBA_BLOCK, hw), lambda i: (i, c0 + 2)),
                  tab, tab, tab],
        out_specs=[row, row, row, row, row, pl.BlockSpec((None, 8, hw), lambda i: (i, 0, 0))],
        out_shape=[jax.ShapeDtypeStruct((m, hw), F32), jax.ShapeDtypeStruct((m, hw), F32),
                   jax.ShapeDtypeStruct((m, hw), F32), jax.ShapeDtypeStruct((m, hw), BF16),
                   jax.ShapeDtypeStruct((m, hw), BF16), jax.ShapeDtypeStruct((m // MOBA_BLOCK, 8, hw), F32)],
        compiler_params=_params("parallel"),
        name="moba_prep",
    )(main, main, main, cosf, sina, sinb)


def _moba_kernel(q_ref, k_ref, v_ref, km_ref, o_ref, *, nblk):
    qi = pl.program_id(2)
    q = q_ref[...]
    gate = lax.dot_general(q, km_ref[...], (((1,), (1,)), ((), ())),
                           precision=lax.Precision.HIGHEST, preferred_element_type=F32)
    blk = lax.broadcasted_iota(jnp.int32, gate.shape, 1)
    blkf = blk.astype(F32)
    past = blk < qi
    g = jnp.where(past, gate, NEG)
    sel = jnp.zeros(gate.shape, F32)
    for _ in range(min(MOBA_TOPK, nblk)):
        mx = jnp.max(g, axis=1, keepdims=True)
        hit = blkf == jnp.min(jnp.where(g == mx, blkf, float(nblk)), axis=1, keepdims=True)
        sel = jnp.where(hit, 1.0, sel)
        g = jnp.where(hit, -jnp.inf, g)
    sel = jnp.where(past, sel, 0.0)

    qb = q.astype(BF16)
    scale = HD_B ** -0.5

    def block(j):
        rows = pl.ds(pl.multiple_of(j * MOBA_BLOCK, MOBA_BLOCK), MOBA_BLOCK)
        s = lax.dot_general(qb, k_ref[rows, :], (((1,), (1,)), ((), ())), preferred_element_type=F32) * scale
        return s, v_ref[rows, :]

    s, vj = block(qi)
    r = lax.broadcasted_iota(jnp.int32, s.shape, 0)
    c = lax.broadcasted_iota(jnp.int32, s.shape, 1)
    s = jnp.where(c <= r, s, NEG)
    m0 = jnp.max(s, axis=1, keepdims=True)
    p = jnp.exp(s - m0)
    l0 = jnp.sum(p, axis=1, keepdims=True)
    acc0 = _bdot(p.astype(BF16), vj)

    def body(j, carry):
        m, l, acc = carry
        selj = jnp.sum(jnp.where(blk == j, sel, 0.0), axis=1, keepdims=True)
        s, vj = block(j)
        s = jnp.where(selj > 0.5, s, NEG)
        m_new = jnp.maximum(m, jnp.max(s, axis=1, keepdims=True))
        alpha = jnp.exp(m - m_new)
        p = jnp.exp(s - m_new)
        return m_new, alpha * l + jnp.sum(p, axis=1, keepdims=True), alpha * acc + _bdot(p.astype(BF16), vj)

    m, l, acc = lax.fori_loop(0, qi, body, (m0, l0, acc0))
    o_ref[...] = (acc / l).astype(BF16)


def _moba_attention_prompt(q_rot, k_bf, v_bf, kmean, n_seq, seq_len):
    m = q_rot.shape[0]
    nblk = seq_len // MOBA_BLOCK
    return pl.pallas_call(
        functools.partial(_moba_kernel, nblk=nblk),
        grid=(n_seq, H_B, nblk),
        in_specs=[pl.BlockSpec((MOBA_BLOCK, HD_B), lambda b, h, i: (b * nblk + i, h)),
                  pl.BlockSpec((seq_len, HD_B), lambda b, h, i: (b, h)),
                  pl.BlockSpec((seq_len, HD_B), lambda b, h, i: (b, h)),
                  pl.BlockSpec((None, nblk, HD_B), lambda b, h, i: (b, 0, h))],
        out_specs=pl.BlockSpec((MOBA_BLOCK, HD_B), lambda b, h, i: (b * nblk + i, h)),
        out_shape=jax.ShapeDtypeStruct((m, H_B * HD_B), BF16),
        compiler_params=_params("parallel", "parallel", "arbitrary"),
        name="moba_attention",
    )(q_rot, k_bf, v_bf, kmean)


MLSTM_ROWS = 128
GATE_LANES = 128


def _mlstm_kernel(q_ref, k_ref, v_ref, og_ref, g_ref, bias_ref, wmh_ref, h_ref, c_ref, n_ref, m_ref,
                  cs_ref, ns_ref, ms_ref):
    r = pl.program_id(1)
    L = q_ref.shape[0]

    @pl.when(r == 0)
    def _():
        cs_ref[...] = jnp.zeros(cs_ref.shape, F32)
        ns_ref[...] = jnp.zeros(ns_ref.shape, F32)
        ms_ref[...] = jnp.zeros(ms_ref.shape, F32)

    gts = g_ref[...] + bias_ref[...]
    lane = lax.broadcasted_iota(jnp.int32, gts.shape, 1)
    lg = jnp.where(lane < H_A, gts, -_softplus(-gts))
    ti = lax.broadcasted_iota(jnp.int32, (L, L), 0)
    si = lax.broadcasted_iota(jnp.int32, (L, L), 1)
    incl = si <= ti
    bmat = jnp.dot(jnp.where(incl, 1.0, 0.0), lg, precision=lax.Precision.HIGHEST, preferred_element_type=F32)
    lgt, bt = lg.T, bmat.T
    nt = (((1,), (1,)), ((), ()))
    tn = (((0,), (0,)), ((), ()))

    for h in range(H_A):
        bcol, brow = bmat[:, H_A + h:H_A + h + 1], bt[H_A + h:H_A + h + 1, :]
        licol, lirow = lg[:, h:h + 1], lgt[h:h + 1, :]
        m_prev = ms_ref[h:h + 1, 0:1]
        d = jnp.where(incl, bcol - brow + lirow, NEG)
        inter = bcol + m_prev
        mt = jnp.maximum(inter, jnp.max(d, axis=1, keepdims=True))
        qh = q_ref[:, h * DK_A:(h + 1) * DK_A]
        kh = k_ref[:, h * DK_A:(h + 1) * DK_A] * (DK_A ** -0.5)
        vb = v_ref[:, h * DV_A:(h + 1) * DV_A].astype(BF16)
        qb = qh.astype(BF16)
        s = lax.dot_general(qb, kh.astype(BF16), nt, preferred_element_type=F32) * jnp.exp(d - mt)
        wi = jnp.exp(inter - mt)
        c_old = cs_ref[h]
        n_old = ns_ref[h:h + 1, :]
        num = wi * _bdot(qb, c_old.astype(BF16)) + _bdot(s.astype(BF16), vb)
        den = wi * jnp.sum(qh * n_old, axis=1, keepdims=True) + jnp.sum(s, axis=1, keepdims=True)
        hh = num / jnp.maximum(jnp.abs(den), jnp.exp(-mt))
        hs = slice(h * DV_A, (h + 1) * DV_A)
        og = og_ref[:, hs]
        h_ref[:, hs] = (_rms(hh, wmh_ref[:, hs]) / (1.0 + jnp.exp(-og))).astype(BF16)
        b_last = bcol[L - 1:L]
        gcol = b_last - bcol + licol
        m_new = jnp.maximum(b_last + m_prev, jnp.max(gcol, axis=0, keepdims=True))
        wc = jnp.exp(b_last + m_prev - m_new)
        kws = kh * jnp.exp(gcol - m_new)
        cs_ref[h] = wc * c_old + lax.dot_general(kws.astype(BF16), vb, tn, preferred_element_type=F32)
        ns_ref[h:h + 1, :] = wc * n_old + jnp.sum(kws, axis=0, keepdims=True)
        ms_ref[h:h + 1, :] = jnp.broadcast_to(m_new, (1, ms_ref.shape[1]))

    @pl.when(r == pl.num_programs(1) - 1)
    def _():
        c_ref[...] = cs_ref[...]
        n_ref[...] = ns_ref[...]
        m_ref[...] = ms_ref[...]


def _mlstm_prompt(main, gates, b_i, b_f, w_mh, n_seq, seq_len):
    m = main.shape[0]
    rows = MLSTM_ROWS
    nr = seq_len // rows
    qkw, vw = H_A * DK_A, H_A * DV_A
    bias = jnp.concatenate([b_i, b_f, jnp.zeros((GATE_LANES - 2 * H_A,), F32)]).reshape(1, GATE_LANES)
    return pl.pallas_call(
        _mlstm_kernel,
        grid=(n_seq, nr),
        in_specs=[pl.BlockSpec((rows, qkw), lambda b, r: (b * nr + r, 0)),
                  pl.BlockSpec((rows, qkw), lambda b, r: (b * nr + r, 1)),
                  pl.BlockSpec((rows, vw), lambda b, r: (b * nr + r, 1)),
                  pl.BlockSpec((rows, vw), lambda b, r: (b * nr + r, 2)),
                  pl.BlockSpec((rows, GATE_LANES), lambda b, r: (b * nr + r, 0)),
                  pl.BlockSpec((1, GATE_LANES), lambda b, r: (0, 0)),
                  pl.BlockSpec((1, vw), lambda b, r: (0, 0))],
        out_specs=[pl.BlockSpec((rows, vw), lambda b, r: (b * nr + r, 0)),
                   pl.BlockSpec((None, H_A, DK_A, DV_A), lambda b, r: (b, 0, 0, 0)),
                   pl.BlockSpec((None, H_A, DK_A), lambda b, r: (b, 0, 0)),
                   pl.BlockSpec((None, H_A, GATE_LANES), lambda b, r: (b, 0, 0))],
        out_shape=[jax.ShapeDtypeStruct((m, vw), BF16),
                   jax.ShapeDtypeStruct((n_seq, H_A, DK_A, DV_A), F32),
                   jax.ShapeDtypeStruct((n_seq, H_A, DK_A), F32),
                   jax.ShapeDtypeStruct((n_seq, H_A, GATE_LANES), F32)],
        scratch_shapes=[pltpu.VMEM((H_A, DK_A, DV_A), F32), pltpu.VMEM((H_A, DK_A), F32),
                        pltpu.VMEM((H_A, GATE_LANES), F32)],
        compiler_params=_params("parallel", "arbitrary"),
        name="mlstm_prompt",
    )(main, main, main, main, gates, bias, w_mh.reshape(1, vw))


GDN_ROWS = 256
GDN_REP = H_V_C // H_QK_C
CONV_HALO = 8


def _silu(x):
    return x / (1.0 + jnp.exp(-x))


def _softplus(x):
    return jnp.maximum(x, 0.0) + jnp.log1p(jnp.exp(-jnp.abs(x)))


def _conv_silu(halo, x, w):
    ext = jnp.concatenate([halo, x], axis=0)
    y = pltpu.roll(ext, CONV_C - 1, 0) * w[0:1]
    for i in range(1, CONV_C - 1):
        y = y + pltpu.roll(ext, CONV_C - 1 - i, 0) * w[i:i + 1]
    y = y + ext * w[CONV_C - 1:CONV_C]
    return _silu(y[halo.shape[0]:])


def _gdn_kernel(qh_ref, kh_ref, vh_ref, q_ref, k_ref, v_ref, z_ref, t_ref, cwq_ref, cwk_ref, cwv_ref,
                al_ref, dt_ref, wn_ref, o_ref, s_ref, st_ref):
    j = pl.program_id(1)
    r = pl.program_id(2)
    L = CHUNK
    rows = q_ref.shape[0]

    @pl.when(r == 0)
    def _():
        st_ref[...] = jnp.zeros(st_ref.shape, F32)

    fresh = r == 0

    def halo(ref):
        return jnp.where(fresh, 0.0, ref[...])

    q = _conv_silu(halo(qh_ref), q_ref[...], cwq_ref[...])
    k = _conv_silu(halo(kh_ref), k_ref[...], cwk_ref[...])
    v = _conv_silu(halo(vh_ref), v_ref[...], cwv_ref[...])
    q = q * lax.rsqrt(jnp.sum(q * q, axis=-1, keepdims=True) + EPS) * (DK_C ** -0.5)
    k = k * lax.rsqrt(jnp.sum(k * k, axis=-1, keepdims=True) + EPS)

    t = t_ref[...]
    lane = lax.broadcasted_iota(jnp.int32, t.shape, 1)
    y = jnp.where(lane < H_V_C, 1.0 / (1.0 + jnp.exp(-t)), -jnp.exp(al_ref[...]) * _softplus(t + dt_ref[...]))
    lane128 = lax.broadcasted_iota(jnp.int32, (rows, 128), 1)
    cols = jnp.zeros((rows, 128), F32)
    betas = []
    for e in range(GDN_REP):
        hv = j * GDN_REP + e
        betas.append(jnp.sum(jnp.where(lane == hv, y, 0.0), axis=1, keepdims=True))
        g_e = jnp.sum(jnp.where(lane == H_V_C + hv, y, 0.0), axis=1, keepdims=True)
        cols = jnp.where(lane128 == e, g_e, cols)
    ri = lax.broadcasted_iota(jnp.int32, (rows, rows), 0)
    ci = lax.broadcasted_iota(jnp.int32, (rows, rows), 1)
    cum = jnp.where((ri // L == ci // L) & (ci <= ri), 1.0, 0.0)
    dcols = jnp.dot(cum, cols, precision=lax.Precision.HIGHEST, preferred_element_type=F32)
    drows = dcols.T

    ti = lax.broadcasted_iota(jnp.int32, (L, L), 0)
    si = lax.broadcasted_iota(jnp.int32, (L, L), 1)
    strict, incl = si < ti, si <= ti
    eye = jnp.where(si == ti, 1.0, 0.0)
    wn = wn_ref[...]
    nt = (((1,), (1,)), ((), ()))
    tn = (((0,), (0,)), ((), ()))

    for c in range(rows // L):
        cs = slice(c * L, (c + 1) * L)
        kc, qc = k[cs], q[cs]
        kcb = kc.astype(BF16)
        kk = lax.dot_general(kcb, kcb, nt, preferred_element_type=F32)
        qk = lax.dot_general(qc.astype(BF16), kcb, nt, preferred_element_type=F32)
        for e in range(GDN_REP):
            dcol = dcols[cs, e:e + 1]
            drow = drows[e:e + 1, cs]
            dlast = dcols[(c + 1) * L - 1:(c + 1) * L, e:e + 1]
            bcol = betas[e][cs]
            ex = jnp.exp(jnp.where(incl, dcol - drow, 0.0))
            n_mat = jnp.where(strict, -(bcol * kk * ex), 0.0)
            attn = jnp.where(incl, qk * ex, 0.0)
            tinv = eye + n_mat
            p = n_mat
            for _ in range(int(math.log2(L)) - 1):
                pb = p.astype(BF16)
                p = _bdot(pb, pb)
                tinv = tinv + _bdot(tinv.astype(BF16), p.astype(BF16))
            edc = jnp.exp(dcol)
            ve = v[cs, e * DV_C:(e + 1) * DV_C]
            rhs = jnp.concatenate([ve * bcol, kc * (bcol * edc)], axis=1).astype(BF16)
            sol = _bdot(tinv.astype(BF16), rhs)
            u, w = sol[:, :DV_C], sol[:, DV_C:]
            s_old = st_ref[e]
            sb = s_old.astype(BF16)
            lhs = jnp.concatenate([w, qc * edc], axis=0).astype(BF16)
            prod = _bdot(lhs, sb)
            v_new = u - prod[:L]
            vnb = v_new.astype(BF16)
            o = prod[L:] + _bdot(attn.astype(BF16), vnb)
            kd = (kc * jnp.exp(dlast - dcol)).astype(BF16)
            st_ref[e] = jnp.exp(dlast) * s_old + lax.dot_general(kd, vnb, tn, preferred_element_type=F32)
            zc = z_ref[cs, e * DV_C:(e + 1) * DV_C]
            o_ref[cs, e * DV_C:(e + 1) * DV_C] = (_rms(o, wn) * _silu(zc)).astype(BF16)

    @pl.when(r == pl.num_programs(2) - 1)
    def _():
        s_ref[...] = st_ref[...]


def _gdn_prompt(main, tail, conv_w, a_log, dt_bias, w_norm, n_seq, seq_len):
    m = main.shape[0]
    rows = GDN_ROWS
    nr = seq_len // rows
    hb = rows // CONV_HALO
    qkw, vw = DK_C, GDN_REP * DV_C
    k0 = H_QK_C
    v0 = 2 * H_QK_C * DK_C // vw
    z0 = CONV_DIM_C // vw
    zeros = jnp.zeros((H_V_C,), F32)
    al = jnp.concatenate([zeros, a_log]).reshape(1, 2 * H_V_C)
    dt = jnp.concatenate([zeros, dt_bias]).reshape(1, 2 * H_V_C)

    def row_idx(b, r):
        return b * nr + r

    def halo_idx(b, r):
        return jnp.maximum(row_idx(b, r) * hb - 1, 0)

    return pl.pallas_call(
        _gdn_kernel,
        grid=(n_seq, H_QK_C, nr),
        in_specs=[pl.BlockSpec((CONV_HALO, qkw), lambda b, j, r: (halo_idx(b, r), j)),
                  pl.BlockSpec((CONV_HALO, qkw), lambda b, j, r: (halo_idx(b, r), k0 + j)),
                  pl.BlockSpec((CONV_HALO, vw), lambda b, j, r: (halo_idx(b, r), v0 + j)),
                  pl.BlockSpec((rows, qkw), lambda b, j, r: (row_idx(b, r), j)),
                  pl.BlockSpec((rows, qkw), lambda b, j, r: (row_idx(b, r), k0 + j)),
                  pl.BlockSpec((rows, vw), lambda b, j, r: (row_idx(b, r), v0 + j)),
                  pl.BlockSpec((rows, vw), lambda b, j, r: (row_idx(b, r), z0 + j)),
                  pl.BlockSpec((rows, 2 * H_V_C), lambda b, j, r: (row_idx(b, r), 0)),
                  pl.BlockSpec((CONV_C, qkw), lambda b, j, r: (0, j)),
                  pl.BlockSpec((CONV_C, qkw), lambda b, j, r: (0, k0 + j)),
                  pl.BlockSpec((CONV_C, vw), lambda b, j, r: (0, v0 + j)),
                  pl.BlockSpec((1, 2 * H_V_C), lambda b, j, r: (0, 0)),
                  pl.BlockSpec((1, 2 * H_V_C), lambda b, j, r: (0, 0)),
                  pl.BlockSpec((1, DV_C), lambda b, j, r: (0, 0))],
        out_specs=[pl.BlockSpec((rows, vw), lambda b, j, r: (row_idx(b, r), j)),
                   pl.BlockSpec((None, GDN_REP, DK_C, DV_C), lambda b, j, r: (b, j, 0, 0))],
        out_shape=[jax.ShapeDtypeStruct((m, H_V_C * DV_C), BF16),
                   jax.ShapeDtypeStruct((n_seq, H_V_C, DK_C, DV_C), F32)],
        scratch_shapes=[pltpu.VMEM((GDN_REP, DK_C, DV_C), F32)],
        compiler_params=_params("parallel", "parallel", "arbitrary"),
        name="gdn_prompt",
    )(main, main, main, main, main, main, main, tail, conv_w, conv_w, conv_w, al, dt, w_norm.reshape(1, DV_C))


def _to_col(row):
    n = row.shape[1]
    i = lax.broadcasted_iota(jnp.int32, (n, n), 0)
    j = lax.broadcasted_iota(jnp.int32, (n, n), 1)
    return jnp.sum(jnp.where(i == j, jnp.broadcast_to(row, (n, n)), 0.0), axis=1, keepdims=True)


def _sigmoid(x):
    return 1.0 / (1.0 + jnp.exp(-x))


def _mlstm_step_kernel(q_ref, k_ref, v_ref, og_ref, g_ref, bias_ref, wmh_ref, c0_ref, n0_ref, m0_ref,
                       h_ref, c_ref, n_ref, m_ref):
    nb = q_ref.shape[0]
    gts = g_ref[...] + bias_ref[...]
    lane = lax.broadcasted_iota(jnp.int32, gts.shape, 1)
    m_out = jnp.zeros(gts.shape, F32)
    for h in range(H_A):
        li = gts[:, h:h + 1]
        lf = -_softplus(-gts[:, H_A + h:H_A + h + 1])
        m_prev = m0_ref[:, h:h + 1]
        inter = lf + m_prev
        mt = jnp.maximum(inter, li)
        qh = q_ref[:, h * DK_A:(h + 1) * DK_A]
        kh = k_ref[:, h * DK_A:(h + 1) * DK_A] * (DK_A ** -0.5)
        hs = slice(h * DV_A, (h + 1) * DV_A)
        vh = v_ref[:, hs]
        s = jnp.sum(qh * kh, axis=1, keepdims=True) * jnp.exp(li - mt)
        wi = jnp.exp(inter - mt)
        m_new = jnp.maximum(lf + m_prev, li)
        wc = jnp.exp(lf + m_prev - m_new)
        kws = kh * jnp.exp(li - m_new)
        n_old = n0_ref[:, h, :]
        qc_rows = []
        for b in range(nb):
            c_old = c0_ref[b, h]
            qc_rows.append(jnp.sum(_to_col(qh[b:b + 1]) * c_old, axis=0, keepdims=True))
            c_ref[b, h] = wc[b:b + 1] * c_old + _to_col(kws[b:b + 1]) * vh[b:b + 1]
        num = wi * jnp.concatenate(qc_rows, axis=0) + s * vh
        den = wi * jnp.sum(qh * n_old, axis=1, keepdims=True) + s
        hh = num / jnp.maximum(jnp.abs(den), jnp.exp(-mt))
        h_ref[:, hs] = _rms(hh, wmh_ref[:, hs]) * _sigmoid(og_ref[:, hs])
        n_ref[:, h, :] = wc * n_old + kws
        m_out = jnp.where(lane == h, m_new, m_out)
    m_ref[...] = m_out


def _mlstm_step(main, gates, b_i, b_f, w_mh, c0, n0, m0):
    nb = main.shape[0]
    qkw, vw = H_A * DK_A, H_A * DV_A
    bias = jnp.concatenate([b_i, b_f, jnp.zeros((GATE_LANES - 2 * H_A,), F32)]).reshape(1, GATE_LANES)
    full = lambda shape: pl.BlockSpec(shape, lambda i: (0,) * len(shape))
    return pl.pallas_call(
        _mlstm_step_kernel,
        grid=(1,),
        in_specs=[pl.BlockSpec((nb, qkw), lambda i: (0, 0)),
                  pl.BlockSpec((nb, qkw), lambda i: (0, 1)),
                  pl.BlockSpec((nb, vw), lambda i: (0, 1)),
                  pl.BlockSpec((nb, vw), lambda i: (0, 2)),
                  full((nb, GATE_LANES)), full((1, GATE_LANES)), full((1, vw)),
                  full((nb, H_A, DK_A, DV_A)), full((nb, H_A, DK_A)), full((nb, H_A))],
        out_specs=[full((nb, vw)), full((nb, H_A, DK_A, DV_A)), full((nb, H_A, DK_A)), full((nb, GATE_LANES))],
        out_shape=[jax.ShapeDtypeStruct((nb, vw), F32),
                   jax.ShapeDtypeStruct((nb, H_A, DK_A, DV_A), F32),
                   jax.ShapeDtypeStruct((nb, H_A, DK_A), F32),
                   jax.ShapeDtypeStruct((nb, GATE_LANES), F32)],
        compiler_params=_params("arbitrary"),
        name="mlstm_step",
    )(main, main, main, main, gates, bias, w_mh.reshape(1, vw), c0, n0, m0)


def _gdn_step_kernel(x_ref, t_ref, buf_ref, cw_ref, al_ref, dt_ref, wn_ref, s0_ref, o_ref, cn_ref, s_ref):
    x = x_ref[...]
    xc = x[:, :CONV_DIM_C]
    buf = buf_ref[...]
    cw = cw_ref[...]
    y = buf[0:1] * cw[0:1]
    for i in range(1, CONV_C - 1):
        y = y + buf[i:i + 1] * cw[i:i + 1]
    y = _silu(y + xc * cw[CONV_C - 1:CONV_C])
    cn_ref[...] = jnp.concatenate([buf[1:], xc], axis=0)
    t = t_ref[...]
    lane = lax.broadcasted_iota(jnp.int32, t.shape, 1)
    gts = jnp.where(lane < H_V_C, _sigmoid(t), -jnp.exp(al_ref[...]) * _softplus(t + dt_ref[...]))
    wn = wn_ref[...]
    k0, v0 = H_QK_C * DK_C, 2 * H_QK_C * DK_C
    for j in range(H_QK_C):
        qj = y[:, j * DK_C:(j + 1) * DK_C]
        kj = y[:, k0 + j * DK_C:k0 + (j + 1) * DK_C]
        qj = qj * lax.rsqrt(jnp.sum(qj * qj, axis=-1, keepdims=True) + EPS) * (DK_C ** -0.5)
        kj = kj * lax.rsqrt(jnp.sum(kj * kj, axis=-1, keepdims=True) + EPS)
        qcol, kcol = _to_col(qj), _to_col(kj)
        qk = jnp.sum(qj * kj, axis=1, keepdims=True)
        for e in range(GDN_REP):
            hv = j * GDN_REP + e
            hs = slice(hv * DV_C, (hv + 1) * DV_C)
            beta = gts[:, hv:hv + 1]
            eg = jnp.exp(gts[:, H_V_C + hv:H_V_C + hv + 1])
            s_old = s0_ref[hv]
            ks = jnp.sum(kcol * s_old, axis=0, keepdims=True)
            qs = jnp.sum(qcol * s_old, axis=0, keepdims=True)
            v_new = y[:, v0 + hv * DV_C:v0 + (hv + 1) * DV_C] * beta - (beta * eg) * ks
            o = eg * qs + qk * v_new
            s_ref[hv] = eg * s_old + kcol * v_new
            o_ref[:, hs] = _rms(o, wn) * _silu(x[:, CONV_DIM_C + hv * DV_C:CONV_DIM_C + (hv + 1) * DV_C])


def _gdn_step(main, tail, conv_buf, conv_w, a_log, dt_bias, w_norm, s0):
    nb, n_main = main.shape
    zeros = jnp.zeros((H_V_C,), F32)
    al = jnp.concatenate([zeros, a_log]).reshape(1, 2 * H_V_C)
    dt = jnp.concatenate([zeros, dt_bias]).reshape(1, 2 * H_V_C)
    hw = H_V_C * DV_C
    const = lambda shape: pl.BlockSpec(shape, lambda b: (0,) * len(shape))
    o, conv_new, s = pl.pallas_call(
        _gdn_step_kernel,
        grid=(nb,),
        in_specs=[pl.BlockSpec((None, 1, n_main), lambda b: (b, 0, 0)),
                  pl.BlockSpec((None, 1, 2 * H_V_C), lambda b: (b, 0, 0)),
                  pl.BlockSpec((None, CONV_C - 1, CONV_DIM_C), lambda b: (b, 0, 0)),
                  const((CONV_C, CONV_DIM_C)), const((1, 2 * H_V_C)), const((1, 2 * H_V_C)), const((1, DV_C)),
                  pl.BlockSpec((None, H_V_C, DK_C, DV_C), lambda b: (b, 0, 0, 0))],
        out_specs=[pl.BlockSpec((None, 1, hw), lambda b: (b, 0, 0)),
                   pl.BlockSpec((None, CONV_C - 1, CONV_DIM_C), lambda b: (b, 0, 0)),
                   pl.BlockSpec((None, H_V_C, DK_C, DV_C), lambda b: (b, 0, 0, 0))],
        out_shape=[jax.ShapeDtypeStruct((nb, 1, hw), F32),
                   jax.ShapeDtypeStruct((nb, CONV_C - 1, CONV_DIM_C), F32),
                   jax.ShapeDtypeStruct((nb, H_V_C, DK_C, DV_C), F32)],
        compiler_params=_params("parallel"),
        name="gdn_step",
    )(main.reshape(nb, 1, n_main), tail.reshape(nb, 1, 2 * H_V_C), conv_buf, conv_w, al, dt,
      w_norm.reshape(1, DV_C), s0)
    return o.reshape(nb, hw), s, conv_new


SCAN_PAGES = 8
PAGES_PER_BLOCK = MOBA_BLOCK // PAGE_SIZE


def _rope_rows_kernel(q_ref, k_ref, v_ref, cos_ref, sina_ref, sinb_ref, qr_ref, kr_ref, vr_ref):
    cosf, sina, sinb = cos_ref[...], sina_ref[...], sinb_ref[...]
    for h in range(H_B):
        sl = slice(h * HD_B, (h + 1) * HD_B)
        qr_ref[:, sl] = _rope_head(q_ref[:, sl], cosf, sina, sinb)
        kr_ref[:, sl] = _rope_head(k_ref[:, sl], cosf, sina, sinb)
    vr_ref[...] = v_ref[...]


def _rope_rows(main, pos):
    nb = main.shape[0]
    hw = H_B * HD_B
    c0 = A_COLS // hw
    cosf, sina, sinb = _rope_tables(jnp.full((1,), pos, jnp.int32))
    tab = pl.BlockSpec((1, HD_B), lambda i: (0, 0))
    out = pl.BlockSpec((nb, hw), lambda i: (0, 0))
    return pl.pallas_call(
        _rope_rows_kernel,
        grid=(1,),
        in_specs=[pl.BlockSpec((nb, hw), lambda i: (0, c0)),
                  pl.BlockSpec((nb, hw), lambda i: (0, c0 + 1)),
                  pl.BlockSpec((nb, hw), lambda i: (0, c0 + 2)), tab, tab, tab],
        out_specs=[out, out, out],
        out_shape=[jax.ShapeDtypeStruct((nb, hw), F32)] * 3,
        compiler_params=_params("arbitrary"),
        name="rope_rows",
    )(main, main, main, cosf, sina, sinb)


def _moba_scan_kernel(pt_ref, q_ref, *refs):
    pages, sel_ref, g_ref = refs[:SCAN_PAGES], refs[SCAN_PAGES], refs[SCAN_PAGES + 1]
    gi = pl.program_id(1)

    @pl.when(gi == 0)
    def _():
        g_ref[...] = jnp.full(g_ref.shape, NEG, F32)

    q = q_ref[...]
    sub = lax.broadcasted_iota(jnp.int32, g_ref.shape, 0)
    lane = lax.broadcasted_iota(jnp.int32, g_ref.shape, 1)
    g = g_ref[...]
    for p in range(SCAN_PAGES // PAGES_PER_BLOCK):
        ksum = jnp.sum(pages[PAGES_PER_BLOCK * p][...], axis=0, keepdims=True)
        for i in range(1, PAGES_PER_BLOCK):
            ksum = ksum + jnp.sum(pages[PAGES_PER_BLOCK * p + i][...], axis=0, keepdims=True)
        prod = q * (ksum * (1.0 / MOBA_BLOCK))
        blk = gi * (SCAN_PAGES // PAGES_PER_BLOCK) + p
        for h in range(H_B):
            val = jnp.sum(prod[:, h * HD_B:(h + 1) * HD_B], axis=1, keepdims=True)
            g = jnp.where((sub == h) & (lane == blk), val, g)
    g_ref[...] = g

    @pl.when(gi == pl.num_programs(1) - 1)
    def _():
        lanef = lane.astype(F32)
        gg = g
        out = jnp.zeros(g.shape, F32)
        for r in range(MOBA_TOPK):
            mx = jnp.max(gg, axis=1, keepdims=True)
            idx = jnp.min(jnp.where(gg == mx, lanef, float(g.shape[1])), axis=1, keepdims=True)
            out = jnp.where(lane == r, idx, out)
            gg = jnp.where(lanef == idx, -jnp.inf, gg)
        sel_ref[...] = out.astype(jnp.int32)


def _moba_scan(q_rot, cache_k, layer, page_table):
    nb = q_rot.shape[0]
    hw = H_B * HD_B
    n_pages = page_table.shape[1]
    assert n_pages % SCAN_PAGES == 0 and n_pages // PAGES_PER_BLOCK <= 128
    n_layers, n_pool = cache_k.shape[:2]
    pool = cache_k.reshape(n_layers, n_pool, PAGE_SIZE, hw)

    def page_spec(i):
        return pl.BlockSpec((None, None, PAGE_SIZE, hw),
                            lambda b, g, pt: (layer, pt[b * n_pages + g * SCAN_PAGES + i], 0, 0))

    grid_spec = pltpu.PrefetchScalarGridSpec(
        num_scalar_prefetch=1,
        grid=(nb, n_pages // SCAN_PAGES),
        in_specs=[pl.BlockSpec((None, 1, hw), lambda b, g, pt: (b, 0, 0))] + [page_spec(i) for i in range(SCAN_PAGES)],
        out_specs=pl.BlockSpec((None, H_B, 128), lambda b, g, pt: (b, 0, 0)),
        scratch_shapes=[pltpu.VMEM((H_B, 128), F32)],
    )
    return pl.pallas_call(
        _moba_scan_kernel,
        grid_spec=grid_spec,
        out_shape=jax.ShapeDtypeStruct((nb, H_B, 128), jnp.int32),
        compiler_params=_params("parallel", "arbitrary"),
        name="moba_scan",
    )(page_table.reshape(-1), q_rot.reshape(nb, 1, hw), *([pool] * SCAN_PAGES))


SEL_PAGES = MOBA_TOPK * PAGES_PER_BLOCK


def _moba_decode_kernel(pg_ref, q_ref, kn_ref, vn_ref, *refs):
    ks, vs, o_ref = refs[:SEL_PAGES], refs[SEL_PAGES:2 * SEL_PAGES], refs[2 * SEL_PAGES]
    q = q_ref[...]
    scale = HD_B ** -0.5
    q8 = jnp.broadcast_to(q, (8, HD_B))
    nt = (((1,), (1,)), ((), ()))
    hi = lax.Precision.HIGHEST
    logits = [lax.dot_general(q8, k[...], nt, precision=hi, preferred_element_type=F32)[0:1] * scale for k in ks]
    own = jnp.sum(q * kn_ref[...], axis=1, keepdims=True) * scale
    m = own
    for s in logits:
        m = jnp.maximum(m, jnp.max(s, axis=1, keepdims=True))
    p_own = jnp.exp(own - m)
    l = p_own
    acc = p_own * vn_ref[...]
    for s, v in zip(logits, vs):
        p = jnp.exp(s - m)
        l = l + jnp.sum(p, axis=1, keepdims=True)
        acc = acc + jnp.dot(jnp.broadcast_to(p, (8, PAGE_SIZE)), v[...], precision=hi, preferred_element_type=F32)[0:1]
    o_ref[...] = acc / l


def _moba_decode(q_rot, k_new, v_new, cache_k, cache_v, layer, pages):
    nb = q_rot.shape[0]
    hw = H_B * HD_B
    n_layers, n_pool = cache_k.shape[:2]
    kpool = cache_k.reshape(n_layers, n_pool, PAGE_SIZE, hw)
    vpool = cache_v.reshape(n_layers, n_pool, PAGE_SIZE, hw)

    def page_spec(i):
        return pl.BlockSpec((None, None, PAGE_SIZE, HD_B),
                            lambda b, h, pg: (layer, pg[(b * H_B + h) * SEL_PAGES + i], 0, h))

    row = pl.BlockSpec((None, 1, HD_B), lambda b, h, pg: (b, 0, h))
    grid_spec = pltpu.PrefetchScalarGridSpec(
        num_scalar_prefetch=1,
        grid=(nb, H_B),
        in_specs=[row, row, row] + [page_spec(i) for i in range(SEL_PAGES)] * 2,
        out_specs=row,
    )
    r3 = lambda a: a.reshape(nb, 1, hw)
    out = pl.pallas_call(
        _moba_decode_kernel,
        grid_spec=grid_spec,
        out_shape=jax.ShapeDtypeStruct((nb, 1, hw), F32),
        compiler_params=_params("parallel", "arbitrary"),
        name="moba_decode",
    )(pages, r3(q_rot), r3(k_new), r3(v_new), *([kpool] * SEL_PAGES), *([vpool] * SEL_PAGES))
    return out.reshape(nb, hw)


def _ffn_mid_kernel(u_ref, buf_ref, cw_ref, cb_ref, act_ref, nb_ref):
    u = u_ref[...]
    cw = cw_ref[...]
    y = buf_ref[:, 0, :] * cw[0:1]
    for i in range(1, FFN_CONV - 1):
        y = y + buf_ref[:, i, :] * cw[i:i + 1]
        nb_ref[:, i - 1, :] = buf_ref[:, i, :]
    y = y + u * cw[FFN_CONV - 1:FFN_CONV] + cb_ref[...]
    nb_ref[:, FFN_CONV - 2, :] = u
    act_ref[...] = _silu(y[:, :D_FF]) * y[:, D_FF:]


def _ffn_mid_step(u, buf, conv_w, conv_b):
    nb = u.shape[0]
    full = lambda shape: pl.BlockSpec(shape, lambda i: (0,) * len(shape))
    return pl.pallas_call(
        _ffn_mid_kernel,
        grid=(1,),
        in_specs=[full((nb, 2 * D_FF)), full((nb, FFN_CONV - 1, 2 * D_FF)), full((FFN_CONV, 2 * D_FF)),
                  full((1, 2 * D_FF))],
        out_specs=[full((nb, D_FF)), full((nb, FFN_CONV - 1, 2 * D_FF))],
        out_shape=[jax.ShapeDtypeStruct((nb, D_FF), F32), jax.ShapeDtypeStruct((nb, FFN_CONV - 1, 2 * D_FF), F32)],
        compiler_params=_params("arbitrary"),
        name="ffn_mid_step",
    )(u, buf, conv_w, conv_b.reshape(1, 2 * D_FF))
```

```python
import functools
import math

import jax
import jax.numpy as jnp
import numpy as np
from jax import lax
from jax.experimental import pallas as pl
from jax.experimental.pallas import tpu as pltpu

D_MODEL = 2048
DEPTH = 4
PAGE_SIZE = 128
H_A, DK_A, DV_A = 8, 64, 128
H_B, HD_B = 8, 128
MOBA_BLOCK, MOBA_TOPK, MOBA_Q_CHUNK = 256, 3, 32
ROPE_DIMS = HD_B // 4
ROPE_THETA = 500000.0
H_QK_C, H_V_C, DK_C, DV_C, CONV_C = 16, 32, 128, 128, 4
CONV_DIM_C = 2 * H_QK_C * DK_C + H_V_C * DV_C
D_FF = 5504
FFN_CONV = 3
CHUNK = 64
EPS = 1e-6
NEG = -1e30
A_COLS = 2 * H_A * DK_A + 2 * H_A * DV_A
B_COLS = 3 * H_B * HD_B
GATE_COLS = 2 * H_A

F32 = jnp.float32
BF16 = jnp.bfloat16
VMEM_LIMIT_BYTES = 56 * 1024 * 1024
HALO = 16


def _params(*sem):
    return pltpu.CompilerParams(dimension_semantics=sem, vmem_limit_bytes=VMEM_LIMIT_BYTES)


def _rms(x, w):
    return x * lax.rsqrt(jnp.mean(x * x, axis=-1, keepdims=True) + EPS) * w


def _bdot(a, b):
    return jnp.dot(a, b, preferred_element_type=F32)


def _nmm_kernel(x_ref, nw_ref, w_ref, wt_ref, o_ref, ot_ref, xn_ref):
    @pl.when(pl.program_id(1) == 0)
    def _():
        xn = _rms(x_ref[...], nw_ref[...]).astype(BF16)
        xn_ref[...] = xn
        ot_ref[...] = _bdot(xn, wt_ref[...].astype(BF16))

    o_ref[...] = _bdot(xn_ref[...], w_ref[...].astype(BF16))


def _norm_matmul(x, nw, w, n_main, w_tail, *, tm, tn):
    m, k = x.shape
    nt = w_tail.shape[1]
    return pl.pallas_call(
        _nmm_kernel,
        grid=(m // tm, n_main // tn),
        in_specs=[pl.BlockSpec((tm, k), lambda i, j: (i, 0)),
                  pl.BlockSpec((1, k), lambda i, j: (0, 0)),
                  pl.BlockSpec((k, tn), lambda i, j: (0, j)),
                  pl.BlockSpec((k, nt), lambda i, j: (0, 0))],
        out_specs=[pl.BlockSpec((tm, tn), lambda i, j: (i, j)),
                   pl.BlockSpec((tm, nt), lambda i, j: (i, 0))],
        out_shape=[jax.ShapeDtypeStruct((m, n_main), F32), jax.ShapeDtypeStruct((m, nt), F32)],
        scratch_shapes=[pltpu.VMEM((tm, k), BF16)],
        compiler_params=_params("parallel", "arbitrary"),
        name="norm_matmul",
    )(x, nw.reshape(1, k), w, w_tail)


def _mmr_kernel(a_ref, w_ref, r_ref, o_ref):
    o_ref[...] = r_ref[...] + _bdot(a_ref[...].astype(BF16), w_ref[...].astype(BF16))


def _matmul_res(a, w, res, *, tm, tn):
    m, k = a.shape
    n = w.shape[1]
    return pl.pallas_call(
        _mmr_kernel,
        grid=(m // tm, n // tn),
        in_specs=[pl.BlockSpec((tm, k), lambda i, j: (i, 0)),
                  pl.BlockSpec((k, tn), lambda i, j: (0, j)),
                  pl.BlockSpec((tm, tn), lambda i, j: (i, j))],
        out_specs=pl.BlockSpec((tm, tn), lambda i, j: (i, j)),
        out_shape=jax.ShapeDtypeStruct((m, n), F32),
        compiler_params=_params("parallel", "arbitrary"),
        name="matmul_res",
    )(a, w, res)


def _mm2r_kernel(a1_ref, a2_ref, w1_ref, w2_ref, r_ref, o_ref):
    o_ref[...] = (r_ref[...] + _bdot(a1_ref[...].astype(BF16), w1_ref[...].astype(BF16))
                  + _bdot(a2_ref[...].astype(BF16), w2_ref[...].astype(BF16)))


def _matmul2_res(a1, a2, w, res, *, tm, tn):
    m, k = a1.shape
    n = w.shape[1]
    return pl.pallas_call(
        _mm2r_kernel,
        grid=(m // tm, n // tn),
        in_specs=[pl.BlockSpec((tm, k), lambda i, j: (i, 0)),
                  pl.BlockSpec((tm, k), lambda i, j: (i, 0)),
                  pl.BlockSpec((k, tn), lambda i, j: (0, j)),
                  pl.BlockSpec((k, tn), lambda i, j: (1, j)),
                  pl.BlockSpec((tm, tn), lambda i, j: (i, j))],
        out_specs=pl.BlockSpec((tm, tn), lambda i, j: (i, j)),
        out_shape=jax.ShapeDtypeStruct((m, n), F32),
        compiler_params=_params("parallel", "arbitrary"),
        name="matmul2_res",
    )(a1, a2, w, w, res)


def _final_norm_kernel(x_ref, w_ref, o_ref):
    o_ref[...] = _rms(x_ref[...], w_ref[...])


def _final_norm(x, w, *, tm):
    m, k = x.shape
    return pl.pallas_call(
        _final_norm_kernel,
        grid=(m // tm,),
        in_specs=[pl.BlockSpec((tm, k), lambda i: (i, 0)), pl.BlockSpec((1, k), lambda i: (0, 0))],
        out_specs=pl.BlockSpec((tm, k), lambda i: (i, 0)),
        out_shape=jax.ShapeDtypeStruct((m, k), F32),
        compiler_params=_params("parallel"),
        name="final_norm",
    )(x, w.reshape(1, k))


def _ffn_up_kernel(xh_ref, x_ref, nw_ref, wg_ref, wu_ref, cwg_ref, cwu_ref, cbg_ref, cbu_ref,
                   act_ref, lastg_ref, lastu_ref, xn_ref, *, tiles_per_seq):
    i = pl.program_id(0)

    @pl.when(pl.program_id(1) == 0)
    def _():
        nw = nw_ref[...]
        xn_ref[HALO:, :] = _rms(x_ref[...], nw).astype(BF16)
        halo = jnp.where(i % tiles_per_seq == 0, 0.0, _rms(xh_ref[...], nw))
        xn_ref[:HALO, :] = halo.astype(BF16)

    xn = xn_ref[...]

    def branch(w_ref, cw_ref, cb_ref, last_ref):
        v = _bdot(xn, w_ref[...].astype(BF16))
        last_ref[...] = v[v.shape[0] - 8:, :]
        cw = cw_ref[...]
        y = pltpu.roll(v, 2, 0) * cw[0:1] + pltpu.roll(v, 1, 0) * cw[1:2] + v * cw[2:3] + cb_ref[...]
        return y[HALO:, :]

    g = branch(wg_ref, cwg_ref, cbg_ref, lastg_ref)
    u = branch(wu_ref, cwu_ref, cbu_ref, lastu_ref)
    act_ref[...] = (g / (1.0 + jnp.exp(-g)) * u).astype(BF16)


def _ffn_up(x, nw, w_up, conv_w, conv_b, seq_len, *, tm, tn):
    m, k = x.shape
    wg, wu = w_up[:, :D_FF], w_up[:, D_FF:]
    cwg, cwu = conv_w[:, :D_FF], conv_w[:, D_FF:]
    cbg, cbu = conv_b[:D_FF].reshape(1, D_FF), conv_b[D_FF:].reshape(1, D_FF)
    n_i, n_j = m // tm, pl.cdiv(D_FF, tn)
    hb = tm // HALO
    wspec = pl.BlockSpec((k, tn), lambda i, j: (0, j))
    cwspec = pl.BlockSpec((FFN_CONV, tn), lambda i, j: (0, j))
    cbspec = pl.BlockSpec((1, tn), lambda i, j: (0, j))
    lastspec = pl.BlockSpec((None, 8, tn), lambda i, j: (i, 0, j))
    act, lastg, lastu = pl.pallas_call(
        functools.partial(_ffn_up_kernel, tiles_per_seq=seq_len // tm),
        grid=(n_i, n_j),
        in_specs=[pl.BlockSpec((HALO, k), lambda i, j: (jnp.maximum(i * hb - 1, 0), 0)),
                  pl.BlockSpec((tm, k), lambda i, j: (i, 0)),
                  pl.BlockSpec((1, k), lambda i, j: (0, 0)),
                  wspec, wspec, cwspec, cwspec, cbspec, cbspec],
        out_specs=[pl.BlockSpec((tm, tn), lambda i, j: (i, j)), lastspec, lastspec],
        out_shape=[jax.ShapeDtypeStruct((m, D_FF), BF16),
                   jax.ShapeDtypeStruct((n_i, 8, D_FF), F32),
                   jax.ShapeDtypeStruct((n_i, 8, D_FF), F32)],
        scratch_shapes=[pltpu.VMEM((HALO + tm, k), BF16)],
        compiler_params=_params("parallel", "arbitrary"),
        name="ffn_up",
    )(x, x, nw.reshape(1, k), wg, wu, cwg, cwu, cbg, cbu)
    return act, lastg, lastu


def _trunk(x, pos0, c0, n0, m0, s0, gconv0, fconv0, kv_pages, params, *, tm):
    (norm_mix, norm_ffn, norm_final, ab_w_in, ab_b_i, ab_b_f, ab_w_mh, ab_w_out,
     c_w_in, c_conv_w, c_a_log, c_dt_bias, c_w_norm, c_w_out,
     ffn_w_up, ffn_conv_w, ffn_conv_b, ffn_w_down) = params
    b, t, d = x.shape
    prompt = kv_pages is None
    rows_k, rows_v, cs, ns, ms, ss, gcs, fcs = [], [], [], [], [], [], [], []
    h = x.reshape(b * t, d)
    for l in range(DEPTH):
        j = l // 2
        if l % 2 == 0:
            w = ab_w_in[j]
            w_main = jnp.concatenate([w[:, :A_COLS], w[:, A_COLS + GATE_COLS:]], axis=1)
            w_gate = jnp.pad(w[:, A_COLS:A_COLS + GATE_COLS], ((0, 0), (0, GATE_LANES - GATE_COLS)))
            main, gates = _norm_matmul(h, norm_mix[l], w_main, A_COLS + B_COLS, w_gate, tm=tm, tn=512)
            if prompt:
                ha, c, n, m = _mlstm_prompt(main, gates, ab_b_i[j], ab_b_f[j], ab_w_mh[j], b, t)
                q_rot, kn, vn, k_bf, v_bf, km = _moba_prep(main, t)
                kmean = km[:, 0, :].reshape(b, t // MOBA_BLOCK, H_B * HD_B)
                ob = _moba_attention_prompt(q_rot, k_bf, v_bf, kmean, b, t)
                m = m[:, :, 0]
                kn = kn.reshape(b, t, H_B, HD_B)
                vn = vn.reshape(b, t, H_B, HD_B)
                h = _matmul2_res(ha, ob, ab_w_out[j], h, tm=tm, tn=512)
            else:
                cache_k, cache_v, page_table = kv_pages
                ha, c, n, m = _mlstm_step(main, gates, ab_b_i[j], ab_b_f[j], ab_w_mh[j], c0[j], n0[j], m0[j])
                m = m[:, :H_A]
                q_rot, kn, vn = _rope_rows(main, pos0)
                sel = _moba_scan(q_rot, cache_k, j, page_table)[:, :, :MOBA_TOPK]
                page_idx = sel[..., None] * PAGES_PER_BLOCK + jnp.arange(PAGES_PER_BLOCK, dtype=jnp.int32)
                pages = jnp.take_along_axis(page_table[:, None, :], page_idx.reshape(b, H_B, SEL_PAGES), axis=2)
                ob = _moba_decode(q_rot, kn, vn, cache_k, cache_v, j, pages.reshape(-1))
                kn = kn.reshape(b, t, H_B, HD_B)
                vn = vn.reshape(b, t, H_B, HD_B)
                h = _matmul2_res(ha, ob, ab_w_out[j], h, tm=tm, tn=512)
            rows_k.append(kn)
            rows_v.append(vn)
            cs.append(c)
            ns.append(n)
            ms.append(m)
        else:
            w = c_w_in[j]
            n_main = CONV_DIM_C + H_V_C * DV_C
            main, tail = _norm_matmul(h, norm_mix[l], w, n_main, w[:, n_main:], tm=tm, tn=512)
            if prompt:
                o, s = _gdn_prompt(main, tail, c_conv_w[j], c_a_log[j], c_dt_bias[j], c_w_norm[j], b, t)
                gc = main.reshape(b, t, n_main)[:, t - (CONV_C - 1):, :CONV_DIM_C]
            else:
                o, s, gc = _gdn_step(main, tail, gconv0[j], c_conv_w[j], c_a_log[j], c_dt_bias[j], c_w_norm[j], s0[j])
            ss.append(s)
            gcs.append(gc)
            h = _matmul_res(o, c_w_out[j], h, tm=tm, tn=256)
        if prompt:
            act, lastg, lastu = _ffn_up(h, norm_ffn[l], ffn_w_up[l], ffn_conv_w[l], ffn_conv_b[l], t, tm=tm, tn=512)
            tps = t // tm
            last = jnp.concatenate([lastg[tps - 1::tps, 8 - (FFN_CONV - 1):], lastu[tps - 1::tps, 8 - (FFN_CONV - 1):]],
                                   axis=-1)
            fcs.append(last)
        else:
            w = ffn_w_up[l]
            n_main = (2 * D_FF) // 512 * 512
            main, tail = _norm_matmul(h, norm_ffn[l], w, n_main, w[:, n_main:], tm=tm, tn=512)
            act, fc = _ffn_mid_step(jnp.concatenate([main, tail], axis=-1), fconv0[l], ffn_conv_w[l], ffn_conv_b[l])
            fcs.append(fc)
        h = _matmul_res(act, ffn_w_down[l], h, tm=tm, tn=256)
    y = _final_norm(h, norm_final, tm=tm).reshape(b, t, d)
    return (y, jnp.stack(rows_k), jnp.stack(rows_v), jnp.stack(cs), jnp.stack(ns), jnp.stack(ms),
            jnp.stack(ss), jnp.stack(gcs), jnp.stack(fcs))


def kernel(x_prompt, x_sample, cache_k, cache_v, page_table, state_mlstm_c, state_mlstm_n, state_mlstm_m,
           state_gdn_s, state_gdn_conv, state_ffn_conv, norm_mix, norm_ffn, norm_final,
           ab_w_in, ab_b_i, ab_b_f, ab_w_mh, ab_w_out, c_w_in, c_conv_w, c_a_log, c_dt_bias, c_w_norm, c_w_out,
           ffn_w_up, ffn_conv_w, ffn_conv_b, ffn_w_down):
    params = (norm_mix, norm_ffn, norm_final, ab_w_in, ab_b_i, ab_b_f, ab_w_mh, ab_w_out,
              c_w_in, c_conv_w, c_a_log, c_dt_bias, c_w_norm, c_w_out,
              ffn_w_up, ffn_conv_w, ffn_conv_b, ffn_w_down)
    batch = x_prompt.shape[0]
    n_ab, n_c = ab_w_in.shape[0], c_w_in.shape[0]
    zc = jnp.zeros((n_ab, batch, H_A, DK_A, DV_A), F32)
    zn = jnp.zeros((n_ab, batch, H_A, DK_A), F32)
    zm = jnp.zeros((n_ab, batch, H_A), F32)
    zs = jnp.zeros((n_c, batch, H_V_C, DK_C, DV_C), F32)
    zg = jnp.zeros((n_c, batch, CONV_C - 1, CONV_DIM_C), F32)
    outs_p = _trunk(x_prompt, 0, zc, zn, zm, zs, zg, None, None, params, tm=1024)
    past_len = page_table.shape[1] * PAGE_SIZE
    outs_s = _trunk(x_sample, past_len, state_mlstm_c, state_mlstm_n, state_mlstm_m,
                    state_gdn_s, state_gdn_conv, state_ffn_conv, (cache_k, cache_v, page_table), params,
                    tm=x_sample.shape[0] * x_sample.shape[1])
    return (outs_p[0], outs_s[0]) + tuple(outs_p[1:]) + tuple(outs_s[1:])


def _rope_tables(pos):
    half = ROPE_DIMS // 2
    inv = ROPE_THETA ** (-jnp.arange(half, dtype=F32) / half)
    ang = pos.astype(F32)[:, None] * inv[None, :]
    cos, sin = jnp.cos(ang), jnp.sin(ang)
    t = pos.shape[0]
    cosf = jnp.concatenate([cos, cos, jnp.ones((t, HD_B - ROPE_DIMS), F32)], axis=-1)
    sina = jnp.concatenate([-sin, jnp.zeros((t, HD_B - half), F32)], axis=-1)
    sinb = jnp.concatenate([jnp.zeros((t, half), F32), sin, jnp.zeros((t, HD_B - ROPE_DIMS), F32)], axis=-1)
    return cosf, sina, sinb


def _rope_head(x, cosf, sina, sinb):
    half = ROPE_DIMS // 2
    return x * cosf + pltpu.roll(x, HD_B - half, 1) * sina + pltpu.roll(x, half, 1) * sinb


def _moba_prep_kernel(q_ref, k_ref, v_ref, cos_ref, sina_ref, sinb_ref,
                      qr_ref, kr_ref, vr_ref, kb_ref, vb_ref, km_ref):
    cosf, sina, sinb = cos_ref[...], sina_ref[...], sinb_ref[...]
    for h in range(H_B):
        sl = slice(h * HD_B, (h + 1) * HD_B)
        qr_ref[:, sl] = _rope_head(q_ref[:, sl], cosf, sina, sinb)
        kr = _rope_head(k_ref[:, sl], cosf, sina, sinb)
        kr_ref[:, sl] = kr
        kb_ref[:, sl] = kr.astype(BF16)
        km_ref[:, sl] = jnp.broadcast_to(jnp.mean(kr, axis=0, keepdims=True), (8, HD_B))
    v = v_ref[...]
    vr_ref[...] = v
    vb_ref[...] = v.astype(BF16)


def _moba_prep(main, seq_len):
    m = main.shape[0]
    hw = H_B * HD_B
    nblk = seq_len // MOBA_BLOCK
    cosf, sina, sinb = _rope_tables(jnp.arange(seq_len, dtype=jnp.int32))
    c0 = A_COLS // hw
    row = pl.BlockSpec((MOBA_BLOCK, hw), lambda i: (i, 0))
    tab = pl.BlockSpec((MOBA_BLOCK, HD_B), lambda i: (i % nblk, 0))
    return pl.pallas_call(
        _moba_prep_kernel,
        grid=(m // MOBA_BLOCK,),
        in_specs=[pl.BlockSpec((MOBA_BLOCK, hw), lambda i: (i, c0)),
                  pl.BlockSpec((MOBA_BLOCK, hw), lambda i: (i, c0 + 1)),
                  pl.BlockSpec((MOBA_BLOCK, hw), lambda i: (i, c0 + 2)),
                  tab, tab, tab],
        out_specs=[row, row, row, row, row, pl.BlockSpec((None, 8, hw), lambda i: (i, 0, 0))],
        out_shape=[jax.ShapeDtypeStruct((m, hw), F32), jax.ShapeDtypeStruct((m, hw), F32),
                   jax.ShapeDtypeStruct((m, hw), F32), jax.ShapeDtypeStruct((m, hw), BF16),
                   jax.ShapeDtypeStruct((m, hw), BF16), jax.ShapeDtypeStruct((m // MOBA_BLOCK, 8, hw), F32)],
        compiler_params=_params("parallel"),
        name="moba_prep",
    )(main, main, main, cosf, sina, sinb)


def _moba_kernel(q_ref, k_ref, v_ref, km_ref, o_ref, *, nblk):
    qi = pl.program_id(2)
    q = q_ref[...]
    gate = lax.dot_general(q, km_ref[...], (((1,), (1,)), ((), ())),
                           precision=lax.Precision.HIGHEST, preferred_element_type=F32)
    blk = lax.broadcasted_iota(jnp.int32, gate.shape, 1)
    blkf = blk.astype(F32)
    past = blk < qi
    g = jnp.where(past, gate, NEG)
    sel = jnp.zeros(gate.shape, F32)
    for _ in range(min(MOBA_TOPK, nblk)):
        mx = jnp.max(g, axis=1, keepdims=True)
        hit = blkf == jnp.min(jnp.where(g == mx, blkf, float(nblk)), axis=1, keepdims=True)
        sel = jnp.where(hit, 1.0, sel)
        g = jnp.where(hit, -jnp.inf, g)
    sel = jnp.where(past, sel, 0.0)

    qb = q.astype(BF16)
    scale = HD_B ** -0.5

    def blocks(j, n):
        rows = pl.ds(pl.multiple_of(j * MOBA_BLOCK, MOBA_BLOCK), n * MOBA_BLOCK)
        s = lax.dot_general(qb, k_ref[rows, :], (((1,), (1,)), ((), ())), preferred_element_type=F32) * scale
        return s, v_ref[rows, :]

    s, vj = blocks(qi, 1)
    r = lax.broadcasted_iota(jnp.int32, s.shape, 0)
    c = lax.broadcasted_iota(jnp.int32, s.shape, 1)
    s = jnp.where(c <= r, s, NEG)
    m0 = jnp.max(s, axis=1, keepdims=True)
    p = jnp.exp(s - m0)
    l0 = jnp.sum(p, axis=1, keepdims=True)
    acc0 = _bdot(p.astype(BF16), vj)
    first = lax.broadcasted_iota(jnp.int32, (MOBA_BLOCK, 2 * MOBA_BLOCK), 1) < MOBA_BLOCK

    def body(jj, carry):
        m, l, acc = carry
        sel0 = jnp.sum(jnp.where(blk == 2 * jj, sel, 0.0), axis=1, keepdims=True)
        sel1 = jnp.sum(jnp.where(blk == 2 * jj + 1, sel, 0.0), axis=1, keepdims=True)
        s, vj = blocks(2 * jj, 2)
        s = jnp.where(jnp.where(first, sel0, sel1) > 0.5, s, NEG)
        m_new = jnp.maximum(m, jnp.max(s, axis=1, keepdims=True))
        alpha = jnp.exp(m - m_new)
        p = jnp.exp(s - m_new)
        return m_new, alpha * l + jnp.sum(p, axis=1, keepdims=True), alpha * acc + _bdot(p.astype(BF16), vj)

    m, l, acc = lax.fori_loop(0, (qi + 1) // 2, body, (m0, l0, acc0))
    o_ref[...] = (acc / l).astype(BF16)


def _moba_attention_prompt(q_rot, k_bf, v_bf, kmean, n_seq, seq_len):
    m = q_rot.shape[0]
    nblk = seq_len // MOBA_BLOCK
    return pl.pallas_call(
        functools.partial(_moba_kernel, nblk=nblk),
        grid=(n_seq, H_B, nblk),
        in_specs=[pl.BlockSpec((MOBA_BLOCK, HD_B), lambda b, h, i: (b * nblk + i, h)),
                  pl.BlockSpec((seq_len, HD_B), lambda b, h, i: (b, h)),
                  pl.BlockSpec((seq_len, HD_B), lambda b, h, i: (b, h)),
                  pl.BlockSpec((None, nblk, HD_B), lambda b, h, i: (b, 0, h))],
        out_specs=pl.BlockSpec((MOBA_BLOCK, HD_B), lambda b, h, i: (b * nblk + i, h)),
        out_shape=jax.ShapeDtypeStruct((m, H_B * HD_B), BF16),
        compiler_params=_params("parallel", "parallel", "arbitrary"),
        name="moba_attention",
    )(q_rot, k_bf, v_bf, kmean)


MLSTM_ROWS = 128
GATE_LANES = 128


def _mlstm_kernel(q_ref, k_ref, v_ref, og_ref, g_ref, bias_ref, wmh_ref, h_ref, c_ref, n_ref, m_ref,
                  cs_ref, ns_ref, ms_ref):
    r = pl.program_id(1)
    L = q_ref.shape[0]

    @pl.when(r == 0)
    def _():
        cs_ref[...] = jnp.zeros(cs_ref.shape, F32)
        ns_ref[...] = jnp.zeros(ns_ref.shape, F32)
        ms_ref[...] = jnp.zeros(ms_ref.shape, F32)

    gts = g_ref[...] + bias_ref[...]
    lane = lax.broadcasted_iota(jnp.int32, gts.shape, 1)
    lg = jnp.where(lane < H_A, gts, -_softplus(-gts))
    ti = lax.broadcasted_iota(jnp.int32, (L, L), 0)
    si = lax.broadcasted_iota(jnp.int32, (L, L), 1)
    incl = si <= ti
    bmat = jnp.dot(jnp.where(incl, 1.0, 0.0), lg, precision=lax.Precision.HIGHEST, preferred_element_type=F32)
    lgt, bt = lg.T, bmat.T
    nt = (((1,), (1,)), ((), ()))
    tn = (((0,), (0,)), ((), ()))

    for h in range(H_A):
        bcol, brow = bmat[:, H_A + h:H_A + h + 1], bt[H_A + h:H_A + h + 1, :]
        licol, lirow = lg[:, h:h + 1], lgt[h:h + 1, :]
        m_prev = ms_ref[h:h + 1, 0:1]
        d = jnp.where(incl, bcol - brow + lirow, NEG)
        inter = bcol + m_prev
        mt = jnp.maximum(inter, jnp.max(d, axis=1, keepdims=True))
        qh = q_ref[:, h * DK_A:(h + 1) * DK_A]
        kh = k_ref[:, h * DK_A:(h + 1) * DK_A] * (DK_A ** -0.5)
        vb = v_ref[:, h * DV_A:(h + 1) * DV_A].astype(BF16)
        qb = qh.astype(BF16)
        s = lax.dot_general(qb, kh.astype(BF16), nt, preferred_element_type=F32) * jnp.exp(d - mt)
        wi = jnp.exp(inter - mt)
        c_old = cs_ref[h]
        n_old = ns_ref[h:h + 1, :]
        num = wi * _bdot(qb, c_old.astype(BF16)) + _bdot(s.astype(BF16), vb)
        den = wi * jnp.sum(qh * n_old, axis=1, keepdims=True) + jnp.sum(s, axis=1, keepdims=True)
        hh = num / jnp.maximum(jnp.abs(den), jnp.exp(-mt))
        hs = slice(h * DV_A, (h + 1) * DV_A)
        og = og_ref[:, hs]
        h_ref[:, hs] = (_rms(hh, wmh_ref[:, hs]) / (1.0 + jnp.exp(-og))).astype(BF16)
        b_last = bcol[L - 1:L]
        gcol = b_last - bcol + licol
        m_new = jnp.maximum(b_last + m_prev, jnp.max(gcol, axis=0, keepdims=True))
        wc = jnp.exp(b_last + m_prev - m_new)
        kws = kh * jnp.exp(gcol - m_new)
        cs_ref[h] = wc * c_old + lax.dot_general(kws.astype(BF16), vb, tn, preferred_element_type=F32)
        ns_ref[h:h + 1, :] = wc * n_old + jnp.sum(kws, axis=0, keepdims=True)
        ms_ref[h:h + 1, :] = jnp.broadcast_to(m_new, (1, ms_ref.shape[1]))

    @pl.when(r == pl.num_programs(1) - 1)
    def _():
        c_ref[...] = cs_ref[...]
        n_ref[...] = ns_ref[...]
        m_ref[...] = ms_ref[...]


def _mlstm_prompt(main, gates, b_i, b_f, w_mh, n_seq, seq_len):
    m = main.shape[0]
    rows = MLSTM_ROWS
    nr = seq_len // rows
    qkw, vw = H_A * DK_A, H_A * DV_A
    bias = jnp.concatenate([b_i, b_f, jnp.zeros((GATE_LANES - 2 * H_A,), F32)]).reshape(1, GATE_LANES)
    return pl.pallas_call(
        _mlstm_kernel,
        grid=(n_seq, nr),
        in_specs=[pl.BlockSpec((rows, qkw), lambda b, r: (b * nr + r, 0)),
                  pl.BlockSpec((rows, qkw), lambda b, r: (b * nr + r, 1)),
                  pl.BlockSpec((rows, vw), lambda b, r: (b * nr + r, 1)),
                  pl.BlockSpec((rows, vw), lambda b, r: (b * nr + r, 2)),
                  pl.BlockSpec((rows, GATE_LANES), lambda b, r: (b * nr + r, 0)),
                  pl.BlockSpec((1, GATE_LANES), lambda b, r: (0, 0)),
                  pl.BlockSpec((1, vw), lambda b, r: (0, 0))],
        out_specs=[pl.BlockSpec((rows, vw), lambda b, r: (b * nr + r, 0)),
                   pl.BlockSpec((None, H_A, DK_A, DV_A), lambda b, r: (b, 0, 0, 0)),
                   pl.BlockSpec((None, H_A, DK_A), lambda b, r: (b, 0, 0)),
                   pl.BlockSpec((None, H_A, GATE_LANES), lambda b, r: (b, 0, 0))],
        out_shape=[jax.ShapeDtypeStruct((m, vw), BF16),
                   jax.ShapeDtypeStruct((n_seq, H_A, DK_A, DV_A), F32),
                   jax.ShapeDtypeStruct((n_seq, H_A, DK_A), F32),
                   jax.ShapeDtypeStruct((n_seq, H_A, GATE_LANES), F32)],
        scratch_shapes=[pltpu.VMEM((H_A, DK_A, DV_A), F32), pltpu.VMEM((H_A, DK_A), F32),
                        pltpu.VMEM((H_A, GATE_LANES), F32)],
        compiler_params=_params("parallel", "arbitrary"),
        name="mlstm_prompt",
    )(main, main, main, main, gates, bias, w_mh.reshape(1, vw))


GDN_ROWS = 256
GDN_REP = H_V_C // H_QK_C
CONV_HALO = 8


def _silu(x):
    return x / (1.0 + jnp.exp(-x))


def _softplus(x):
    return jnp.maximum(x, 0.0) + jnp.log1p(jnp.exp(-jnp.abs(x)))


def _conv_silu(halo, x, w):
    ext = jnp.concatenate([halo, x], axis=0)
    y = pltpu.roll(ext, CONV_C - 1, 0) * w[0:1]
    for i in range(1, CONV_C - 1):
        y = y + pltpu.roll(ext, CONV_C - 1 - i, 0) * w[i:i + 1]
    y = y + ext * w[CONV_C - 1:CONV_C]
    return _silu(y[halo.shape[0]:])


def _gdn_kernel(qh_ref, kh_ref, vh_ref, q_ref, k_ref, v_ref, z_ref, t_ref, cwq_ref, cwk_ref, cwv_ref,
                al_ref, dt_ref, wn_ref, o_ref, s_ref, st_ref):
    j = pl.program_id(1)
    r = pl.program_id(2)
    L = CHUNK
    rows = q_ref.shape[0]

    @pl.when(r == 0)
    def _():
        st_ref[...] = jnp.zeros(st_ref.shape, F32)

    fresh = r == 0

    def halo(ref):
        return jnp.where(fresh, 0.0, ref[...])

    q = _conv_silu(halo(qh_ref), q_ref[...], cwq_ref[...])
    k = _conv_silu(halo(kh_ref), k_ref[...], cwk_ref[...])
    v = _conv_silu(halo(vh_ref), v_ref[...], cwv_ref[...])
    q = q * lax.rsqrt(jnp.sum(q * q, axis=-1, keepdims=True) + EPS) * (DK_C ** -0.5)
    k = k * lax.rsqrt(jnp.sum(k * k, axis=-1, keepdims=True) + EPS)

    t = t_ref[...]
    lane = lax.broadcasted_iota(jnp.int32, t.shape, 1)
    y = jnp.where(lane < H_V_C, 1.0 / (1.0 + jnp.exp(-t)), -jnp.exp(al_ref[...]) * _softplus(t + dt_ref[...]))
    lane128 = lax.broadcasted_iota(jnp.int32, (rows, 128), 1)
    cols = jnp.zeros((rows, 128), F32)
    betas = []
    for e in range(GDN_REP):
        hv = j * GDN_REP + e
        betas.append(jnp.sum(jnp.where(lane == hv, y, 0.0), axis=1, keepdims=True))
        g_e = jnp.sum(jnp.where(lane == H_V_C + hv, y, 0.0), axis=1, keepdims=True)
        hi = g_e.astype(BF16).astype(F32)
        mid = (g_e - hi).astype(BF16).astype(F32)
        for piece, val in enumerate((hi, mid, g_e - hi - mid)):
            cols = jnp.where(lane128 == piece * GDN_REP + e, val, cols)
    ri = lax.broadcasted_iota(jnp.int32, (rows, rows), 0)
    ci = lax.broadcasted_iota(jnp.int32, (rows, rows), 1)
    cum = jnp.where((ri // L == ci // L) & (ci <= ri), 1.0, 0.0).astype(BF16)
    pieces = _bdot(cum, cols.astype(BF16))
    dcols = pieces + pltpu.roll(pieces, 128 - GDN_REP, 1) + pltpu.roll(pieces, 128 - 2 * GDN_REP, 1)
    drows = dcols.T

    ti = lax.broadcasted_iota(jnp.int32, (L, L), 0)
    si = lax.broadcasted_iota(jnp.int32, (L, L), 1)
    strict, incl = si < ti, si <= ti
    eye = jnp.where(si == ti, 1.0, 0.0)
    wn = wn_ref[...]
    nt = (((1,), (1,)), ((), ()))
    tn = (((0,), (0,)), ((), ()))
    n_chunks = rows // L
    units = [(c, e) for c in range(n_chunks) for e in range(GDN_REP)]

    kk, qk = [], []
    for c in range(n_chunks):
        kcb = k[c * L:(c + 1) * L].astype(BF16)
        kk.append(lax.dot_general(kcb, kcb, nt, preferred_element_type=F32))
        qk.append(lax.dot_general(q[c * L:(c + 1) * L].astype(BF16), kcb, nt, preferred_element_type=F32))
    dcol, dlast, bcol, attn, tinv, p = {}, {}, {}, {}, {}, {}
    for c, e in units:
        cs = slice(c * L, (c + 1) * L)
        dcol[c, e] = dcols[cs, e:e + 1]
        dlast[c, e] = dcols[(c + 1) * L - 1:(c + 1) * L, e:e + 1]
        bcol[c, e] = betas[e][cs]
        ex = jnp.exp(jnp.where(incl, dcol[c, e] - drows[e:e + 1, cs], 0.0))
        p[c, e] = jnp.where(strict, -(bcol[c, e] * kk[c] * ex), 0.0)
        attn[c, e] = jnp.where(incl, qk[c] * ex, 0.0).astype(BF16)
        tinv[c, e] = eye + p[c, e]
    for _ in range(int(math.log2(L)) - 1):
        for u_ in units:
            pb = p[u_].astype(BF16)
            p[u_] = _bdot(pb, pb)
        for u_ in units:
            tinv[u_] = tinv[u_] + _bdot(tinv[u_].astype(BF16), p[u_].astype(BF16))
    sol, qd = {}, {}
    for c, e in units:
        cs = slice(c * L, (c + 1) * L)
        edc = jnp.exp(dcol[c, e])
        rhs = jnp.concatenate([v[cs, e * DV_C:(e + 1) * DV_C] * bcol[c, e], k[cs] * (bcol[c, e] * edc)], axis=1)
        sol[c, e] = _bdot(tinv[c, e].astype(BF16), rhs.astype(BF16))
        qd[c, e] = q[cs] * edc
    o_const, q_eff, s_mat, s_add = {}, {}, {}, {}
    for c, e in units:
        solb = sol[c, e].astype(BF16)
        au = _bdot(attn[c, e], solb)
        o_const[c, e] = au[:, :DV_C]
        q_eff[c, e] = (qd[c, e] - au[:, DV_C:]).astype(BF16)
        kd = (k[c * L:(c + 1) * L] * jnp.exp(dlast[c, e] - dcol[c, e])).astype(BF16)
        ks = lax.dot_general(kd, solb, tn, preferred_element_type=F32)
        s_add[c, e] = ks[:, :DV_C]
        s_mat[c, e] = ks[:, DV_C:].astype(BF16)

    for e in range(GDN_REP):
        s = st_ref[e]
        for c in range(n_chunks):
            sb = s.astype(BF16)
            o = _bdot(q_eff[c, e], sb) + o_const[c, e]
            s = jnp.exp(dlast[c, e]) * s - _bdot(s_mat[c, e], sb) + s_add[c, e]
            zc = z_ref[c * L:(c + 1) * L, e * DV_C:(e + 1) * DV_C]
            o_ref[c * L:(c + 1) * L, e * DV_C:(e + 1) * DV_C] = (_rms(o, wn) * _silu(zc)).astype(BF16)
        st_ref[e] = s

    @pl.when(r == pl.num_programs(2) - 1)
    def _():
        s_ref[...] = st_ref[...]


def _gdn_prompt(main, tail, conv_w, a_log, dt_bias, w_norm, n_seq, seq_len):
    m = main.shape[0]
    rows = GDN_ROWS
    nr = seq_len // rows
    hb = rows // CONV_HALO
    qkw, vw = DK_C, GDN_REP * DV_C
    k0 = H_QK_C
    v0 = 2 * H_QK_C * DK_C // vw
    z0 = CONV_DIM_C // vw
    zeros = jnp.zeros((H_V_C,), F32)
    al = jnp.concatenate([zeros, a_log]).reshape(1, 2 * H_V_C)
    dt = jnp.concatenate([zeros, dt_bias]).reshape(1, 2 * H_V_C)

    def row_idx(b, r):
        return b * nr + r

    def halo_idx(b, r):
        return jnp.maximum(row_idx(b, r) * hb - 1, 0)

    return pl.pallas_call(
        _gdn_kernel,
        grid=(n_seq, H_QK_C, nr),
        in_specs=[pl.BlockSpec((CONV_HALO, qkw), lambda b, j, r: (halo_idx(b, r), j)),
                  pl.BlockSpec((CONV_HALO, qkw), lambda b, j, r: (halo_idx(b, r), k0 + j)),
                  pl.BlockSpec((CONV_HALO, vw), lambda b, j, r: (halo_idx(b, r), v0 + j)),
                  pl.BlockSpec((rows, qkw), lambda b, j, r: (row_idx(b, r), j)),
                  pl.BlockSpec((rows, qkw), lambda b, j, r: (row_idx(b, r), k0 + j)),
                  pl.BlockSpec((rows, vw), lambda b, j, r: (row_idx(b, r), v0 + j)),
                  pl.BlockSpec((rows, vw), lambda b, j, r: (row_idx(b, r), z0 + j)),
                  pl.BlockSpec((rows, 2 * H_V_C), lambda b, j, r: (row_idx(b, r), 0)),
                  pl.BlockSpec((CONV_C, qkw), lambda b, j, r: (0, j)),
                  pl.BlockSpec((CONV_C, qkw), lambda b, j, r: (0, k0 + j)),
                  pl.BlockSpec((CONV_C, vw), lambda b, j, r: (0, v0 + j)),
                  pl.BlockSpec((1, 2 * H_V_C), lambda b, j, r: (0, 0)),
                  pl.BlockSpec((1, 2 * H_V_C), lambda b, j, r: (0, 0)),
                  pl.BlockSpec((1, DV_C), lambda b, j, r: (0, 0))],
        out_specs=[pl.BlockSpec((rows, vw), lambda b, j, r: (row_idx(b, r), j)),
                   pl.BlockSpec((None, GDN_REP, DK_C, DV_C), lambda b, j, r: (b, j, 0, 0))],
        out_shape=[jax.ShapeDtypeStruct((m, H_V_C * DV_C), BF16),
                   jax.ShapeDtypeStruct((n_seq, H_V_C, DK_C, DV_C), F32)],
        scratch_shapes=[pltpu.VMEM((GDN_REP, DK_C, DV_C), F32)],
        compiler_params=_params("parallel", "parallel", "arbitrary"),
        name="gdn_prompt",
    )(main, main, main, main, main, main, main, tail, conv_w, conv_w, conv_w, al, dt, w_norm.reshape(1, DV_C))


def _to_col(row):
    n = row.shape[1]
    i = lax.broadcasted_iota(jnp.int32, (n, n), 0)
    j = lax.broadcasted_iota(jnp.int32, (n, n), 1)
    return jnp.sum(jnp.where(i == j, jnp.broadcast_to(row, (n, n)), 0.0), axis=1, keepdims=True)


def _sigmoid(x):
    return 1.0 / (1.0 + jnp.exp(-x))


def _mlstm_step_kernel(q_ref, k_ref, v_ref, og_ref, g_ref, bias_ref, wmh_ref, c0_ref, n0_ref, m0_ref,
                       h_ref, c_ref, n_ref, m_ref):
    nb = q_ref.shape[0]
    gts = g_ref[...] + bias_ref[...]
    lane = lax.broadcasted_iota(jnp.int32, gts.shape, 1)
    m_out = jnp.zeros(gts.shape, F32)
    for h in range(H_A):
        li = gts[:, h:h + 1]
        lf = -_softplus(-gts[:, H_A + h:H_A + h + 1])
        m_prev = m0_ref[:, h:h + 1]
        inter = lf + m_prev
        mt = jnp.maximum(inter, li)
        qh = q_ref[:, h * DK_A:(h + 1) * DK_A]
        kh = k_ref[:, h * DK_A:(h + 1) * DK_A] * (DK_A ** -0.5)
        hs = slice(h * DV_A, (h + 1) * DV_A)
        vh = v_ref[:, hs]
        s = jnp.sum(qh * kh, axis=1, keepdims=True) * jnp.exp(li - mt)
        wi = jnp.exp(inter - mt)
        m_new = jnp.maximum(lf + m_prev, li)
        wc = jnp.exp(lf + m_prev - m_new)
        kws = kh * jnp.exp(li - m_new)
        n_old = n0_ref[:, h, :]
        qc_rows = []
        for b in range(nb):
            c_old = c0_ref[b, h]
            qc_rows.append(jnp.sum(_to_col(qh[b:b + 1]) * c_old, axis=0, keepdims=True))
            c_ref[b, h] = wc[b:b + 1] * c_old + _to_col(kws[b:b + 1]) * vh[b:b + 1]
        num = wi * jnp.concatenate(qc_rows, axis=0) + s * vh
        den = wi * jnp.sum(qh * n_old, axis=1, keepdims=True) + s
        hh = num / jnp.maximum(jnp.abs(den), jnp.exp(-mt))
        h_ref[:, hs] = _rms(hh, wmh_ref[:, hs]) * _sigmoid(og_ref[:, hs])
        n_ref[:, h, :] = wc * n_old + kws
        m_out = jnp.where(lane == h, m_new, m_out)
    m_ref[...] = m_out


def _mlstm_step(main, gates, b_i, b_f, w_mh, c0, n0, m0):
    nb = main.shape[0]
    qkw, vw = H_A * DK_A, H_A * DV_A
    bias = jnp.concatenate([b_i, b_f, jnp.zeros((GATE_LANES - 2 * H_A,), F32)]).reshape(1, GATE_LANES)
    full = lambda shape: pl.BlockSpec(shape, lambda i: (0,) * len(shape))
    return pl.pallas_call(
        _mlstm_step_kernel,
        grid=(1,),
        in_specs=[pl.BlockSpec((nb, qkw), lambda i: (0, 0)),
                  pl.BlockSpec((nb, qkw), lambda i: (0, 1)),
                  pl.BlockSpec((nb, vw), lambda i: (0, 1)),
                  pl.BlockSpec((nb, vw), lambda i: (0, 2)),
                  full((nb, GATE_LANES)), full((1, GATE_LANES)), full((1, vw)),
                  full((nb, H_A, DK_A, DV_A)), full((nb, H_A, DK_A)), full((nb, H_A))],
        out_specs=[full((nb, vw)), full((nb, H_A, DK_A, DV_A)), full((nb, H_A, DK_A)), full((nb, GATE_LANES))],
        out_shape=[jax.ShapeDtypeStruct((nb, vw), F32),
                   jax.ShapeDtypeStruct((nb, H_A, DK_A, DV_A), F32),
                   jax.ShapeDtypeStruct((nb, H_A, DK_A), F32),
                   jax.ShapeDtypeStruct((nb, GATE_LANES), F32)],
        compiler_params=_params("arbitrary"),
        name="mlstm_step",
    )(main, main, main, main, gates, bias, w_mh.reshape(1, vw), c0, n0, m0)


def _gdn_step_kernel(x_ref, t_ref, buf_ref, cw_ref, al_ref, dt_ref, wn_ref, s0_ref, o_ref, cn_ref, s_ref):
    x = x_ref[...]
    xc = x[:, :CONV_DIM_C]
    buf = buf_ref[...]
    cw = cw_ref[...]
    y = buf[0:1] * cw[0:1]
    for i in range(1, CONV_C - 1):
        y = y + buf[i:i + 1] * cw[i:i + 1]
    y = _silu(y + xc * cw[CONV_C - 1:CONV_C])
    cn_ref[...] = jnp.concatenate([buf[1:], xc], axis=0)
    t = t_ref[...]
    lane = lax.broadcasted_iota(jnp.int32, t.shape, 1)
    gts = jnp.where(lane < H_V_C, _sigmoid(t), -jnp.exp(al_ref[...]) * _softplus(t + dt_ref[...]))
    wn = wn_ref[...]
    k0, v0 = H_QK_C * DK_C, 2 * H_QK_C * DK_C
    for j in range(H_QK_C):
        qj = y[:, j * DK_C:(j + 1) * DK_C]
        kj = y[:, k0 + j * DK_C:k0 + (j + 1) * DK_C]
        qj = qj * lax.rsqrt(jnp.sum(qj * qj, axis=-1, keepdims=True) + EPS) * (DK_C ** -0.5)
        kj = kj * lax.rsqrt(jnp.sum(kj * kj, axis=-1, keepdims=True) + EPS)
        qcol, kcol = _to_col(qj), _to_col(kj)
        qk = jnp.sum(qj * kj, axis=1, keepdims=True)
        for e in range(GDN_REP):
            hv = j * GDN_REP + e
            hs = slice(hv * DV_C, (hv + 1) * DV_C)
            beta = gts[:, hv:hv + 1]
            eg = jnp.exp(gts[:, H_V_C + hv:H_V_C + hv + 1])
            s_old = s0_ref[hv]
            ks = jnp.sum(kcol * s_old, axis=0, keepdims=True)
            qs = jnp.sum(qcol * s_old, axis=0, keepdims=True)
            v_new = y[:, v0 + hv * DV_C:v0 + (hv + 1) * DV_C] * beta - (beta * eg) * ks
            o = eg * qs + qk * v_new
            s_ref[hv] = eg * s_old + kcol * v_new
            o_ref[:, hs] = _rms(o, wn) * _silu(x[:, CONV_DIM_C + hv * DV_C:CONV_DIM_C + (hv + 1) * DV_C])


def _gdn_step(main, tail, conv_buf, conv_w, a_log, dt_bias, w_norm, s0):
    nb, n_main = main.shape
    zeros = jnp.zeros((H_V_C,), F32)
    al = jnp.concatenate([zeros, a_log]).reshape(1, 2 * H_V_C)
    dt = jnp.concatenate([zeros, dt_bias]).reshape(1, 2 * H_V_C)
    hw = H_V_C * DV_C
    const = lambda shape: pl.BlockSpec(shape, lambda b: (0,) * len(shape))
    o, conv_new, s = pl.pallas_call(
        _gdn_step_kernel,
        grid=(nb,),
        in_specs=[pl.BlockSpec((None, 1, n_main), lambda b: (b, 0, 0)),
                  pl.BlockSpec((None, 1, 2 * H_V_C), lambda b: (b, 0, 0)),
                  pl.BlockSpec((None, CONV_C - 1, CONV_DIM_C), lambda b: (b, 0, 0)),
                  const((CONV_C, CONV_DIM_C)), const((1, 2 * H_V_C)), const((1, 2 * H_V_C)), const((1, DV_C)),
                  pl.BlockSpec((None, H_V_C, DK_C, DV_C), lambda b: (b, 0, 0, 0))],
        out_specs=[pl.BlockSpec((None, 1, hw), lambda b: (b, 0, 0)),
                   pl.BlockSpec((None, CONV_C - 1, CONV_DIM_C), lambda b: (b, 0, 0)),
                   pl.BlockSpec((None, H_V_C, DK_C, DV_C), lambda b: (b, 0, 0, 0))],
        out_shape=[jax.ShapeDtypeStruct((nb, 1, hw), F32),
                   jax.ShapeDtypeStruct((nb, CONV_C - 1, CONV_DIM_C), F32),
                   jax.ShapeDtypeStruct((nb, H_V_C, DK_C, DV_C), F32)],
        compiler_params=_params("parallel"),
        name="gdn_step",
    )(main.reshape(nb, 1, n_main), tail.reshape(nb, 1, 2 * H_V_C), conv_buf, conv_w, al, dt,
      w_norm.reshape(1, DV_C), s0)
    return o.reshape(nb, hw), s, conv_new


SCAN_PAGES = 8
PAGES_PER_BLOCK = MOBA_BLOCK // PAGE_SIZE


def _rope_rows_kernel(q_ref, k_ref, v_ref, cos_ref, sina_ref, sinb_ref, qr_ref, kr_ref, vr_ref):
    cosf, sina, sinb = cos_ref[...], sina_ref[...], sinb_ref[...]
    for h in range(H_B):
        sl = slice(h * HD_B, (h + 1) * HD_B)
        qr_ref[:, sl] = _rope_head(q_ref[:, sl], cosf, sina, sinb)
        kr_ref[:, sl] = _rope_head(k_ref[:, sl], cosf, sina, sinb)
    vr_ref[...] = v_ref[...]


def _rope_rows(main, pos):
    nb = main.shape[0]
    hw = H_B * HD_B
    c0 = A_COLS // hw
    cosf, sina, sinb = _rope_tables(jnp.full((1,), pos, jnp.int32))
    tab = pl.BlockSpec((1, HD_B), lambda i: (0, 0))
    out = pl.BlockSpec((nb, hw), lambda i: (0, 0))
    return pl.pallas_call(
        _rope_rows_kernel,
        grid=(1,),
        in_specs=[pl.BlockSpec((nb, hw), lambda i: (0, c0)),
                  pl.BlockSpec((nb, hw), lambda i: (0, c0 + 1)),
                  pl.BlockSpec((nb, hw), lambda i: (0, c0 + 2)), tab, tab, tab],
        out_specs=[out, out, out],
        out_shape=[jax.ShapeDtypeStruct((nb, hw), F32)] * 3,
        compiler_params=_params("arbitrary"),
        name="rope_rows",
    )(main, main, main, cosf, sina, sinb)


def _moba_scan_kernel(pt_ref, q_ref, *refs):
    pages, sel_ref, g_ref = refs[:SCAN_PAGES], refs[SCAN_PAGES], refs[SCAN_PAGES + 1]
    gi = pl.program_id(1)

    @pl.when(gi == 0)
    def _():
        g_ref[...] = jnp.full(g_ref.shape, NEG, F32)

    q = q_ref[...]
    lane = lax.broadcasted_iota(jnp.int32, g_ref.shape, 1)
    g = g_ref[...]
    for p in range(SCAN_PAGES // PAGES_PER_BLOCK):
        ksum = jnp.sum(pages[PAGES_PER_BLOCK * p][...], axis=0)
        for i in range(1, PAGES_PER_BLOCK):
            ksum = ksum + jnp.sum(pages[PAGES_PER_BLOCK * p + i][...], axis=0)
        val = jnp.sum(q * (ksum * (1.0 / MOBA_BLOCK)), axis=1, keepdims=True)
        g = jnp.where(lane == gi * (SCAN_PAGES // PAGES_PER_BLOCK) + p, val, g)
    g_ref[...] = g

    @pl.when(gi == pl.num_programs(1) - 1)
    def _():
        lanef = lane.astype(F32)
        gg = g
        out = jnp.zeros(g.shape, F32)
        for r in range(MOBA_TOPK):
            mx = jnp.max(gg, axis=1, keepdims=True)
            idx = jnp.min(jnp.where(gg == mx, lanef, float(g.shape[1])), axis=1, keepdims=True)
            out = jnp.where(lane == r, idx, out)
            gg = jnp.where(lanef == idx, -jnp.inf, gg)
        sel_ref[...] = out.astype(jnp.int32)


def _moba_scan(q_rot, cache_k, layer, page_table):
    nb = q_rot.shape[0]
    hw = H_B * HD_B
    n_pages = page_table.shape[1]
    assert n_pages % SCAN_PAGES == 0 and n_pages // PAGES_PER_BLOCK <= 128
    def page_spec(i):
        return pl.BlockSpec((None, None, PAGE_SIZE, H_B, HD_B),
                            lambda b, g, pt: (layer, pt[b * n_pages + g * SCAN_PAGES + i], 0, 0, 0))

    grid_spec = pltpu.PrefetchScalarGridSpec(
        num_scalar_prefetch=1,
        grid=(nb, n_pages // SCAN_PAGES),
        in_specs=[pl.BlockSpec((None, H_B, HD_B), lambda b, g, pt: (b, 0, 0))]
        + [page_spec(i) for i in range(SCAN_PAGES)],
        out_specs=pl.BlockSpec((None, H_B, 128), lambda b, g, pt: (b, 0, 0)),
        scratch_shapes=[pltpu.VMEM((H_B, 128), F32)],
    )
    return pl.pallas_call(
        _moba_scan_kernel,
        grid_spec=grid_spec,
        out_shape=jax.ShapeDtypeStruct((nb, H_B, 128), jnp.int32),
        compiler_params=_params("parallel", "arbitrary"),
        name="moba_scan",
    )(page_table.reshape(-1), q_rot.reshape(nb, H_B, HD_B), *([cache_k] * SCAN_PAGES))


SEL_PAGES = MOBA_TOPK * PAGES_PER_BLOCK


def _moba_decode_kernel(pg_ref, q_ref, kn_ref, vn_ref, *refs):
    ks, vs, o_ref = refs[:SEL_PAGES], refs[SEL_PAGES:2 * SEL_PAGES], refs[2 * SEL_PAGES]
    q = q_ref[...]
    scale = HD_B ** -0.5
    rid = lax.broadcasted_iota(jnp.int32, (PAGE_SIZE * H_B, 1), 0)
    mine = rid % H_B == pl.program_id(1)
    logits = [jnp.where(mine, jnp.sum(k[...] * q, axis=1, keepdims=True) * scale, NEG) for k in ks]
    own = jnp.sum(q * kn_ref[...], axis=1, keepdims=True) * scale
    m = own
    for s in logits:
        m = jnp.maximum(m, jnp.max(s, axis=0, keepdims=True))
    p_own = jnp.exp(own - m)
    l = p_own
    acc = p_own * vn_ref[...]
    for s, v in zip(logits, vs):
        p = jnp.exp(s - m)
        l = l + jnp.sum(p, axis=0, keepdims=True)
        acc = acc + jnp.sum(p * v[...], axis=0, keepdims=True)
    o_ref[...] = acc / l


def _moba_decode(q_rot, k_new, v_new, cache_k, cache_v, layer, pages):
    nb = q_rot.shape[0]
    hw = H_B * HD_B
    n_layers, n_pool = cache_k.shape[:2]
    kpool = cache_k.reshape(n_layers, n_pool, PAGE_SIZE * H_B, HD_B)
    vpool = cache_v.reshape(n_layers, n_pool, PAGE_SIZE * H_B, HD_B)

    def page_spec(i):
        return pl.BlockSpec((None, None, PAGE_SIZE * H_B, HD_B),
                            lambda b, h, pg: (layer, pg[(b * H_B + h) * SEL_PAGES + i], 0, 0))

    row = pl.BlockSpec((None, 1, HD_B), lambda b, h, pg: (b, 0, h))
    grid_spec = pltpu.PrefetchScalarGridSpec(
        num_scalar_prefetch=1,
        grid=(nb, H_B),
        in_specs=[row, row, row] + [page_spec(i) for i in range(SEL_PAGES)] * 2,
        out_specs=row,
    )
    r3 = lambda a: a.reshape(nb, 1, hw)
    out = pl.pallas_call(
        _moba_decode_kernel,
        grid_spec=grid_spec,
        out_shape=jax.ShapeDtypeStruct((nb, 1, hw), F32),
        compiler_params=_params("parallel", "arbitrary"),
        name="moba_decode",
    )(pages, r3(q_rot), r3(k_new), r3(v_new), *([kpool] * SEL_PAGES), *([vpool] * SEL_PAGES))
    return out.reshape(nb, hw)


def _ffn_mid_kernel(u_ref, buf_ref, cw_ref, cb_ref, act_ref, nb_ref):
    u = u_ref[...]
    cw = cw_ref[...]
    y = buf_ref[:, 0, :] * cw[0:1]
    for i in range(1, FFN_CONV - 1):
        y = y + buf_ref[:, i, :] * cw[i:i + 1]
        nb_ref[:, i - 1, :] = buf_ref[:, i, :]
    y = y + u * cw[FFN_CONV - 1:FFN_CONV] + cb_ref[...]
    nb_ref[:, FFN_CONV - 2, :] = u
    act_ref[...] = _silu(y[:, :D_FF]) * y[:, D_FF:]


def _ffn_mid_step(u, buf, conv_w, conv_b):
    nb = u.shape[0]
    full = lambda shape: pl.BlockSpec(shape, lambda i: (0,) * len(shape))
    return pl.pallas_call(
        _ffn_mid_kernel,
        grid=(1,),
        in_specs=[full((nb, 2 * D_FF)), full((nb, FFN_CONV - 1, 2 * D_FF)), full((FFN_CONV, 2 * D_FF)),
                  full((1, 2 * D_FF))],
        out_specs=[full((nb, D_FF)), full((nb, FFN_CONV - 1, 2 * D_FF))],
        out_shape=[jax.ShapeDtypeStruct((nb, D_FF), F32), jax.ShapeDtypeStruct((nb, FFN_CONV - 1, 2 * D_FF), F32)],
        compiler_params=_params("arbitrary"),
        name="ffn_mid_step",
    )(u, buf, conv_w, conv_b.reshape(1, 2 * D_FF))
```

```python
import functools
import math

import jax
import jax.numpy as jnp
import numpy as np
from jax import lax
from jax.experimental import pallas as pl
from jax.experimental.pallas import tpu as pltpu

D_MODEL = 2048
DEPTH = 4
PAGE_SIZE = 128
H_A, DK_A, DV_A = 8, 64, 128
H_B, HD_B = 8, 128
MOBA_BLOCK, MOBA_TOPK, MOBA_Q_CHUNK = 256, 3, 32
ROPE_DIMS = HD_B // 4
ROPE_THETA = 500000.0
H_QK_C, H_V_C, DK_C, DV_C, CONV_C = 16, 32, 128, 128, 4
CONV_DIM_C = 2 * H_QK_C * DK_C + H_V_C * DV_C
D_FF = 5504
FFN_CONV = 3
CHUNK = 64
EPS = 1e-6
NEG = -1e30
A_COLS = 2 * H_A * DK_A + 2 * H_A * DV_A
B_COLS = 3 * H_B * HD_B
GATE_COLS = 2 * H_A

F32 = jnp.float32
BF16 = jnp.bfloat16
VMEM_LIMIT_BYTES = 56 * 1024 * 1024
HALO = 16


def _params(*sem):
    return pltpu.CompilerParams(dimension_semantics=sem, vmem_limit_bytes=VMEM_LIMIT_BYTES)


def _rms(x, w):
    return x * lax.rsqrt(jnp.mean(x * x, axis=-1, keepdims=True) + EPS) * w


def _bdot(a, b):
    return jnp.dot(a, b, preferred_element_type=F32)


def _nmm_kernel(x_ref, nw_ref, *refs, split):
    wt_ref, o_ref, ot_ref, xn_ref = refs[-4:]
    j = pl.program_id(1)

    @pl.when(j == 0)
    def _():
        xn = _rms(x_ref[...], nw_ref[...]).astype(BF16)
        xn_ref[...] = xn
        ot_ref[...] = _bdot(xn, wt_ref[...].astype(BF16))

    if split is None:
        o_ref[...] = _bdot(xn_ref[...], refs[0][...].astype(BF16))
    else:
        @pl.when(j < split)
        def _():
            o_ref[...] = _bdot(xn_ref[...], refs[0][...].astype(BF16))

        @pl.when(j >= split)
        def _():
            o_ref[...] = _bdot(xn_ref[...], refs[1][...].astype(BF16))


def _norm_matmul(x, nw, sources, layer, tail, *, tm, tn):
    m, k = x.shape
    tiles = [n for _, n in sources]
    n_main = sum(tiles) * tn
    w_tail, nt, bt = tail
    if len(sources) == 1:
        split = None
        w_specs = [pl.BlockSpec((None, k, tn), lambda i, j: (layer, 0, j))]
    else:
        split = tiles[0]
        w_specs = [pl.BlockSpec((None, k, tn), lambda i, j: (layer, 0, jnp.minimum(j, split - 1))),
                   pl.BlockSpec((None, k, tn), lambda i, j: (layer, 0, jnp.maximum(j - split, 0)))]
    return pl.pallas_call(
        functools.partial(_nmm_kernel, split=split),
        grid=(m // tm, n_main // tn),
        in_specs=[pl.BlockSpec((tm, k), lambda i, j: (i, 0)),
                  pl.BlockSpec((1, k), lambda i, j: (0, 0))] + w_specs
        + [pl.BlockSpec((None, k, nt), lambda i, j: (layer, 0, bt))],
        out_specs=[pl.BlockSpec((tm, tn), lambda i, j: (i, j)),
                   pl.BlockSpec((tm, nt), lambda i, j: (i, 0))],
        out_shape=[jax.ShapeDtypeStruct((m, n_main), F32), jax.ShapeDtypeStruct((m, nt), F32)],
        scratch_shapes=[pltpu.VMEM((tm, k), BF16)],
        compiler_params=_params("parallel", "arbitrary"),
        name="norm_matmul",
    )(x, nw.reshape(1, k), *[w for w, _ in sources], w_tail)


def _mmr_kernel(a_ref, w_ref, r_ref, o_ref):
    o_ref[...] = r_ref[...] + _bdot(a_ref[...].astype(BF16), w_ref[...].astype(BF16))


def _matmul_res(a, w, layer, res, *, tm, tn):
    m, k = a.shape
    n = w.shape[2]
    return pl.pallas_call(
        _mmr_kernel,
        grid=(m // tm, n // tn),
        in_specs=[pl.BlockSpec((tm, k), lambda i, j: (i, 0)),
                  pl.BlockSpec((None, k, tn), lambda i, j: (layer, 0, j)),
                  pl.BlockSpec((tm, tn), lambda i, j: (i, j))],
        out_specs=pl.BlockSpec((tm, tn), lambda i, j: (i, j)),
        out_shape=jax.ShapeDtypeStruct((m, n), F32),
        compiler_params=_params("parallel", "arbitrary"),
        name="matmul_res",
    )(a, w, res)


def _mm2r_kernel(a1_ref, a2_ref, w1_ref, w2_ref, r_ref, o_ref):
    o_ref[...] = (r_ref[...] + _bdot(a1_ref[...].astype(BF16), w1_ref[...].astype(BF16))
                  + _bdot(a2_ref[...].astype(BF16), w2_ref[...].astype(BF16)))


def _matmul2_res(a1, a2, w, layer, res, *, tm, tn):
    m, k = a1.shape
    n = w.shape[2]
    return pl.pallas_call(
        _mm2r_kernel,
        grid=(m // tm, n // tn),
        in_specs=[pl.BlockSpec((tm, k), lambda i, j: (i, 0)),
                  pl.BlockSpec((tm, k), lambda i, j: (i, 0)),
                  pl.BlockSpec((None, k, tn), lambda i, j: (layer, 0, j)),
                  pl.BlockSpec((None, k, tn), lambda i, j: (layer, 1, j)),
                  pl.BlockSpec((tm, tn), lambda i, j: (i, j))],
        out_specs=pl.BlockSpec((tm, tn), lambda i, j: (i, j)),
        out_shape=jax.ShapeDtypeStruct((m, n), F32),
        compiler_params=_params("parallel", "arbitrary"),
        name="matmul2_res",
    )(a1, a2, w, w, res)


def _final_norm_kernel(x_ref, w_ref, o_ref):
    o_ref[...] = _rms(x_ref[...], w_ref[...])


def _final_norm(x, w, *, tm):
    m, k = x.shape
    return pl.pallas_call(
        _final_norm_kernel,
        grid=(m // tm,),
        in_specs=[pl.BlockSpec((tm, k), lambda i: (i, 0)), pl.BlockSpec((1, k), lambda i: (0, 0))],
        out_specs=pl.BlockSpec((tm, k), lambda i: (i, 0)),
        out_shape=jax.ShapeDtypeStruct((m, k), F32),
        compiler_params=_params("parallel"),
        name="final_norm",
    )(x, w.reshape(1, k))


LANES = 128


def _ffn_up_kernel(xh_ref, x_ref, nw_ref, *refs, tiles_per_seq, n_sub):
    w_refs = refs[:2 * n_sub]
    cwg_ref, cwu_ref, cbg_ref, cbu_ref, act_ref, lastg_ref, lastu_ref, xn_ref, wb_ref = refs[2 * n_sub:]
    i = pl.program_id(0)
    tn = n_sub * LANES

    @pl.when(pl.program_id(1) == 0)
    def _():
        nw = nw_ref[...]
        xn_ref[HALO:, :] = _rms(x_ref[...], nw).astype(BF16)
        halo = jnp.where(i % tiles_per_seq == 0, 0.0, _rms(xh_ref[...], nw))
        xn_ref[:HALO, :] = halo.astype(BF16)

    for s, w_ref in enumerate(w_refs):
        wb_ref[:, s * LANES:(s + 1) * LANES] = w_ref[...].astype(BF16)
    v = _bdot(xn_ref[...], wb_ref[...])

    def conv(v, cw_ref, cb_ref, last_ref):
        last_ref[...] = v[v.shape[0] - 8:, :]
        cw = cw_ref[...]
        y = pltpu.roll(v, 2, 0) * cw[0:1] + pltpu.roll(v, 1, 0) * cw[1:2] + v * cw[2:3] + cb_ref[...]
        return y[HALO:, :]

    g = conv(v[:, :tn], cwg_ref, cbg_ref, lastg_ref)
    u = conv(v[:, tn:], cwu_ref, cbu_ref, lastu_ref)
    act_ref[...] = (g / (1.0 + jnp.exp(-g)) * u).astype(BF16)


def _ffn_up(x, nw, w_up, layer, conv_w, conv_b, seq_len, *, tm, tn):
    m, k = x.shape
    cwg, cwu = conv_w[:, :D_FF], conv_w[:, D_FF:]
    cbg, cbu = conv_b[:D_FF].reshape(1, D_FF), conv_b[D_FF:].reshape(1, D_FF)
    n_i, n_j = m // tm, pl.cdiv(D_FF, tn)
    hb = tm // HALO
    n_sub = tn // LANES
    half_blocks = D_FF // LANES

    def wspec(half, s):
        return pl.BlockSpec((None, k, LANES), lambda i, j: (
            layer, 0, half * half_blocks + jnp.minimum(j * n_sub + s, half_blocks - 1)))

    cwspec = pl.BlockSpec((FFN_CONV, tn), lambda i, j: (0, j))
    cbspec = pl.BlockSpec((1, tn), lambda i, j: (0, j))
    lastspec = pl.BlockSpec((None, 8, tn), lambda i, j: (i, 0, j))
    act, lastg, lastu = pl.pallas_call(
        functools.partial(_ffn_up_kernel, tiles_per_seq=seq_len // tm, n_sub=n_sub),
        grid=(n_i, n_j),
        in_specs=[pl.BlockSpec((HALO, k), lambda i, j: (jnp.maximum(i * hb - 1, 0), 0)),
                  pl.BlockSpec((tm, k), lambda i, j: (i, 0)),
                  pl.BlockSpec((1, k), lambda i, j: (0, 0))]
        + [wspec(half, s) for half in range(2) for s in range(n_sub)]
        + [cwspec, cwspec, cbspec, cbspec],
        out_specs=[pl.BlockSpec((tm, tn), lambda i, j: (i, j)), lastspec, lastspec],
        out_shape=[jax.ShapeDtypeStruct((m, D_FF), BF16),
                   jax.ShapeDtypeStruct((n_i, 8, D_FF), F32),
                   jax.ShapeDtypeStruct((n_i, 8, D_FF), F32)],
        scratch_shapes=[pltpu.VMEM((HALO + tm, k), BF16), pltpu.VMEM((k, 2 * tn), BF16)],
        compiler_params=_params("parallel", "arbitrary"),
        name="ffn_up",
    )(x, x, nw.reshape(1, k), *([w_up] * (2 * n_sub)), cwg, cwu, cbg, cbu)
    return act, lastg, lastu


def _trunk(x, pos0, c0, n0, m0, s0, gconv0, fconv0, kv_pages, params, *, tm):
    (norm_mix, norm_ffn, norm_final, ab_w_in, ab_b_i, ab_b_f, ab_w_mh, ab_w_out,
     c_w_in, c_conv_w, c_a_log, c_dt_bias, c_w_norm, c_w_out,
     ffn_w_up, ffn_conv_w, ffn_conv_b, ffn_w_down) = params
    ab_w_moba = ab_w_in[:, :, A_COLS + GATE_COLS:]
    c_w_tail = c_w_in[:, :, CONV_DIM_C + H_V_C * DV_C:]
    b, t, d = x.shape
    prompt = kv_pages is None
    rows_k, rows_v, cs, ns, ms, ss, gcs, fcs = [], [], [], [], [], [], [], []
    h = x.reshape(b * t, d)
    for l in range(DEPTH):
        j = l // 2
        if l % 2 == 0:
            main, gates = _norm_matmul(h, norm_mix[l], [(ab_w_in, A_COLS // 512), (ab_w_moba, B_COLS // 512)], j,
                                       (ab_w_in, GATE_LANES, A_COLS // GATE_LANES), tm=tm, tn=512)
            if prompt:
                ha, c, n, m = _mlstm_prompt(main, gates, ab_b_i[j], ab_b_f[j], ab_w_mh[j], b, t)
                q_rot, kn, vn, k_bf, v_bf, km = _moba_prep(main, t)
                kmean = km[:, 0, :].reshape(b, t // MOBA_BLOCK, H_B * HD_B)
                ob = _moba_attention_prompt(q_rot, k_bf, v_bf, kmean, b, t)
                m = m[:, :, 0]
                kn = kn.reshape(b, t, H_B, HD_B)
                vn = vn.reshape(b, t, H_B, HD_B)
                h = _matmul2_res(ha, ob, ab_w_out, j, h, tm=tm, tn=512)
            else:
                cache_k, cache_v, page_table = kv_pages
                ha, c, n, m = _mlstm_step(main, gates, ab_b_i[j], ab_b_f[j], ab_w_mh[j], c0[j], n0[j], m0[j])
                m = m[:, :H_A]
                q_rot, kn, vn = _rope_rows(main, pos0)
                sel = _moba_scan(q_rot, cache_k, j, page_table)[:, :, :MOBA_TOPK]
                page_idx = sel[..., None] * PAGES_PER_BLOCK + jnp.arange(PAGES_PER_BLOCK, dtype=jnp.int32)
                pages = jnp.take_along_axis(page_table[:, None, :], page_idx.reshape(b, H_B, SEL_PAGES), axis=2)
                ob = _moba_decode(q_rot, kn, vn, cache_k, cache_v, j, pages.reshape(-1))
                kn = kn.reshape(b, t, H_B, HD_B)
                vn = vn.reshape(b, t, H_B, HD_B)
                h = _matmul2_res(ha, ob, ab_w_out, j, h, tm=tm, tn=512)
            rows_k.append(kn)
            rows_v.append(vn)
            cs.append(c)
            ns.append(n)
            ms.append(m)
        else:
            n_main = CONV_DIM_C + H_V_C * DV_C
            main, tail = _norm_matmul(h, norm_mix[l], [(c_w_in, n_main // 512)], j,
                                      (c_w_tail, 2 * H_V_C, 0), tm=tm, tn=512)
            if prompt:
                o, s = _gdn_prompt(main, tail, c_conv_w[j], c_a_log[j], c_dt_bias[j], c_w_norm[j], b, t)
                gc = main.reshape(b, t, n_main)[:, t - (CONV_C - 1):, :CONV_DIM_C]
            else:
                o, s, gc = _gdn_step(main, tail, gconv0[j], c_conv_w[j], c_a_log[j], c_dt_bias[j], c_w_norm[j], s0[j])
            ss.append(s)
            gcs.append(gc)
            h = _matmul_res(o, c_w_out, j, h, tm=tm, tn=256)
        if prompt:
            act, lastg, lastu = _ffn_up(h, norm_ffn[l], ffn_w_up, l, ffn_conv_w[l], ffn_conv_b[l], t, tm=tm, tn=512)
            tps = t // tm
            last = jnp.concatenate([lastg[tps - 1::tps, 8 - (FFN_CONV - 1):], lastu[tps - 1::tps, 8 - (FFN_CONV - 1):]],
                                   axis=-1)
            fcs.append(last)
        else:
            n_tiles = (2 * D_FF) // 512
            n_tail = 2 * D_FF - n_tiles * 512
            main, tail = _norm_matmul(h, norm_ffn[l], [(ffn_w_up, n_tiles)], l,
                                      (ffn_w_up, n_tail, n_tiles * 512 // n_tail), tm=tm, tn=512)
            act, fc = _ffn_mid_step(jnp.concatenate([main, tail], axis=-1), fconv0[l], ffn_conv_w[l], ffn_conv_b[l])
            fcs.append(fc)
        h = _matmul_res(act, ffn_w_down, l, h, tm=tm, tn=256)
    y = _final_norm(h, norm_final, tm=tm).reshape(b, t, d)
    return (y, jnp.stack(rows_k), jnp.stack(rows_v), jnp.stack(cs), jnp.stack(ns), jnp.stack(ms),
            jnp.stack(ss), jnp.stack(gcs), jnp.stack(fcs))


def kernel(x_prompt, x_sample, cache_k, cache_v, page_table, state_mlstm_c, state_mlstm_n, state_mlstm_m,
           state_gdn_s, state_gdn_conv, state_ffn_conv, norm_mix, norm_ffn, norm_final,
           ab_w_in, ab_b_i, ab_b_f, ab_w_mh, ab_w_out, c_w_in, c_conv_w, c_a_log, c_dt_bias, c_w_norm, c_w_out,
           ffn_w_up, ffn_conv_w, ffn_conv_b, ffn_w_down):
    params = (norm_mix, norm_ffn, norm_final, ab_w_in, ab_b_i, ab_b_f, ab_w_mh, ab_w_out,
              c_w_in, c_conv_w, c_a_log, c_dt_bias, c_w_norm, c_w_out,
              ffn_w_up, ffn_conv_w, ffn_conv_b, ffn_w_down)
    batch = x_prompt.shape[0]
    n_ab, n_c = ab_w_in.shape[0], c_w_in.shape[0]
    zc = jnp.zeros((n_ab, batch, H_A, DK_A, DV_A), F32)
    zn = jnp.zeros((n_ab, batch, H_A, DK_A), F32)
    zm = jnp.zeros((n_ab, batch, H_A), F32)
    zs = jnp.zeros((n_c, batch, H_V_C, DK_C, DV_C), F32)
    zg = jnp.zeros((n_c, batch, CONV_C - 1, CONV_DIM_C), F32)
    outs_p = _trunk(x_prompt, 0, zc, zn, zm, zs, zg, None, None, params, tm=1024)
    past_len = page_table.shape[1] * PAGE_SIZE
    outs_s = _trunk(x_sample, past_len, state_mlstm_c, state_mlstm_n, state_mlstm_m,
                    state_gdn_s, state_gdn_conv, state_ffn_conv, (cache_k, cache_v, page_table), params,
                    tm=x_sample.shape[0] * x_sample.shape[1])
    return (outs_p[0], outs_s[0]) + tuple(outs_p[1:]) + tuple(outs_s[1:])


def _rope_tables(pos):
    half = ROPE_DIMS // 2
    inv = ROPE_THETA ** (-jnp.arange(half, dtype=F32) / half)
    ang = pos.astype(F32)[:, None] * inv[None, :]
    cos, sin = jnp.cos(ang), jnp.sin(ang)
    t = pos.shape[0]
    cosf = jnp.concatenate([cos, cos, jnp.ones((t, HD_B - ROPE_DIMS), F32)], axis=-1)
    sina = jnp.concatenate([-sin, jnp.zeros((t, HD_B - half), F32)], axis=-1)
    sinb = jnp.concatenate([jnp.zeros((t, half), F32), sin, jnp.zeros((t, HD_B - ROPE_DIMS), F32)], axis=-1)
    return cosf, sina, sinb


def _rope_head(x, cosf, sina, sinb):
    half = ROPE_DIMS // 2
    return x * cosf + pltpu.roll(x, HD_B - half, 1) * sina + pltpu.roll(x, half, 1) * sinb


def _moba_prep_kernel(q_ref, k_ref, v_ref, cos_ref, sina_ref, sinb_ref,
                      qr_ref, kr_ref, vr_ref, kb_ref, vb_ref, km_ref):
    cosf, sina, sinb = cos_ref[...], sina_ref[...], sinb_ref[...]
    for h in range(H_B):
        sl = slice(h * HD_B, (h + 1) * HD_B)
        qr_ref[:, sl] = _rope_head(q_ref[:, sl], cosf, sina, sinb)
        kr = _rope_head(k_ref[:, sl], cosf, sina, sinb)
        kr_ref[:, sl] = kr
        kb_ref[:, sl] = kr.astype(BF16)
        km_ref[:, sl] = jnp.broadcast_to(jnp.mean(kr, axis=0, keepdims=True), (8, HD_B))
    v = v_ref[...]
    vr_ref[...] = v
    vb_ref[...] = v.astype(BF16)


def _moba_prep(main, seq_len):
    m = main.shape[0]
    hw = H_B * HD_B
    nblk = seq_len // MOBA_BLOCK
    cosf, sina, sinb = _rope_tables(jnp.arange(seq_len, dtype=jnp.int32))
    c0 = A_COLS // hw
    row = pl.BlockSpec((MOBA_BLOCK, hw), lambda i: (i, 0))
    tab = pl.BlockSpec((MOBA_BLOCK, HD_B), lambda i: (i % nblk, 0))
    return pl.pallas_call(
        _moba_prep_kernel,
        grid=(m // MOBA_BLOCK,),
        in_specs=[pl.BlockSpec((MOBA_BLOCK, hw), lambda i: (i, c0)),
                  pl.BlockSpec((MOBA_BLOCK, hw), lambda i: (i, c0 + 1)),
                  pl.BlockSpec((MOBA_BLOCK, hw), lambda i: (i, c0 + 2)),
                  tab, tab, tab],
        out_specs=[row, row, row, row, row, pl.BlockSpec((None, 8, hw), lambda i: (i, 0, 0))],
        out_shape=[jax.ShapeDtypeStruct((m, hw), F32), jax.ShapeDtypeStruct((m, hw), F32),
                   jax.ShapeDtypeStruct((m, hw), F32), jax.ShapeDtypeStruct((m, hw), BF16),
                   jax.ShapeDtypeStruct((m, hw), BF16), jax.ShapeDtypeStruct((m // MOBA_BLOCK, 8, hw), F32)],
        compiler_params=_params("parallel"),
        name="moba_prep",
    )(main, main, main, cosf, sina, sinb)


def _moba_kernel(q_ref, k_ref, v_ref, km_ref, o_ref, *, nblk):
    qi = pl.program_id(2)
    q = q_ref[...]
    gate = lax.dot_general(q, km_ref[...], (((1,), (1,)), ((), ())),
                           precision=lax.Precision.HIGHEST, preferred_element_type=F32)
    blk = lax.broadcasted_iota(jnp.int32, gate.shape, 1)
    blkf = blk.astype(F32)
    past = blk < qi
    g = jnp.where(past, gate, NEG)
    sel = jnp.zeros(gate.shape, F32)
    for _ in range(min(MOBA_TOPK, nblk)):
        mx = jnp.max(g, axis=1, keepdims=True)
        hit = blkf == jnp.min(jnp.where(g == mx, blkf, float(nblk)), axis=1, keepdims=True)
        sel = jnp.where(hit, 1.0, sel)
        g = jnp.where(hit, -jnp.inf, g)
    sel = jnp.where(past, sel, 0.0)

    qb = q.astype(BF16)
    scale = HD_B ** -0.5

    def blocks(j, n):
        rows = pl.ds(pl.multiple_of(j * MOBA_BLOCK, MOBA_BLOCK), n * MOBA_BLOCK)
        s = lax.dot_general(qb, k_ref[rows, :], (((1,), (1,)), ((), ())), preferred_element_type=F32) * scale
        return s, v_ref[rows, :]

    s, vj = blocks(qi, 1)
    r = lax.broadcasted_iota(jnp.int32, s.shape, 0)
    c = lax.broadcasted_iota(jnp.int32, s.shape, 1)
    s = jnp.where(c <= r, s, NEG)
    m0 = jnp.max(s, axis=1, keepdims=True)
    p = jnp.exp(s - m0)
    l0 = jnp.sum(p, axis=1, keepdims=True)
    acc0 = _bdot(p.astype(BF16), vj)
    first = lax.broadcasted_iota(jnp.int32, (MOBA_BLOCK, 2 * MOBA_BLOCK), 1) < MOBA_BLOCK

    def body(jj, carry):
        m, l, acc = carry
        sel0 = jnp.sum(jnp.where(blk == 2 * jj, sel, 0.0), axis=1, keepdims=True)
        sel1 = jnp.sum(jnp.where(blk == 2 * jj + 1, sel, 0.0), axis=1, keepdims=True)
        s, vj = blocks(2 * jj, 2)
        s = jnp.where(jnp.where(first, sel0, sel1) > 0.5, s, NEG)
        m_new = jnp.maximum(m, jnp.max(s, axis=1, keepdims=True))
        alpha = jnp.exp(m - m_new)
        p = jnp.exp(s - m_new)
        return m_new, alpha * l + jnp.sum(p, axis=1, keepdims=True), alpha * acc + _bdot(p.astype(BF16), vj)

    m, l, acc = lax.fori_loop(0, (qi + 1) // 2, body, (m0, l0, acc0))
    o_ref[...] = (acc / l).astype(BF16)


def _moba_attention_prompt(q_rot, k_bf, v_bf, kmean, n_seq, seq_len):
    m = q_rot.shape[0]
    nblk = seq_len // MOBA_BLOCK
    return pl.pallas_call(
        functools.partial(_moba_kernel, nblk=nblk),
        grid=(n_seq, H_B, nblk),
        in_specs=[pl.BlockSpec((MOBA_BLOCK, HD_B), lambda b, h, i: (b * nblk + i, h)),
                  pl.BlockSpec((seq_len, HD_B), lambda b, h, i: (b, h)),
                  pl.BlockSpec((seq_len, HD_B), lambda b, h, i: (b, h)),
                  pl.BlockSpec((None, nblk, HD_B), lambda b, h, i: (b, 0, h))],
        out_specs=pl.BlockSpec((MOBA_BLOCK, HD_B), lambda b, h, i: (b * nblk + i, h)),
        out_shape=jax.ShapeDtypeStruct((m, H_B * HD_B), BF16),
        compiler_params=_params("parallel", "parallel", "arbitrary"),
        name="moba_attention",
    )(q_rot, k_bf, v_bf, kmean)


MLSTM_ROWS = 128
GATE_LANES = 128


def _mlstm_kernel(q_ref, k_ref, v_ref, og_ref, g_ref, bias_ref, wmh_ref, h_ref, c_ref, n_ref, m_ref,
                  cs_ref, ns_ref, ms_ref):
    r = pl.program_id(1)
    L = q_ref.shape[0]

    @pl.when(r == 0)
    def _():
        cs_ref[...] = jnp.zeros(cs_ref.shape, F32)
        ns_ref[...] = jnp.zeros(ns_ref.shape, F32)
        ms_ref[...] = jnp.zeros(ms_ref.shape, F32)

    gts = g_ref[...] + bias_ref[...]
    lane = lax.broadcasted_iota(jnp.int32, gts.shape, 1)
    lg = jnp.where(lane < H_A, gts, -_softplus(-gts))
    ti = lax.broadcasted_iota(jnp.int32, (L, L), 0)
    si = lax.broadcasted_iota(jnp.int32, (L, L), 1)
    incl = si <= ti
    bmat = jnp.dot(jnp.where(incl, 1.0, 0.0), lg, precision=lax.Precision.HIGHEST, preferred_element_type=F32)
    lgt, bt = lg.T, bmat.T
    nt = (((1,), (1,)), ((), ()))
    tn = (((0,), (0,)), ((), ()))

    for h in range(H_A):
        bcol, brow = bmat[:, H_A + h:H_A + h + 1], bt[H_A + h:H_A + h + 1, :]
        licol, lirow = lg[:, h:h + 1], lgt[h:h + 1, :]
        m_prev = ms_ref[h:h + 1, 0:1]
        d = jnp.where(incl, bcol - brow + lirow, NEG)
        inter = bcol + m_prev
        mt = jnp.maximum(inter, jnp.max(d, axis=1, keepdims=True))
        qh = q_ref[:, h * DK_A:(h + 1) * DK_A]
        kh = k_ref[:, h * DK_A:(h + 1) * DK_A] * (DK_A ** -0.5)
        vb = v_ref[:, h * DV_A:(h + 1) * DV_A].astype(BF16)
        qb = qh.astype(BF16)
        s = lax.dot_general(qb, kh.astype(BF16), nt, preferred_element_type=F32) * jnp.exp(d - mt)
        wi = jnp.exp(inter - mt)
        c_old = cs_ref[h]
        n_old = ns_ref[h:h + 1, :]
        num = wi * _bdot(qb, c_old.astype(BF16)) + _bdot(s.astype(BF16), vb)
        den = wi * jnp.sum(qh * n_old, axis=1, keepdims=True) + jnp.sum(s, axis=1, keepdims=True)
        hh = num / jnp.maximum(jnp.abs(den), jnp.exp(-mt))
        hs = slice(h * DV_A, (h + 1) * DV_A)
        og = og_ref[:, hs]
        h_ref[:, hs] = (_rms(hh, wmh_ref[:, hs]) / (1.0 + jnp.exp(-og))).astype(BF16)
        b_last = bcol[L - 1:L]
        gcol = b_last - bcol + licol
        m_new = jnp.maximum(b_last + m_prev, jnp.max(gcol, axis=0, keepdims=True))
        wc = jnp.exp(b_last + m_prev - m_new)
        kws = kh * jnp.exp(gcol - m_new)
        cs_ref[h] = wc * c_old + lax.dot_general(kws.astype(BF16), vb, tn, preferred_element_type=F32)
        ns_ref[h:h + 1, :] = wc * n_old + jnp.sum(kws, axis=0, keepdims=True)
        ms_ref[h:h + 1, :] = jnp.broadcast_to(m_new, (1, ms_ref.shape[1]))

    @pl.when(r == pl.num_programs(1) - 1)
    def _():
        c_ref[...] = cs_ref[...]
        n_ref[...] = ns_ref[...]
        m_ref[...] = ms_ref[...]


def _mlstm_prompt(main, gates, b_i, b_f, w_mh, n_seq, seq_len):
    m = main.shape[0]
    rows = MLSTM_ROWS
    nr = seq_len // rows
    qkw, vw = H_A * DK_A, H_A * DV_A
    bias = jnp.concatenate([b_i, b_f, jnp.zeros((GATE_LANES - 2 * H_A,), F32)]).reshape(1, GATE_LANES)
    return pl.pallas_call(
        _mlstm_kernel,
        grid=(n_seq, nr),
        in_specs=[pl.BlockSpec((rows, qkw), lambda b, r: (b * nr + r, 0)),
                  pl.BlockSpec((rows, qkw), lambda b, r: (b * nr + r, 1)),
                  pl.BlockSpec((rows, vw), lambda b, r: (b * nr + r, 1)),
                  pl.BlockSpec((rows, vw), lambda b, r: (b * nr + r, 2)),
                  pl.BlockSpec((rows, GATE_LANES), lambda b, r: (b * nr + r, 0)),
                  pl.BlockSpec((1, GATE_LANES), lambda b, r: (0, 0)),
                  pl.BlockSpec((1, vw), lambda b, r: (0, 0))],
        out_specs=[pl.BlockSpec((rows, vw), lambda b, r: (b * nr + r, 0)),
                   pl.BlockSpec((None, H_A, DK_A, DV_A), lambda b, r: (b, 0, 0, 0)),
                   pl.BlockSpec((None, H_A, DK_A), lambda b, r: (b, 0, 0)),
                   pl.BlockSpec((None, H_A, GATE_LANES), lambda b, r: (b, 0, 0))],
        out_shape=[jax.ShapeDtypeStruct((m, vw), BF16),
                   jax.ShapeDtypeStruct((n_seq, H_A, DK_A, DV_A), F32),
                   jax.ShapeDtypeStruct((n_seq, H_A, DK_A), F32),
                   jax.ShapeDtypeStruct((n_seq, H_A, GATE_LANES), F32)],
        scratch_shapes=[pltpu.VMEM((H_A, DK_A, DV_A), F32), pltpu.VMEM((H_A, DK_A), F32),
                        pltpu.VMEM((H_A, GATE_LANES), F32)],
        compiler_params=_params("parallel", "arbitrary"),
        name="mlstm_prompt",
    )(main, main, main, main, gates, bias, w_mh.reshape(1, vw))


GDN_ROWS = 256
GDN_REP = H_V_C // H_QK_C
GDN_QK = 4
GDN_HEADS = GDN_QK * GDN_REP
CONV_HALO = 8


def _silu(x):
    return x / (1.0 + jnp.exp(-x))


def _softplus(x):
    return jnp.maximum(x, 0.0) + jnp.log1p(jnp.exp(-jnp.abs(x)))


def _conv_silu(halo, x, w):
    ext = jnp.concatenate([halo, x], axis=0)
    y = pltpu.roll(ext, CONV_C - 1, 0) * w[0:1]
    for i in range(1, CONV_C - 1):
        y = y + pltpu.roll(ext, CONV_C - 1 - i, 0) * w[i:i + 1]
    y = y + ext * w[CONV_C - 1:CONV_C]
    return _silu(y[halo.shape[0]:])


def _gdn_kernel(qh_ref, kh_ref, vh_ref, q_ref, k_ref, v_ref, z_ref, t_ref, cwq_ref, cwk_ref, cwv_ref,
                al_ref, dt_ref, wn_ref, o_ref, s_ref, st_ref):
    j = pl.program_id(1)
    r = pl.program_id(2)
    L = CHUNK
    rows = q_ref.shape[0]

    @pl.when(r == 0)
    def _():
        st_ref[...] = jnp.zeros(st_ref.shape, F32)

    fresh = r == 0

    def halo(ref):
        return jnp.where(fresh, 0.0, ref[...])

    qs = _conv_silu(halo(qh_ref), q_ref[...], cwq_ref[...])
    ks = _conv_silu(halo(kh_ref), k_ref[...], cwk_ref[...])
    vs = _conv_silu(halo(vh_ref), v_ref[...], cwv_ref[...])
    q, k = [], []
    for a in range(GDN_QK):
        qa, ka = qs[:, a * DK_C:(a + 1) * DK_C], ks[:, a * DK_C:(a + 1) * DK_C]
        q.append(qa * lax.rsqrt(jnp.sum(qa * qa, axis=-1, keepdims=True) + EPS) * (DK_C ** -0.5))
        k.append(ka * lax.rsqrt(jnp.sum(ka * ka, axis=-1, keepdims=True) + EPS))

    t = t_ref[...]
    lane = lax.broadcasted_iota(jnp.int32, t.shape, 1)
    y = jnp.where(lane < H_V_C, 1.0 / (1.0 + jnp.exp(-t)), -jnp.exp(al_ref[...]) * _softplus(t + dt_ref[...]))
    lane128 = lax.broadcasted_iota(jnp.int32, (rows, 128), 1)
    cols = jnp.zeros((rows, 128), F32)
    betas = []
    for e in range(GDN_HEADS):
        hv = j * GDN_HEADS + e
        betas.append(jnp.sum(jnp.where(lane == hv, y, 0.0), axis=1, keepdims=True))
        g_e = jnp.sum(jnp.where(lane == H_V_C + hv, y, 0.0), axis=1, keepdims=True)
        hi = g_e.astype(BF16).astype(F32)
        mid = (g_e - hi).astype(BF16).astype(F32)
        for piece, val in enumerate((hi, mid, g_e - hi - mid)):
            cols = jnp.where(lane128 == piece * GDN_HEADS + e, val, cols)
    ri = lax.broadcasted_iota(jnp.int32, (rows, rows), 0)
    ci = lax.broadcasted_iota(jnp.int32, (rows, rows), 1)
    cum = jnp.where((ri // L == ci // L) & (ci <= ri), 1.0, 0.0).astype(BF16)
    pieces = _bdot(cum, cols.astype(BF16))
    dcols = pieces + pltpu.roll(pieces, 128 - GDN_HEADS, 1) + pltpu.roll(pieces, 128 - 2 * GDN_HEADS, 1)
    drows = dcols.T

    ti = lax.broadcasted_iota(jnp.int32, (L, L), 0)
    si = lax.broadcasted_iota(jnp.int32, (L, L), 1)
    strict, incl = si < ti, si <= ti
    eye = jnp.where(si == ti, 1.0, 0.0)
    wn = wn_ref[...]
    nt = (((1,), (1,)), ((), ()))
    tn = (((0,), (0,)), ((), ()))
    n_chunks = rows // L
    assert L == 64
    units = [(c, e) for c in range(n_chunks) for e in range(GDN_HEADS)]

    def mm(a, b):
        return _bdot(a.astype(BF16), b.astype(BF16))

    kk, qk = {}, {}
    for c in range(n_chunks):
        for a in range(GDN_QK):
            kcb = k[a][c * L:(c + 1) * L].astype(BF16)
            kk[c, a] = lax.dot_general(kcb, kcb, nt, preferred_element_type=F32)
            qk[c, a] = lax.dot_general(q[a][c * L:(c + 1) * L].astype(BF16), kcb, nt, preferred_element_type=F32)
    dcol, dlast, bcol, attn, p1 = {}, {}, {}, {}, {}
    for c, e in units:
        cs = slice(c * L, (c + 1) * L)
        dcol[c, e] = dcols[cs, e:e + 1]
        dlast[c, e] = dcols[(c + 1) * L - 1:(c + 1) * L, e:e + 1]
        bcol[c, e] = betas[e][cs]
        ex = jnp.exp(jnp.where(incl, dcol[c, e] - drows[e:e + 1, cs], 0.0))
        p1[c, e] = jnp.where(strict, -(bcol[c, e] * kk[c, e // GDN_REP] * ex), 0.0)
        attn[c, e] = jnp.where(incl, qk[c, e // GDN_REP] * ex, 0.0).astype(BF16)
    p2 = {u_: mm(p1[u_], p1[u_]) for u_ in units}
    p4 = {u_: mm(p2[u_], p2[u_]) for u_ in units}
    a0 = {u_: eye + p1[u_] + p2[u_] + mm(p1[u_], p2[u_]) for u_ in units}
    p8 = {u_: mm(p4[u_], p4[u_]) for u_ in units}
    p16 = {u_: mm(p8[u_], p8[u_]) for u_ in units}
    a1 = {u_: eye + p4[u_] + p8[u_] + mm(p4[u_], p8[u_]) for u_ in units}
    p32 = {u_: mm(p16[u_], p16[u_]) for u_ in units}
    a01 = {u_: mm(a0[u_], a1[u_]) for u_ in units}
    a2 = {u_: eye + p16[u_] + p32[u_] + mm(p16[u_], p32[u_]) for u_ in units}
    tinv = {u_: mm(a01[u_], a2[u_]) for u_ in units}
    sol, qd = {}, {}
    for c, e in units:
        cs = slice(c * L, (c + 1) * L)
        edc = jnp.exp(dcol[c, e])
        kc = k[e // GDN_REP][cs]
        rhs = jnp.concatenate([vs[cs, e * DV_C:(e + 1) * DV_C] * bcol[c, e], kc * (bcol[c, e] * edc)], axis=1)
        sol[c, e] = mm(tinv[c, e], rhs)
        qd[c, e] = q[e // GDN_REP][cs] * edc
    o_const, q_eff, s_mat, s_add = {}, {}, {}, {}
    for c, e in units:
        solb = sol[c, e].astype(BF16)
        au = _bdot(attn[c, e], solb)
        o_const[c, e] = au[:, :DV_C]
        q_eff[c, e] = (qd[c, e] - au[:, DV_C:]).astype(BF16)
        kd = (k[e // GDN_REP][c * L:(c + 1) * L] * jnp.exp(dlast[c, e] - dcol[c, e])).astype(BF16)
        ksol = lax.dot_general(kd, solb, tn, preferred_element_type=F32)
        s_add[c, e] = ksol[:, :DV_C]
        s_mat[c, e] = ksol[:, DV_C:].astype(BF16)

    s = [st_ref[e] for e in range(GDN_HEADS)]
    for c in range(n_chunks):
        for e in range(GDN_HEADS):
            sb = s[e].astype(BF16)
            o = _bdot(q_eff[c, e], sb) + o_const[c, e]
            s[e] = jnp.exp(dlast[c, e]) * s[e] - _bdot(s_mat[c, e], sb) + s_add[c, e]
            zc = z_ref[c * L:(c + 1) * L, e * DV_C:(e + 1) * DV_C]
            o_ref[c * L:(c + 1) * L, e * DV_C:(e + 1) * DV_C] = (_rms(o, wn) * _silu(zc)).astype(BF16)
    for e in range(GDN_HEADS):
        st_ref[e] = s[e]

    @pl.when(r == pl.num_programs(2) - 1)
    def _():
        s_ref[...] = st_ref[...]


def _gdn_prompt(main, tail, conv_w, a_log, dt_bias, w_norm, n_seq, seq_len):
    m = main.shape[0]
    rows = GDN_ROWS
    nr = seq_len // rows
    hb = rows // CONV_HALO
    qkw, vw = GDN_QK * DK_C, GDN_HEADS * DV_C
    k0 = H_QK_C * DK_C // qkw
    v0 = 2 * H_QK_C * DK_C // vw
    z0 = CONV_DIM_C // vw
    zeros = jnp.zeros((H_V_C,), F32)
    al = jnp.concatenate([zeros, a_log]).reshape(1, 2 * H_V_C)
    dt = jnp.concatenate([zeros, dt_bias]).reshape(1, 2 * H_V_C)

    def row_idx(b, r):
        return b * nr + r

    def halo_idx(b, r):
        return jnp.maximum(row_idx(b, r) * hb - 1, 0)

    return pl.pallas_call(
        _gdn_kernel,
        grid=(n_seq, H_QK_C // GDN_QK, nr),
        in_specs=[pl.BlockSpec((CONV_HALO, qkw), lambda b, j, r: (halo_idx(b, r), j)),
                  pl.BlockSpec((CONV_HALO, qkw), lambda b, j, r: (halo_idx(b, r), k0 + j)),
                  pl.BlockSpec((CONV_HALO, vw), lambda b, j, r: (halo_idx(b, r), v0 + j)),
                  pl.BlockSpec((rows, qkw), lambda b, j, r: (row_idx(b, r), j)),
                  pl.BlockSpec((rows, qkw), lambda b, j, r: (row_idx(b, r), k0 + j)),
                  pl.BlockSpec((rows, vw), lambda b, j, r: (row_idx(b, r), v0 + j)),
                  pl.BlockSpec((rows, vw), lambda b, j, r: (row_idx(b, r), z0 + j)),
                  pl.BlockSpec((rows, 2 * H_V_C), lambda b, j, r: (row_idx(b, r), 0)),
                  pl.BlockSpec((CONV_C, qkw), lambda b, j, r: (0, j)),
                  pl.BlockSpec((CONV_C, qkw), lambda b, j, r: (0, k0 + j)),
                  pl.BlockSpec((CONV_C, vw), lambda b, j, r: (0, v0 + j)),
                  pl.BlockSpec((1, 2 * H_V_C), lambda b, j, r: (0, 0)),
                  pl.BlockSpec((1, 2 * H_V_C), lambda b, j, r: (0, 0)),
                  pl.BlockSpec((1, DV_C), lambda b, j, r: (0, 0))],
        out_specs=[pl.BlockSpec((rows, vw), lambda b, j, r: (row_idx(b, r), j)),
                   pl.BlockSpec((None, GDN_HEADS, DK_C, DV_C), lambda b, j, r: (b, j, 0, 0))],
        out_shape=[jax.ShapeDtypeStruct((m, H_V_C * DV_C), BF16),
                   jax.ShapeDtypeStruct((n_seq, H_V_C, DK_C, DV_C), F32)],
        scratch_shapes=[pltpu.VMEM((GDN_HEADS, DK_C, DV_C), F32)],
        compiler_params=_params("parallel", "parallel", "arbitrary"),
        name="gdn_prompt",
    )(main, main, main, main, main, main, main, tail, conv_w, conv_w, conv_w, al, dt, w_norm.reshape(1, DV_C))


def _to_col(row):
    n = row.shape[1]
    i = lax.broadcasted_iota(jnp.int32, (n, n), 0)
    j = lax.broadcasted_iota(jnp.int32, (n, n), 1)
    return jnp.sum(jnp.where(i == j, jnp.broadcast_to(row, (n, n)), 0.0), axis=1, keepdims=True)


def _sigmoid(x):
    return 1.0 / (1.0 + jnp.exp(-x))


def _mlstm_step_kernel(q_ref, k_ref, v_ref, og_ref, g_ref, bias_ref, wmh_ref, c0_ref, n0_ref, m0_ref,
                       h_ref, c_ref, n_ref, m_ref):
    nb = q_ref.shape[0]
    gts = g_ref[...] + bias_ref[...]
    lane = lax.broadcasted_iota(jnp.int32, gts.shape, 1)
    m_out = jnp.zeros(gts.shape, F32)
    for h in range(H_A):
        li = gts[:, h:h + 1]
        lf = -_softplus(-gts[:, H_A + h:H_A + h + 1])
        m_prev = m0_ref[:, h:h + 1]
        inter = lf + m_prev
        mt = jnp.maximum(inter, li)
        qh = q_ref[:, h * DK_A:(h + 1) * DK_A]
        kh = k_ref[:, h * DK_A:(h + 1) * DK_A] * (DK_A ** -0.5)
        hs = slice(h * DV_A, (h + 1) * DV_A)
        vh = v_ref[:, hs]
        s = jnp.sum(qh * kh, axis=1, keepdims=True) * jnp.exp(li - mt)
        wi = jnp.exp(inter - mt)
        m_new = jnp.maximum(lf + m_prev, li)
        wc = jnp.exp(lf + m_prev - m_new)
        kws = kh * jnp.exp(li - m_new)
        n_old = n0_ref[:, h, :]
        qc_rows = []
        for b in range(nb):
            c_old = c0_ref[b, h]
            qc_rows.append(jnp.sum(_to_col(qh[b:b + 1]) * c_old, axis=0, keepdims=True))
            c_ref[b, h] = wc[b:b + 1] * c_old + _to_col(kws[b:b + 1]) * vh[b:b + 1]
        num = wi * jnp.concatenate(qc_rows, axis=0) + s * vh
        den = wi * jnp.sum(qh * n_old, axis=1, keepdims=True) + s
        hh = num / jnp.maximum(jnp.abs(den), jnp.exp(-mt))
        h_ref[:, hs] = _rms(hh, wmh_ref[:, hs]) * _sigmoid(og_ref[:, hs])
        n_ref[:, h, :] = wc * n_old + kws
        m_out = jnp.where(lane == h, m_new, m_out)
    m_ref[...] = m_out


def _mlstm_step(main, gates, b_i, b_f, w_mh, c0, n0, m0):
    nb = main.shape[0]
    qkw, vw = H_A * DK_A, H_A * DV_A
    bias = jnp.concatenate([b_i, b_f, jnp.zeros((GATE_LANES - 2 * H_A,), F32)]).reshape(1, GATE_LANES)
    full = lambda shape: pl.BlockSpec(shape, lambda i: (0,) * len(shape))
    return pl.pallas_call(
        _mlstm_step_kernel,
        grid=(1,),
        in_specs=[pl.BlockSpec((nb, qkw), lambda i: (0, 0)),
                  pl.BlockSpec((nb, qkw), lambda i: (0, 1)),
                  pl.BlockSpec((nb, vw), lambda i: (0, 1)),
                  pl.BlockSpec((nb, vw), lambda i: (0, 2)),
                  full((nb, GATE_LANES)), full((1, GATE_LANES)), full((1, vw)),
                  full((nb, H_A, DK_A, DV_A)), full((nb, H_A, DK_A)), full((nb, H_A))],
        out_specs=[full((nb, vw)), full((nb, H_A, DK_A, DV_A)), full((nb, H_A, DK_A)), full((nb, GATE_LANES))],
        out_shape=[jax.ShapeDtypeStruct((nb, vw), F32),
                   jax.ShapeDtypeStruct((nb, H_A, DK_A, DV_A), F32),
                   jax.ShapeDtypeStruct((nb, H_A, DK_A), F32),
                   jax.ShapeDtypeStruct((nb, GATE_LANES), F32)],
        compiler_params=_params("arbitrary"),
        name="mlstm_step",
    )(main, main, main, main, gates, bias, w_mh.reshape(1, vw), c0, n0, m0)


def _gdn_step_kernel(x_ref, t_ref, buf_ref, cw_ref, al_ref, dt_ref, wn_ref, s0_ref, o_ref, cn_ref, s_ref):
    x = x_ref[...]
    xc = x[:, :CONV_DIM_C]
    buf = buf_ref[...]
    cw = cw_ref[...]
    y = buf[0:1] * cw[0:1]
    for i in range(1, CONV_C - 1):
        y = y + buf[i:i + 1] * cw[i:i + 1]
    y = _silu(y + xc * cw[CONV_C - 1:CONV_C])
    cn_ref[...] = jnp.concatenate([buf[1:], xc], axis=0)
    t = t_ref[...]
    lane = lax.broadcasted_iota(jnp.int32, t.shape, 1)
    gts = jnp.where(lane < H_V_C, _sigmoid(t), -jnp.exp(al_ref[...]) * _softplus(t + dt_ref[...]))
    wn = wn_ref[...]
    k0, v0 = H_QK_C * DK_C, 2 * H_QK_C * DK_C
    for j in range(H_QK_C):
        qj = y[:, j * DK_C:(j + 1) * DK_C]
        kj = y[:, k0 + j * DK_C:k0 + (j + 1) * DK_C]
        qj = qj * lax.rsqrt(jnp.sum(qj * qj, axis=-1, keepdims=True) + EPS) * (DK_C ** -0.5)
        kj = kj * lax.rsqrt(jnp.sum(kj * kj, axis=-1, keepdims=True) + EPS)
        qcol, kcol = _to_col(qj), _to_col(kj)
        qk = jnp.sum(qj * kj, axis=1, keepdims=True)
        for e in range(GDN_REP):
            hv = j * GDN_REP + e
            hs = slice(hv * DV_C, (hv + 1) * DV_C)
            beta = gts[:, hv:hv + 1]
            eg = jnp.exp(gts[:, H_V_C + hv:H_V_C + hv + 1])
            s_old = s0_ref[hv]
            ks = jnp.sum(kcol * s_old, axis=0, keepdims=True)
            qs = jnp.sum(qcol * s_old, axis=0, keepdims=True)
            v_new = y[:, v0 + hv * DV_C:v0 + (hv + 1) * DV_C] * beta - (beta * eg) * ks
            o = eg * qs + qk * v_new
            s_ref[hv] = eg * s_old + kcol * v_new
            o_ref[:, hs] = _rms(o, wn) * _silu(x[:, CONV_DIM_C + hv * DV_C:CONV_DIM_C + (hv + 1) * DV_C])


def _gdn_step(main, tail, conv_buf, conv_w, a_log, dt_bias, w_norm, s0):
    nb, n_main = main.shape
    zeros = jnp.zeros((H_V_C,), F32)
    al = jnp.concatenate([zeros, a_log]).reshape(1, 2 * H_V_C)
    dt = jnp.concatenate([zeros, dt_bias]).reshape(1, 2 * H_V_C)
    hw = H_V_C * DV_C
    const = lambda shape: pl.BlockSpec(shape, lambda b: (0,) * len(shape))
    o, conv_new, s = pl.pallas_call(
        _gdn_step_kernel,
        grid=(nb,),
        in_specs=[pl.BlockSpec((None, 1, n_main), lambda b: (b, 0, 0)),
                  pl.BlockSpec((None, 1, 2 * H_V_C), lambda b: (b, 0, 0)),
                  pl.BlockSpec((None, CONV_C - 1, CONV_DIM_C), lambda b: (b, 0, 0)),
                  const((CONV_C, CONV_DIM_C)), const((1, 2 * H_V_C)), const((1, 2 * H_V_C)), const((1, DV_C)),
                  pl.BlockSpec((None, H_V_C, DK_C, DV_C), lambda b: (b, 0, 0, 0))],
        out_specs=[pl.BlockSpec((None, 1, hw), lambda b: (b, 0, 0)),
                   pl.BlockSpec((None, CONV_C - 1, CONV_DIM_C), lambda b: (b, 0, 0)),
                   pl.BlockSpec((None, H_V_C, DK_C, DV_C), lambda b: (b, 0, 0, 0))],
        out_shape=[jax.ShapeDtypeStruct((nb, 1, hw), F32),
                   jax.ShapeDtypeStruct((nb, CONV_C - 1, CONV_DIM_C), F32),
                   jax.ShapeDtypeStruct((nb, H_V_C, DK_C, DV_C), F32)],
        compiler_params=_params("parallel"),
        name="gdn_step",
    )(main.reshape(nb, 1, n_main), tail.reshape(nb, 1, 2 * H_V_C), conv_buf, conv_w, al, dt,
      w_norm.reshape(1, DV_C), s0)
    return o.reshape(nb, hw), s, conv_new


SCAN_PAGES = 8
PAGES_PER_BLOCK = MOBA_BLOCK // PAGE_SIZE


def _rope_rows_kernel(q_ref, k_ref, v_ref, cos_ref, sina_ref, sinb_ref, qr_ref, kr_ref, vr_ref):
    cosf, sina, sinb = cos_ref[...], sina_ref[...], sinb_ref[...]
    for h in range(H_B):
        sl = slice(h * HD_B, (h + 1) * HD_B)
        qr_ref[:, sl] = _rope_head(q_ref[:, sl], cosf, sina, sinb)
        kr_ref[:, sl] = _rope_head(k_ref[:, sl], cosf, sina, sinb)
    vr_ref[...] = v_ref[...]


def _rope_rows(main, pos):
    nb = main.shape[0]
    hw = H_B * HD_B
    c0 = A_COLS // hw
    cosf, sina, sinb = _rope_tables(jnp.full((1,), pos, jnp.int32))
    tab = pl.BlockSpec((1, HD_B), lambda i: (0, 0))
    out = pl.BlockSpec((nb, hw), lambda i: (0, 0))
    return pl.pallas_call(
        _rope_rows_kernel,
        grid=(1,),
        in_specs=[pl.BlockSpec((nb, hw), lambda i: (0, c0)),
                  pl.BlockSpec((nb, hw), lambda i: (0, c0 + 1)),
                  pl.BlockSpec((nb, hw), lambda i: (0, c0 + 2)), tab, tab, tab],
        out_specs=[out, out, out],
        out_shape=[jax.ShapeDtypeStruct((nb, hw), F32)] * 3,
        compiler_params=_params("arbitrary"),
        name="rope_rows",
    )(main, main, main, cosf, sina, sinb)


def _moba_scan_kernel(pt_ref, q_ref, *refs):
    pages, sel_ref, g_ref = refs[:SCAN_PAGES], refs[SCAN_PAGES], refs[SCAN_PAGES + 1]
    gi = pl.program_id(1)

    @pl.when(gi == 0)
    def _():
        g_ref[...] = jnp.full(g_ref.shape, NEG, F32)

    q = q_ref[...]
    lane = lax.broadcasted_iota(jnp.int32, g_ref.shape, 1)
    g = g_ref[...]
    for p in range(SCAN_PAGES // PAGES_PER_BLOCK):
        ksum = jnp.sum(pages[PAGES_PER_BLOCK * p][...], axis=0)
        for i in range(1, PAGES_PER_BLOCK):
            ksum = ksum + jnp.sum(pages[PAGES_PER_BLOCK * p + i][...], axis=0)
        val = jnp.sum(q * (ksum * (1.0 / MOBA_BLOCK)), axis=1, keepdims=True)
        g = jnp.where(lane == gi * (SCAN_PAGES // PAGES_PER_BLOCK) + p, val, g)
    g_ref[...] = g

    @pl.when(gi == pl.num_programs(1) - 1)
    def _():
        lanef = lane.astype(F32)
        gg = g
        out = jnp.zeros(g.shape, F32)
        for r in range(MOBA_TOPK):
            mx = jnp.max(gg, axis=1, keepdims=True)
            idx = jnp.min(jnp.where(gg == mx, lanef, float(g.shape[1])), axis=1, keepdims=True)
            out = jnp.where(lane == r, idx, out)
            gg = jnp.where(lanef == idx, -jnp.inf, gg)
        sel_ref[...] = out.astype(jnp.int32)


def _moba_scan(q_rot, cache_k, layer, page_table):
    nb = q_rot.shape[0]
    hw = H_B * HD_B
    n_pages = page_table.shape[1]
    assert n_pages % SCAN_PAGES == 0 and n_pages // PAGES_PER_BLOCK <= 128
    def page_spec(i):
        return pl.BlockSpec((None, None, PAGE_SIZE, H_B, HD_B),
                            lambda b, g, pt: (layer, pt[b * n_pages + g * SCAN_PAGES + i], 0, 0, 0))

    grid_spec = pltpu.PrefetchScalarGridSpec(
        num_scalar_prefetch=1,
        grid=(nb, n_pages // SCAN_PAGES),
        in_specs=[pl.BlockSpec((None, H_B, HD_B), lambda b, g, pt: (b, 0, 0))]
        + [page_spec(i) for i in range(SCAN_PAGES)],
        out_specs=pl.BlockSpec((None, H_B, 128), lambda b, g, pt: (b, 0, 0)),
        scratch_shapes=[pltpu.VMEM((H_B, 128), F32)],
    )
    return pl.pallas_call(
        _moba_scan_kernel,
        grid_spec=grid_spec,
        out_shape=jax.ShapeDtypeStruct((nb, H_B, 128), jnp.int32),
        compiler_params=_params("parallel", "arbitrary"),
        name="moba_scan",
    )(page_table.reshape(-1), q_rot.reshape(nb, H_B, HD_B), *([cache_k] * SCAN_PAGES))


SEL_PAGES = MOBA_TOPK * PAGES_PER_BLOCK


def _moba_decode_kernel(pg_ref, q_ref, kn_ref, vn_ref, *refs):
    ks, vs, o_ref = refs[:SEL_PAGES], refs[SEL_PAGES:2 * SEL_PAGES], refs[2 * SEL_PAGES]
    q = q_ref[...]
    scale = HD_B ** -0.5
    rid = lax.broadcasted_iota(jnp.int32, (PAGE_SIZE * H_B, 1), 0)
    mine = rid % H_B == pl.program_id(1)
    logits = [jnp.where(mine, jnp.sum(k[...] * q, axis=1, keepdims=True) * scale, NEG) for k in ks]
    own = jnp.sum(q * kn_ref[...], axis=1, keepdims=True) * scale
    m = own
    for s in logits:
        m = jnp.maximum(m, jnp.max(s, axis=0, keepdims=True))
    p_own = jnp.exp(own - m)
    l = p_own
    acc = p_own * vn_ref[...]
    for s, v in zip(logits, vs):
        p = jnp.exp(s - m)
        l = l + jnp.sum(p, axis=0, keepdims=True)
        acc = acc + jnp.sum(p * v[...], axis=0, keepdims=True)
    o_ref[...] = acc / l


def _moba_decode(q_rot, k_new, v_new, cache_k, cache_v, layer, pages):
    nb = q_rot.shape[0]
    hw = H_B * HD_B
    n_layers, n_pool = cache_k.shape[:2]
    kpool = cache_k.reshape(n_layers, n_pool, PAGE_SIZE * H_B, HD_B)
    vpool = cache_v.reshape(n_layers, n_pool, PAGE_SIZE * H_B, HD_B)

    def page_spec(i):
        return pl.BlockSpec((None, None, PAGE_SIZE * H_B, HD_B),
                            lambda b, h, pg: (layer, pg[(b * H_B + h) * SEL_PAGES + i], 0, 0))

    row = pl.BlockSpec((None, 1, HD_B), lambda b, h, pg: (b, 0, h))
    grid_spec = pltpu.PrefetchScalarGridSpec(
        num_scalar_prefetch=1,
        grid=(nb, H_B),
        in_specs=[row, row, row] + [page_spec(i) for i in range(SEL_PAGES)] * 2,
        out_specs=row,
    )
    r3 = lambda a: a.reshape(nb, 1, hw)
    out = pl.pallas_call(
        _moba_decode_kernel,
        grid_spec=grid_spec,
        out_shape=jax.ShapeDtypeStruct((nb, 1, hw), F32),
        compiler_params=_params("parallel", "arbitrary"),
        name="moba_decode",
    )(pages, r3(q_rot), r3(k_new), r3(v_new), *([kpool] * SEL_PAGES), *([vpool] * SEL_PAGES))
    return out.reshape(nb, hw)


def _ffn_mid_kernel(u_ref, buf_ref, cw_ref, cb_ref, act_ref, nb_ref):
    u = u_ref[...]
    cw = cw_ref[...]
    y = buf_ref[:, 0, :] * cw[0:1]
    for i in range(1, FFN_CONV - 1):
        y = y + buf_ref[:, i, :] * cw[i:i + 1]
        nb_ref[:, i - 1, :] = buf_ref[:, i, :]
    y = y + u * cw[FFN_CONV - 1:FFN_CONV] + cb_ref[...]
    nb_ref[:, FFN_CONV - 2, :] = u
    act_ref[...] = _silu(y[:, :D_FF]) * y[:, D_FF:]


def _ffn_mid_step(u, buf, conv_w, conv_b):
    nb = u.shape[0]
    full = lambda shape: pl.BlockSpec(shape, lambda i: (0,) * len(shape))
    return pl.pallas_call(
        _ffn_mid_kernel,
        grid=(1,),
        in_specs=[full((nb, 2 * D_FF)), full((nb, FFN_CONV - 1, 2 * D_FF)), full((FFN_CONV, 2 * D_FF)),
                  full((1, 2 * D_FF))],
        out_specs=[full((nb, D_FF)), full((nb, FFN_CONV - 1, 2 * D_FF))],
        out_shape=[jax.ShapeDtypeStruct((nb, D_FF), F32), jax.ShapeDtypeStruct((nb, FFN_CONV - 1, 2 * D_FF), F32)],
        compiler_params=_params("arbitrary"),
        name="ffn_mid_step",
    )(u, buf, conv_w, conv_b.reshape(1, 2 * D_FF))
```

```python
import functools
import math

import jax
import jax.numpy as jnp
import numpy as np
from jax import lax
from jax.experimental import pallas as pl
from jax.experimental.pallas import tpu as pltpu

D_MODEL = 2048
DEPTH = 4
PAGE_SIZE = 128
H_A, DK_A, DV_A = 8, 64, 128
H_B, HD_B = 8, 128
MOBA_BLOCK, MOBA_TOPK, MOBA_Q_CHUNK = 256, 3, 32
ROPE_DIMS = HD_B // 4
ROPE_THETA = 500000.0
H_QK_C, H_V_C, DK_C, DV_C, CONV_C = 16, 32, 128, 128, 4
CONV_DIM_C = 2 * H_QK_C * DK_C + H_V_C * DV_C
D_FF = 5504
FFN_CONV = 3
CHUNK = 64
EPS = 1e-6
NEG = -1e30
A_COLS = 2 * H_A * DK_A + 2 * H_A * DV_A
B_COLS = 3 * H_B * HD_B
GATE_COLS = 2 * H_A

F32 = jnp.float32
BF16 = jnp.bfloat16
VMEM_LIMIT_BYTES = 56 * 1024 * 1024
HALO = 16


def _params(*sem):
    return pltpu.CompilerParams(dimension_semantics=sem, vmem_limit_bytes=VMEM_LIMIT_BYTES)


def _rms(x, w):
    return x * lax.rsqrt(jnp.mean(x * x, axis=-1, keepdims=True) + EPS) * w


def _bdot(a, b):
    return jnp.dot(a, b, preferred_element_type=F32)


def _nmm_kernel(x_ref, nw_ref, *refs, split, transposed):
    wt_ref, o_ref, ot_ref, xn_ref = refs[-4:]
    j = pl.program_id(1)
    dims = (((1,), (1,)), ((), ())) if transposed else (((1,), (0,)), ((), ()))

    def mm(a, w_ref):
        return lax.dot_general(a, w_ref[...].astype(BF16), dims, preferred_element_type=F32)

    @pl.when(j == 0)
    def _():
        xn = _rms(x_ref[...], nw_ref[...]).astype(BF16)
        xn_ref[...] = xn
        ot_ref[...] = mm(xn, wt_ref)

    if split is None:
        o_ref[...] = mm(xn_ref[...], refs[0])
    else:
        @pl.when(j < split)
        def _():
            o_ref[...] = mm(xn_ref[...], refs[0])

        @pl.when(j >= split)
        def _():
            o_ref[...] = mm(xn_ref[...], refs[1])


def _norm_matmul(x, nw, sources, layer, tail, *, tm, tn, transposed=False):
    m, k = x.shape
    tiles = [n for _, n in sources]
    n_main = sum(tiles) * tn
    w_tail, nt, bt = tail

    def wspec(width, index):
        if transposed:
            return pl.BlockSpec((None, width, k), lambda i, j: (layer, index(j), 0))
        return pl.BlockSpec((None, k, width), lambda i, j: (layer, 0, index(j)))

    if len(sources) == 1:
        split = None
        w_specs = [wspec(tn, lambda j: j)]
    else:
        split = tiles[0]
        w_specs = [wspec(tn, lambda j: jnp.minimum(j, split - 1)), wspec(tn, lambda j: jnp.maximum(j - split, 0))]
    return pl.pallas_call(
        functools.partial(_nmm_kernel, split=split, transposed=transposed),
        grid=(m // tm, n_main // tn),
        in_specs=[pl.BlockSpec((tm, k), lambda i, j: (i, 0)),
                  pl.BlockSpec((1, k), lambda i, j: (0, 0))] + w_specs + [wspec(nt, lambda j: bt)],
        out_specs=[pl.BlockSpec((tm, tn), lambda i, j: (i, j)),
                   pl.BlockSpec((tm, nt), lambda i, j: (i, 0))],
        out_shape=[jax.ShapeDtypeStruct((m, n_main), F32), jax.ShapeDtypeStruct((m, nt), F32)],
        scratch_shapes=[pltpu.VMEM((tm, k), BF16)],
        compiler_params=_params("parallel", "arbitrary"),
        name="norm_matmul",
    )(x, nw.reshape(1, k), *[w for w, _ in sources], w_tail)


def _mmr_kernel(a_ref, w_ref, r_ref, o_ref):
    o_ref[...] = r_ref[...] + _bdot(a_ref[...].astype(BF16), w_ref[...].astype(BF16))


def _matmul_res(a, w, layer, res, *, tm, tn):
    m, k = a.shape
    n = w.shape[2]
    return pl.pallas_call(
        _mmr_kernel,
        grid=(m // tm, n // tn),
        in_specs=[pl.BlockSpec((tm, k), lambda i, j: (i, 0)),
                  pl.BlockSpec((None, k, tn), lambda i, j: (layer, 0, j)),
                  pl.BlockSpec((tm, tn), lambda i, j: (i, j))],
        out_specs=pl.BlockSpec((tm, tn), lambda i, j: (i, j)),
        out_shape=jax.ShapeDtypeStruct((m, n), F32),
        compiler_params=_params("parallel", "arbitrary"),
        name="matmul_res",
    )(a, w, res)


def _mm2r_kernel(a1_ref, a2_ref, w1_ref, w2_ref, r_ref, o_ref):
    o_ref[...] = (r_ref[...] + _bdot(a1_ref[...].astype(BF16), w1_ref[...].astype(BF16))
                  + _bdot(a2_ref[...].astype(BF16), w2_ref[...].astype(BF16)))


def _matmul2_res(a1, a2, w, layer, res, *, tm, tn):
    m, k = a1.shape
    n = w.shape[2]
    return pl.pallas_call(
        _mm2r_kernel,
        grid=(m // tm, n // tn),
        in_specs=[pl.BlockSpec((tm, k), lambda i, j: (i, 0)),
                  pl.BlockSpec((tm, k), lambda i, j: (i, 0)),
                  pl.BlockSpec((None, k, tn), lambda i, j: (layer, 0, j)),
                  pl.BlockSpec((None, k, tn), lambda i, j: (layer, 1, j)),
                  pl.BlockSpec((tm, tn), lambda i, j: (i, j))],
        out_specs=pl.BlockSpec((tm, tn), lambda i, j: (i, j)),
        out_shape=jax.ShapeDtypeStruct((m, n), F32),
        compiler_params=_params("parallel", "arbitrary"),
        name="matmul2_res",
    )(a1, a2, w, w, res)


def _final_norm_kernel(x_ref, w_ref, o_ref):
    o_ref[...] = _rms(x_ref[...], w_ref[...])


def _final_norm(x, w, *, tm):
    m, k = x.shape
    return pl.pallas_call(
        _final_norm_kernel,
        grid=(m // tm,),
        in_specs=[pl.BlockSpec((tm, k), lambda i: (i, 0)), pl.BlockSpec((1, k), lambda i: (0, 0))],
        out_specs=pl.BlockSpec((tm, k), lambda i: (i, 0)),
        out_shape=jax.ShapeDtypeStruct((m, k), F32),
        compiler_params=_params("parallel"),
        name="final_norm",
    )(x, w.reshape(1, k))


LANES = 128


def _ffn_up_kernel(xh_ref, x_ref, nw_ref, *refs, tiles_per_seq, n_sub):
    wg_ref, wu_refs = refs[0], refs[1:1 + n_sub]
    cwg_ref, cwu_ref, cbg_ref, cbu_ref, act_ref, lastg_ref, lastu_ref, xn_ref, wb_ref = refs[1 + n_sub:]
    i = pl.program_id(0)
    tn = n_sub * LANES

    @pl.when(pl.program_id(1) == 0)
    def _():
        nw = nw_ref[...]
        xn_ref[HALO:, :] = _rms(x_ref[...], nw).astype(BF16)
        halo = jnp.where(i % tiles_per_seq == 0, 0.0, _rms(xh_ref[...], nw))
        xn_ref[:HALO, :] = halo.astype(BF16)

    wb_ref[:, :tn] = wg_ref[...].astype(BF16)
    for s, w_ref in enumerate(wu_refs):
        wb_ref[:, tn + s * LANES:tn + (s + 1) * LANES] = w_ref[...].astype(BF16)
    v = _bdot(xn_ref[...], wb_ref[...])

    def conv(v, cw_ref, cb_ref, last_ref):
        last_ref[...] = v[v.shape[0] - 8:, :]
        cw = cw_ref[...]
        y = pltpu.roll(v, 2, 0) * cw[0:1] + pltpu.roll(v, 1, 0) * cw[1:2] + v * cw[2:3] + cb_ref[...]
        return y[HALO:, :]

    g = conv(v[:, :tn], cwg_ref, cbg_ref, lastg_ref)
    u = conv(v[:, tn:], cwu_ref, cbu_ref, lastu_ref)
    act_ref[...] = (g / (1.0 + jnp.exp(-g)) * u).astype(BF16)


def _ffn_up(x, nw, w_up, layer, conv_w, conv_b, seq_len, *, tm, tn):
    m, k = x.shape
    cwg, cwu = conv_w[:, :D_FF], conv_w[:, D_FF:]
    cbg, cbu = conv_b[:D_FF].reshape(1, D_FF), conv_b[D_FF:].reshape(1, D_FF)
    n_i, n_j = m // tm, pl.cdiv(D_FF, tn)
    hb = tm // HALO
    n_sub = tn // LANES
    half_blocks = D_FF // LANES

    gate_spec = pl.BlockSpec((None, k, tn), lambda i, j: (layer, 0, j))

    def up_spec(s):
        return pl.BlockSpec((None, k, LANES), lambda i, j: (
            layer, 0, half_blocks + jnp.minimum(j * n_sub + s, half_blocks - 1)))

    cwspec = pl.BlockSpec((FFN_CONV, tn), lambda i, j: (0, j))
    cbspec = pl.BlockSpec((1, tn), lambda i, j: (0, j))
    lastspec = pl.BlockSpec((None, 8, tn), lambda i, j: (i, 0, j))
    act, lastg, lastu = pl.pallas_call(
        functools.partial(_ffn_up_kernel, tiles_per_seq=seq_len // tm, n_sub=n_sub),
        grid=(n_i, n_j),
        in_specs=[pl.BlockSpec((HALO, k), lambda i, j: (jnp.maximum(i * hb - 1, 0), 0)),
                  pl.BlockSpec((tm, k), lambda i, j: (i, 0)),
                  pl.BlockSpec((1, k), lambda i, j: (0, 0))]
        + [gate_spec] + [up_spec(s) for s in range(n_sub)]
        + [cwspec, cwspec, cbspec, cbspec],
        out_specs=[pl.BlockSpec((tm, tn), lambda i, j: (i, j)), lastspec, lastspec],
        out_shape=[jax.ShapeDtypeStruct((m, D_FF), BF16),
                   jax.ShapeDtypeStruct((n_i, 8, D_FF), F32),
                   jax.ShapeDtypeStruct((n_i, 8, D_FF), F32)],
        scratch_shapes=[pltpu.VMEM((HALO + tm, k), BF16), pltpu.VMEM((k, 2 * tn), BF16)],
        compiler_params=_params("parallel", "arbitrary"),
        name="ffn_up",
    )(x, x, nw.reshape(1, k), *([w_up] * (1 + n_sub)), cwg, cwu, cbg, cbu)
    return act, lastg, lastu


def _trunk(x, pos0, c0, n0, m0, s0, gconv0, fconv0, kv_pages, params, *, tm):
    (norm_mix, norm_ffn, norm_final, ab_w_in, ab_b_i, ab_b_f, ab_w_mh, ab_w_out,
     c_w_in, c_conv_w, c_a_log, c_dt_bias, c_w_norm, c_w_out,
     ffn_w_up, ffn_conv_w, ffn_conv_b, ffn_w_down) = params
    ab_w_in_t = jnp.swapaxes(ab_w_in, 1, 2)
    ab_w_moba_t = ab_w_in_t[:, A_COLS + GATE_COLS:, :]
    c_w_in_t = jnp.swapaxes(c_w_in, 1, 2)
    b, t, d = x.shape
    prompt = kv_pages is None
    rows_k, rows_v, cs, ns, ms, ss, gcs, fcs = [], [], [], [], [], [], [], []
    h = x.reshape(b * t, d)
    for l in range(DEPTH):
        j = l // 2
        if l % 2 == 0:
            main, gates = _norm_matmul(h, norm_mix[l], [(ab_w_in_t, A_COLS // 512), (ab_w_moba_t, B_COLS // 512)], j,
                                       (ab_w_in_t, GATE_LANES, A_COLS // GATE_LANES), tm=tm, tn=512, transposed=True)
            if prompt:
                ha, c, n, m = _mlstm_prompt(main, gates, ab_b_i[j], ab_b_f[j], ab_w_mh[j], b, t)
                q_rot, kn, vn, k_bf, v_bf, km = _moba_prep(main, t)
                kmean = km[:, 0, :].reshape(b, t // MOBA_BLOCK, H_B * HD_B)
                ob = _moba_attention_prompt(q_rot, k_bf, v_bf, kmean, b, t)
                m = m[:, :, 0]
                kn = kn.reshape(b, t, H_B, HD_B)
                vn = vn.reshape(b, t, H_B, HD_B)
                h = _matmul2_res(ha, ob, ab_w_out, j, h, tm=tm, tn=512)
            else:
                cache_k, cache_v, page_table = kv_pages
                ha, c, n, m = _mlstm_step(main, gates, ab_b_i[j], ab_b_f[j], ab_w_mh[j], c0[j], n0[j], m0[j])
                m = m[:, :H_A]
                q_rot, kn, vn = _rope_rows(main, pos0)
                sel = _moba_scan(q_rot, cache_k, j, page_table)[:, :, :MOBA_TOPK]
                page_idx = sel[..., None] * PAGES_PER_BLOCK + jnp.arange(PAGES_PER_BLOCK, dtype=jnp.int32)
                pages = jnp.take_along_axis(page_table[:, None, :], page_idx.reshape(b, H_B, SEL_PAGES), axis=2)
                ob = _moba_decode(q_rot, kn, vn, cache_k, cache_v, j, pages.reshape(-1))
                kn = kn.reshape(b, t, H_B, HD_B)
                vn = vn.reshape(b, t, H_B, HD_B)
                h = _matmul2_res(ha, ob, ab_w_out, j, h, tm=tm, tn=512)
            rows_k.append(kn)
            rows_v.append(vn)
            cs.append(c)
            ns.append(n)
            ms.append(m)
        else:
            n_main = CONV_DIM_C + H_V_C * DV_C
            main, tail = _norm_matmul(h, norm_mix[l], [(c_w_in_t, n_main // 512)], j,
                                      (c_w_in_t, 2 * H_V_C, n_main // (2 * H_V_C)), tm=tm, tn=512, transposed=True)
            if prompt:
                o, s = _gdn_prompt(main, tail, c_conv_w[j], c_a_log[j], c_dt_bias[j], c_w_norm[j], b, t)
                gc = main.reshape(b, t, n_main)[:, t - (CONV_C - 1):, :CONV_DIM_C]
            else:
                o, s, gc = _gdn_step(main, tail, gconv0[j], c_conv_w[j], c_a_log[j], c_dt_bias[j], c_w_norm[j], s0[j])
            ss.append(s)
            gcs.append(gc)
            h = _matmul_res(o, c_w_out, j, h, tm=tm, tn=256)
        if prompt:
            act, lastg, lastu = _ffn_up(h, norm_ffn[l], ffn_w_up, l, ffn_conv_w[l], ffn_conv_b[l], t, tm=tm, tn=512)
            tps = t // tm
            last = jnp.concatenate([lastg[tps - 1::tps, 8 - (FFN_CONV - 1):], lastu[tps - 1::tps, 8 - (FFN_CONV - 1):]],
                                   axis=-1)
            fcs.append(last)
        else:
            n_tiles = (2 * D_FF) // 512
            n_tail = 2 * D_FF - n_tiles * 512
            main, tail = _norm_matmul(h, norm_ffn[l], [(ffn_w_up, n_tiles)], l,
                                      (ffn_w_up, n_tail, n_tiles * 512 // n_tail), tm=tm, tn=512)
            act, fc = _ffn_mid_step(jnp.concatenate([main, tail], axis=-1), fconv0[l], ffn_conv_w[l], ffn_conv_b[l])
            fcs.append(fc)
        h = _matmul_res(act, ffn_w_down, l, h, tm=tm, tn=256)
    y = _final_norm(h, norm_final, tm=tm).reshape(b, t, d)
    return (y, jnp.stack(rows_k), jnp.stack(rows_v), jnp.stack(cs), jnp.stack(ns), jnp.stack(ms),
            jnp.stack(ss), jnp.stack(gcs), jnp.stack(fcs))


def kernel(x_prompt, x_sample, cache_k, cache_v, page_table, state_mlstm_c, state_mlstm_n, state_mlstm_m,
           state_gdn_s, state_gdn_conv, state_ffn_conv, norm_mix, norm_ffn, norm_final,
           ab_w_in, ab_b_i, ab_b_f, ab_w_mh, ab_w_out, c_w_in, c_conv_w, c_a_log, c_dt_bias, c_w_norm, c_w_out,
           ffn_w_up, ffn_conv_w, ffn_conv_b, ffn_w_down):
    params = (norm_mix, norm_ffn, norm_final, ab_w_in, ab_b_i, ab_b_f, ab_w_mh, ab_w_out,
              c_w_in, c_conv_w, c_a_log, c_dt_bias, c_w_norm, c_w_out,
              ffn_w_up, ffn_conv_w, ffn_conv_b, ffn_w_down)
    batch = x_prompt.shape[0]
    n_ab, n_c = ab_w_in.shape[0], c_w_in.shape[0]
    zc = jnp.zeros((n_ab, batch, H_A, DK_A, DV_A), F32)
    zn = jnp.zeros((n_ab, batch, H_A, DK_A), F32)
    zm = jnp.zeros((n_ab, batch, H_A), F32)
    zs = jnp.zeros((n_c, batch, H_V_C, DK_C, DV_C), F32)
    zg = jnp.zeros((n_c, batch, CONV_C - 1, CONV_DIM_C), F32)
    outs_p = _trunk(x_prompt, 0, zc, zn, zm, zs, zg, None, None, params, tm=1024)
    past_len = page_table.shape[1] * PAGE_SIZE
    outs_s = _trunk(x_sample, past_len, state_mlstm_c, state_mlstm_n, state_mlstm_m,
                    state_gdn_s, state_gdn_conv, state_ffn_conv, (cache_k, cache_v, page_table), params,
                    tm=x_sample.shape[0] * x_sample.shape[1])
    return (outs_p[0], outs_s[0]) + tuple(outs_p[1:]) + tuple(outs_s[1:])


def _rope_tables(pos):
    half = ROPE_DIMS // 2
    inv = ROPE_THETA ** (-jnp.arange(half, dtype=F32) / half)
    ang = pos.astype(F32)[:, None] * inv[None, :]
    cos, sin = jnp.cos(ang), jnp.sin(ang)
    t = pos.shape[0]
    cosf = jnp.concatenate([cos, cos, jnp.ones((t, HD_B - ROPE_DIMS), F32)], axis=-1)
    sina = jnp.concatenate([-sin, jnp.zeros((t, HD_B - half), F32)], axis=-1)
    sinb = jnp.concatenate([jnp.zeros((t, half), F32), sin, jnp.zeros((t, HD_B - ROPE_DIMS), F32)], axis=-1)
    return cosf, sina, sinb


def _rope_head(x, cosf, sina, sinb):
    half = ROPE_DIMS // 2
    return x * cosf + pltpu.roll(x, HD_B - half, 1) * sina + pltpu.roll(x, half, 1) * sinb


def _moba_prep_kernel(q_ref, k_ref, v_ref, cos_ref, sina_ref, sinb_ref,
                      qr_ref, kr_ref, vr_ref, kb_ref, vb_ref, km_ref):
    cosf, sina, sinb = cos_ref[...], sina_ref[...], sinb_ref[...]
    for h in range(H_B):
        sl = slice(h * HD_B, (h + 1) * HD_B)
        qr_ref[:, sl] = _rope_head(q_ref[:, sl], cosf, sina, sinb)
        kr = _rope_head(k_ref[:, sl], cosf, sina, sinb)
        kr_ref[:, sl] = kr
        kb_ref[:, sl] = kr.astype(BF16)
        km_ref[:, sl] = jnp.broadcast_to(jnp.mean(kr, axis=0, keepdims=True), (8, HD_B))
    v = v_ref[...]
    vr_ref[...] = v
    vb_ref[...] = v.astype(BF16)


def _moba_prep(main, seq_len):
    m = main.shape[0]
    hw = H_B * HD_B
    nblk = seq_len // MOBA_BLOCK
    cosf, sina, sinb = _rope_tables(jnp.arange(seq_len, dtype=jnp.int32))
    c0 = A_COLS // hw
    row = pl.BlockSpec((MOBA_BLOCK, hw), lambda i: (i, 0))
    tab = pl.BlockSpec((MOBA_BLOCK, HD_B), lambda i: (i % nblk, 0))
    return pl.pallas_call(
        _moba_prep_kernel,
        grid=(m // MOBA_BLOCK,),
        in_specs=[pl.BlockSpec((MOBA_BLOCK, hw), lambda i: (i, c0)),
                  pl.BlockSpec((MOBA_BLOCK, hw), lambda i: (i, c0 + 1)),
                  pl.BlockSpec((MOBA_BLOCK, hw), lambda i: (i, c0 + 2)),
                  tab, tab, tab],
        out_specs=[row, row, row, row, row, pl.BlockSpec((None, 8, hw), lambda i: (i, 0, 0))],
        out_shape=[jax.ShapeDtypeStruct((m, hw), F32), jax.ShapeDtypeStruct((m, hw), F32),
                   jax.ShapeDtypeStruct((m, hw), F32), jax.ShapeDtypeStruct((m, hw), BF16),
                   jax.ShapeDtypeStruct((m, hw), BF16), jax.ShapeDtypeStruct((m // MOBA_BLOCK, 8, hw), F32)],
        compiler_params=_params("parallel"),
        name="moba_prep",
    )(main, main, main, cosf, sina, sinb)


MOBA_TRIP = 4


def _moba_kernel(q_ref, k_ref, v_ref, km_ref, o_ref, *, nblk):
    assert nblk % MOBA_TRIP == 0
    qi = pl.program_id(2)
    q = q_ref[...]
    gate = lax.dot_general(q, km_ref[...], (((1,), (1,)), ((), ())),
                           precision=lax.Precision.HIGHEST, preferred_element_type=F32)
    blk = lax.broadcasted_iota(jnp.int32, gate.shape, 1)
    blkf = blk.astype(F32)
    past = blk < qi
    g = jnp.where(past, gate, NEG)
    sel = jnp.zeros(gate.shape, F32)
    for _ in range(min(MOBA_TOPK, nblk)):
        mx = jnp.max(g, axis=1, keepdims=True)
        hit = blkf == jnp.min(jnp.where(g == mx, blkf, float(nblk)), axis=1, keepdims=True)
        sel = jnp.where(hit, 1.0, sel)
        g = jnp.where(hit, -jnp.inf, g)
    sel = jnp.where(past, sel, 0.0)

    qb = q.astype(BF16)
    scale = HD_B ** -0.5

    def blocks(j, n):
        rows = pl.ds(pl.multiple_of(j * MOBA_BLOCK, MOBA_BLOCK), n * MOBA_BLOCK)
        s = lax.dot_general(qb, k_ref[rows, :], (((1,), (1,)), ((), ())), preferred_element_type=F32) * scale
        return s, v_ref[rows, :]

    s, vj = blocks(qi, 1)
    r = lax.broadcasted_iota(jnp.int32, s.shape, 0)
    c = lax.broadcasted_iota(jnp.int32, s.shape, 1)
    s = jnp.where(c <= r, s, NEG)
    m0 = jnp.max(s, axis=1, keepdims=True)
    p = jnp.exp(s - m0)
    l0 = jnp.sum(p, axis=1, keepdims=True)
    acc0 = _bdot(p.astype(BF16), vj)
    sub = lax.broadcasted_iota(jnp.int32, (MOBA_BLOCK, MOBA_TRIP * MOBA_BLOCK), 1) // MOBA_BLOCK

    def body(jj, carry):
        m, l, acc = carry
        keep = jnp.zeros(sub.shape, F32)
        for i in range(MOBA_TRIP):
            sel_i = jnp.sum(jnp.where(blk == MOBA_TRIP * jj + i, sel, 0.0), axis=1, keepdims=True)
            keep = jnp.where(sub == i, sel_i, keep)
        s, vj = blocks(MOBA_TRIP * jj, MOBA_TRIP)
        s = jnp.where(keep > 0.5, s, NEG)
        m_new = jnp.maximum(m, jnp.max(s, axis=1, keepdims=True))
        alpha = jnp.exp(m - m_new)
        p = jnp.exp(s - m_new)
        return m_new, alpha * l + jnp.sum(p, axis=1, keepdims=True), alpha * acc + _bdot(p.astype(BF16), vj)

    m, l, acc = lax.fori_loop(0, (qi + MOBA_TRIP - 1) // MOBA_TRIP, body, (m0, l0, acc0))
    o_ref[...] = (acc / l).astype(BF16)


def _moba_attention_prompt(q_rot, k_bf, v_bf, kmean, n_seq, seq_len):
    m = q_rot.shape[0]
    nblk = seq_len // MOBA_BLOCK
    return pl.pallas_call(
        functools.partial(_moba_kernel, nblk=nblk),
        grid=(n_seq, H_B, nblk),
        in_specs=[pl.BlockSpec((MOBA_BLOCK, HD_B), lambda b, h, i: (b * nblk + i, h)),
                  pl.BlockSpec((seq_len, HD_B), lambda b, h, i: (b, h)),
                  pl.BlockSpec((seq_len, HD_B), lambda b, h, i: (b, h)),
                  pl.BlockSpec((None, nblk, HD_B), lambda b, h, i: (b, 0, h))],
        out_specs=pl.BlockSpec((MOBA_BLOCK, HD_B), lambda b, h, i: (b * nblk + i, h)),
        out_shape=jax.ShapeDtypeStruct((m, H_B * HD_B), BF16),
        compiler_params=_params("parallel", "parallel", "arbitrary"),
        name="moba_attention",
    )(q_rot, k_bf, v_bf, kmean)


MLSTM_ROWS = 128
GATE_LANES = 128


def _mlstm_kernel(q_ref, k_ref, v_ref, og_ref, g_ref, bias_ref, wmh_ref, h_ref, c_ref, n_ref, m_ref,
                  cs_ref, ns_ref, ms_ref):
    r = pl.program_id(1)
    L = q_ref.shape[0]

    @pl.when(r == 0)
    def _():
        cs_ref[...] = jnp.zeros(cs_ref.shape, F32)
        ns_ref[...] = jnp.zeros(ns_ref.shape, F32)
        ms_ref[...] = jnp.zeros(ms_ref.shape, F32)

    gts = g_ref[...] + bias_ref[...]
    lane = lax.broadcasted_iota(jnp.int32, gts.shape, 1)
    lg = jnp.where(lane < H_A, gts, -_softplus(-gts))
    ti = lax.broadcasted_iota(jnp.int32, (L, L), 0)
    si = lax.broadcasted_iota(jnp.int32, (L, L), 1)
    incl = si <= ti
    bmat = jnp.dot(jnp.where(incl, 1.0, 0.0), lg, precision=lax.Precision.HIGHEST, preferred_element_type=F32)
    lgt, bt = lg.T, bmat.T
    nt = (((1,), (1,)), ((), ()))
    tn = (((0,), (0,)), ((), ()))

    for h in range(H_A):
        bcol, brow = bmat[:, H_A + h:H_A + h + 1], bt[H_A + h:H_A + h + 1, :]
        licol, lirow = lg[:, h:h + 1], lgt[h:h + 1, :]
        m_prev = ms_ref[h:h + 1, 0:1]
        d = jnp.where(incl, bcol - brow + lirow, NEG)
        inter = bcol + m_prev
        mt = jnp.maximum(inter, jnp.max(d, axis=1, keepdims=True))
        qh = q_ref[:, h * DK_A:(h + 1) * DK_A]
        kh = k_ref[:, h * DK_A:(h + 1) * DK_A] * (DK_A ** -0.5)
        vb = v_ref[:, h * DV_A:(h + 1) * DV_A].astype(BF16)
        qb = qh.astype(BF16)
        s = lax.dot_general(qb, kh.astype(BF16), nt, preferred_element_type=F32) * jnp.exp(d - mt)
        wi = jnp.exp(inter - mt)
        c_old = cs_ref[h]
        n_old = ns_ref[h:h + 1, :]
        num = wi * _bdot(qb, c_old.astype(BF16)) + _bdot(s.astype(BF16), vb)
        den = wi * jnp.sum(qh * n_old, axis=1, keepdims=True) + jnp.sum(s, axis=1, keepdims=True)
        hh = num / jnp.maximum(jnp.abs(den), jnp.exp(-mt))
        hs = slice(h * DV_A, (h + 1) * DV_A)
        og = og_ref[:, hs]
        h_ref[:, hs] = (_rms(hh, wmh_ref[:, hs]) / (1.0 + jnp.exp(-og))).astype(BF16)
        b_last = bcol[L - 1:L]
        gcol = b_last - bcol + licol
        m_new = jnp.maximum(b_last + m_prev, jnp.max(gcol, axis=0, keepdims=True))
        wc = jnp.exp(b_last + m_prev - m_new)
        kws = kh * jnp.exp(gcol - m_new)
        cs_ref[h] = wc * c_old + lax.dot_general(kws.astype(BF16), vb, tn, preferred_element_type=F32)
        ns_ref[h:h + 1, :] = wc * n_old + jnp.sum(kws, axis=0, keepdims=True)
        ms_ref[h:h + 1, :] = jnp.broadcast_to(m_new, (1, ms_ref.shape[1]))

    @pl.when(r == pl.num_programs(1) - 1)
    def _():
        c_ref[...] = cs_ref[...]
        n_ref[...] = ns_ref[...]
        m_ref[...] = ms_ref[...]


def _mlstm_prompt(main, gates, b_i, b_f, w_mh, n_seq, seq_len):
    m = main.shape[0]
    rows = MLSTM_ROWS
    nr = seq_len // rows
    qkw, vw = H_A * DK_A, H_A * DV_A
    bias = jnp.concatenate([b_i, b_f, jnp.zeros((GATE_LANES - 2 * H_A,), F32)]).reshape(1, GATE_LANES)
    return pl.pallas_call(
        _mlstm_kernel,
        grid=(n_seq, nr),
        in_specs=[pl.BlockSpec((rows, qkw), lambda b, r: (b * nr + r, 0)),
                  pl.BlockSpec((rows, qkw), lambda b, r: (b * nr + r, 1)),
                  pl.BlockSpec((rows, vw), lambda b, r: (b * nr + r, 1)),
                  pl.BlockSpec((rows, vw), lambda b, r: (b * nr + r, 2)),
                  pl.BlockSpec((rows, GATE_LANES), lambda b, r: (b * nr + r, 0)),
                  pl.BlockSpec((1, GATE_LANES), lambda b, r: (0, 0)),
                  pl.BlockSpec((1, vw), lambda b, r: (0, 0))],
        out_specs=[pl.BlockSpec((rows, vw), lambda b, r: (b * nr + r, 0)),
                   pl.BlockSpec((None, H_A, DK_A, DV_A), lambda b, r: (b, 0, 0, 0)),
                   pl.BlockSpec((None, H_A, DK_A), lambda b, r: (b, 0, 0)),
                   pl.BlockSpec((None, H_A, GATE_LANES), lambda b, r: (b, 0, 0))],
        out_shape=[jax.ShapeDtypeStruct((m, vw), BF16),
                   jax.ShapeDtypeStruct((n_seq, H_A, DK_A, DV_A), F32),
                   jax.ShapeDtypeStruct((n_seq, H_A, DK_A), F32),
                   jax.ShapeDtypeStruct((n_seq, H_A, GATE_LANES), F32)],
        scratch_shapes=[pltpu.VMEM((H_A, DK_A, DV_A), F32), pltpu.VMEM((H_A, DK_A), F32),
                        pltpu.VMEM((H_A, GATE_LANES), F32)],
        compiler_params=_params("parallel", "arbitrary"),
        name="mlstm_prompt",
    )(main, main, main, main, gates, bias, w_mh.reshape(1, vw))


GDN_ROWS = 256
GDN_REP = H_V_C // H_QK_C
GDN_QK = 4
GDN_HEADS = GDN_QK * GDN_REP
CONV_HALO = 8


def _silu(x):
    return x / (1.0 + jnp.exp(-x))


def _softplus(x):
    return jnp.maximum(x, 0.0) + jnp.log1p(jnp.exp(-jnp.abs(x)))


def _conv_silu(halo, x, w):
    ext = jnp.concatenate([halo, x], axis=0)
    y = pltpu.roll(ext, CONV_C - 1, 0) * w[0:1]
    for i in range(1, CONV_C - 1):
        y = y + pltpu.roll(ext, CONV_C - 1 - i, 0) * w[i:i + 1]
    y = y + ext * w[CONV_C - 1:CONV_C]
    return _silu(y[halo.shape[0]:])


def _gdn_kernel(qh_ref, kh_ref, vh_ref, q_ref, k_ref, v_ref, z_ref, t_ref, cwq_ref, cwk_ref, cwv_ref,
                al_ref, dt_ref, wn_ref, o_ref, s_ref, st_ref):
    j = pl.program_id(1)
    r = pl.program_id(2)
    L = CHUNK
    rows = q_ref.shape[0]

    @pl.when(r == 0)
    def _():
        st_ref[...] = jnp.zeros(st_ref.shape, F32)

    fresh = r == 0

    def halo(ref):
        return jnp.where(fresh, 0.0, ref[...])

    qs = _conv_silu(halo(qh_ref), q_ref[...], cwq_ref[...])
    ks = _conv_silu(halo(kh_ref), k_ref[...], cwk_ref[...])
    vs = _conv_silu(halo(vh_ref), v_ref[...], cwv_ref[...])
    q, k = [], []
    for a in range(GDN_QK):
        qa, ka = qs[:, a * DK_C:(a + 1) * DK_C], ks[:, a * DK_C:(a + 1) * DK_C]
        q.append(qa * lax.rsqrt(jnp.sum(qa * qa, axis=-1, keepdims=True) + EPS) * (DK_C ** -0.5))
        k.append(ka * lax.rsqrt(jnp.sum(ka * ka, axis=-1, keepdims=True) + EPS))

    t = t_ref[...]
    lane = lax.broadcasted_iota(jnp.int32, t.shape, 1)
    y = jnp.where(lane < H_V_C, 1.0 / (1.0 + jnp.exp(-t)), -jnp.exp(al_ref[...]) * _softplus(t + dt_ref[...]))
    lane128 = lax.broadcasted_iota(jnp.int32, (rows, 128), 1)
    cols = jnp.zeros((rows, 128), F32)
    betas = []
    for e in range(GDN_HEADS):
        hv = j * GDN_HEADS + e
        betas.append(jnp.sum(jnp.where(lane == hv, y, 0.0), axis=1, keepdims=True))
        g_e = jnp.sum(jnp.where(lane == H_V_C + hv, y, 0.0), axis=1, keepdims=True)
        hi = g_e.astype(BF16).astype(F32)
        mid = (g_e - hi).astype(BF16).astype(F32)
        for piece, val in enumerate((hi, mid, g_e - hi - mid)):
            cols = jnp.where(lane128 == piece * GDN_HEADS + e, val, cols)
    ri = lax.broadcasted_iota(jnp.int32, (rows, rows), 0)
    ci = lax.broadcasted_iota(jnp.int32, (rows, rows), 1)
    cum = jnp.where((ri // L == ci // L) & (ci <= ri), 1.0, 0.0).astype(BF16)
    pieces = _bdot(cum, cols.astype(BF16))
    dcols = pieces + pltpu.roll(pieces, 128 - GDN_HEADS, 1) + pltpu.roll(pieces, 128 - 2 * GDN_HEADS, 1)
    drows = dcols.T

    ti = lax.broadcasted_iota(jnp.int32, (L, L), 0)
    si = lax.broadcasted_iota(jnp.int32, (L, L), 1)
    strict, incl = si < ti, si <= ti
    eye = jnp.where(si == ti, 1.0, 0.0)
    wn = wn_ref[...]
    nt = (((1,), (1,)), ((), ()))
    tn = (((0,), (0,)), ((), ()))
    n_chunks = rows // L
    assert L == 64
    units = [(c, e) for c in range(n_chunks) for e in range(GDN_HEADS)]

    def mm(a, b):
        return _bdot(a.astype(BF16), b.astype(BF16))

    kk, qk = {}, {}
    for c in range(n_chunks):
        for a in range(GDN_QK):
            kcb = k[a][c * L:(c + 1) * L].astype(BF16)
            kk[c, a] = lax.dot_general(kcb, kcb, nt, preferred_element_type=F32)
            qk[c, a] = lax.dot_general(q[a][c * L:(c + 1) * L].astype(BF16), kcb, nt, preferred_element_type=F32)
    dcol, dlast, bcol, attn, p1 = {}, {}, {}, {}, {}
    for c, e in units:
        cs = slice(c * L, (c + 1) * L)
        dcol[c, e] = dcols[cs, e:e + 1]
        dlast[c, e] = dcols[(c + 1) * L - 1:(c + 1) * L, e:e + 1]
        bcol[c, e] = betas[e][cs]
        ex = jnp.exp(jnp.where(incl, dcol[c, e] - drows[e:e + 1, cs], 0.0))
        p1[c, e] = jnp.where(strict, -(bcol[c, e] * kk[c, e // GDN_REP] * ex), 0.0)
        attn[c, e] = jnp.where(incl, qk[c, e // GDN_REP] * ex, 0.0).astype(BF16)
    p2 = {u_: mm(p1[u_], p1[u_]) for u_ in units}
    p4 = {u_: mm(p2[u_], p2[u_]) for u_ in units}
    a0 = {u_: eye + p1[u_] + p2[u_] + mm(p1[u_], p2[u_]) for u_ in units}
    p8 = {u_: mm(p4[u_], p4[u_]) for u_ in units}
    p16 = {u_: mm(p8[u_], p8[u_]) for u_ in units}
    a1 = {u_: eye + p4[u_] + p8[u_] + mm(p4[u_], p8[u_]) for u_ in units}
    p32 = {u_: mm(p16[u_], p16[u_]) for u_ in units}
    a01 = {u_: mm(a0[u_], a1[u_]) for u_ in units}
    a2 = {u_: eye + p16[u_] + p32[u_] + mm(p16[u_], p32[u_]) for u_ in units}
    tinv = {u_: mm(a01[u_], a2[u_]) for u_ in units}
    sol, qd = {}, {}
    for c, e in units:
        cs = slice(c * L, (c + 1) * L)
        edc = jnp.exp(dcol[c, e])
        kc = k[e // GDN_REP][cs]
        rhs = jnp.concatenate([vs[cs, e * DV_C:(e + 1) * DV_C] * bcol[c, e], kc * (bcol[c, e] * edc)], axis=1)
        sol[c, e] = mm(tinv[c, e], rhs)
        qd[c, e] = q[e // GDN_REP][cs] * edc
    o_const, q_eff, s_mat, s_add = {}, {}, {}, {}
    for c, e in units:
        solb = sol[c, e].astype(BF16)
        au = _bdot(attn[c, e], solb)
        o_const[c, e] = au[:, :DV_C]
        q_eff[c, e] = (qd[c, e] - au[:, DV_C:]).astype(BF16)
        kd = (k[e // GDN_REP][c * L:(c + 1) * L] * jnp.exp(dlast[c, e] - dcol[c, e])).astype(BF16)
        ksol = lax.dot_general(kd, solb, tn, preferred_element_type=F32)
        s_add[c, e] = ksol[:, :DV_C]
        s_mat[c, e] = ksol[:, DV_C:].astype(BF16)

    s = [st_ref[e] for e in range(GDN_HEADS)]
    for c in range(n_chunks):
        for e in range(GDN_HEADS):
            sb = s[e].astype(BF16)
            o = _bdot(q_eff[c, e], sb) + o_const[c, e]
            s[e] = jnp.exp(dlast[c, e]) * s[e] - _bdot(s_mat[c, e], sb) + s_add[c, e]
            zc = z_ref[c * L:(c + 1) * L, e * DV_C:(e + 1) * DV_C]
            o_ref[c * L:(c + 1) * L, e * DV_C:(e + 1) * DV_C] = (_rms(o, wn) * _silu(zc)).astype(BF16)
    for e in range(GDN_HEADS):
        st_ref[e] = s[e]

    @pl.when(r == pl.num_programs(2) - 1)
    def _():
        s_ref[...] = st_ref[...]


def _gdn_prompt(main, tail, conv_w, a_log, dt_bias, w_norm, n_seq, seq_len):
    m = main.shape[0]
    rows = GDN_ROWS
    nr = seq_len // rows
    hb = rows // CONV_HALO
    qkw, vw = GDN_QK * DK_C, GDN_HEADS * DV_C
    k0 = H_QK_C * DK_C // qkw
    v0 = 2 * H_QK_C * DK_C // vw
    z0 = CONV_DIM_C // vw
    zeros = jnp.zeros((H_V_C,), F32)
    al = jnp.concatenate([zeros, a_log]).reshape(1, 2 * H_V_C)
    dt = jnp.concatenate([zeros, dt_bias]).reshape(1, 2 * H_V_C)

    def row_idx(b, r):
        return b * nr + r

    def halo_idx(b, r):
        return jnp.maximum(row_idx(b, r) * hb - 1, 0)

    return pl.pallas_call(
        _gdn_kernel,
        grid=(n_seq, H_QK_C // GDN_QK, nr),
        in_specs=[pl.BlockSpec((CONV_HALO, qkw), lambda b, j, r: (halo_idx(b, r), j)),
                  pl.BlockSpec((CONV_HALO, qkw), lambda b, j, r: (halo_idx(b, r), k0 + j)),
                  pl.BlockSpec((CONV_HALO, vw), lambda b, j, r: (halo_idx(b, r), v0 + j)),
                  pl.BlockSpec((rows, qkw), lambda b, j, r: (row_idx(b, r), j)),
                  pl.BlockSpec((rows, qkw), lambda b, j, r: (row_idx(b, r), k0 + j)),
                  pl.BlockSpec((rows, vw), lambda b, j, r: (row_idx(b, r), v0 + j)),
                  pl.BlockSpec((rows, vw), lambda b, j, r: (row_idx(b, r), z0 + j)),
                  pl.BlockSpec((rows, 2 * H_V_C), lambda b, j, r: (row_idx(b, r), 0)),
                  pl.BlockSpec((CONV_C, qkw), lambda b, j, r: (0, j)),
                  pl.BlockSpec((CONV_C, qkw), lambda b, j, r: (0, k0 + j)),
                  pl.BlockSpec((CONV_C, vw), lambda b, j, r: (0, v0 + j)),
                  pl.BlockSpec((1, 2 * H_V_C), lambda b, j, r: (0, 0)),
                  pl.BlockSpec((1, 2 * H_V_C), lambda b, j, r: (0, 0)),
                  pl.BlockSpec((1, DV_C), lambda b, j, r: (0, 0))],
        out_specs=[pl.BlockSpec((rows, vw), lambda b, j, r: (row_idx(b, r), j)),
                   pl.BlockSpec((None, GDN_HEADS, DK_C, DV_C), lambda b, j, r: (b, j, 0, 0))],
        out_shape=[jax.ShapeDtypeStruct((m, H_V_C * DV_C), BF16),
                   jax.ShapeDtypeStruct((n_seq, H_V_C, DK_C, DV_C), F32)],
        scratch_shapes=[pltpu.VMEM((GDN_HEADS, DK_C, DV_C), F32)],
        compiler_params=_params("parallel", "parallel", "arbitrary"),
        name="gdn_prompt",
    )(main, main, main, main, main, main, main, tail, conv_w, conv_w, conv_w, al, dt, w_norm.reshape(1, DV_C))


def _to_col(row):
    n = row.shape[1]
    i = lax.broadcasted_iota(jnp.int32, (n, n), 0)
    j = lax.broadcasted_iota(jnp.int32, (n, n), 1)
    return jnp.sum(jnp.where(i == j, jnp.broadcast_to(row, (n, n)), 0.0), axis=1, keepdims=True)


def _sigmoid(x):
    return 1.0 / (1.0 + jnp.exp(-x))


def _mlstm_step_kernel(q_ref, k_ref, v_ref, og_ref, g_ref, bias_ref, wmh_ref, c0_ref, n0_ref, m0_ref,
                       h_ref, c_ref, n_ref, m_ref):
    nb = q_ref.shape[0]
    gts = g_ref[...] + bias_ref[...]
    lane = lax.broadcasted_iota(jnp.int32, gts.shape, 1)
    m_out = jnp.zeros(gts.shape, F32)
    for h in range(H_A):
        li = gts[:, h:h + 1]
        lf = -_softplus(-gts[:, H_A + h:H_A + h + 1])
        m_prev = m0_ref[:, h:h + 1]
        inter = lf + m_prev
        mt = jnp.maximum(inter, li)
        qh = q_ref[:, h * DK_A:(h + 1) * DK_A]
        kh = k_ref[:, h * DK_A:(h + 1) * DK_A] * (DK_A ** -0.5)
        hs = slice(h * DV_A, (h + 1) * DV_A)
        vh = v_ref[:, hs]
        s = jnp.sum(qh * kh, axis=1, keepdims=True) * jnp.exp(li - mt)
        wi = jnp.exp(inter - mt)
        m_new = jnp.maximum(lf + m_prev, li)
        wc = jnp.exp(lf + m_prev - m_new)
        kws = kh * jnp.exp(li - m_new)
        n_old = n0_ref[:, h, :]
        qc_rows = []
        for b in range(nb):
            c_old = c0_ref[b, h]
            qc_rows.append(jnp.sum(_to_col(qh[b:b + 1]) * c_old, axis=0, keepdims=True))
            c_ref[b, h] = wc[b:b + 1] * c_old + _to_col(kws[b:b + 1]) * vh[b:b + 1]
        num = wi * jnp.concatenate(qc_rows, axis=0) + s * vh
        den = wi * jnp.sum(qh * n_old, axis=1, keepdims=True) + s
        hh = num / jnp.maximum(jnp.abs(den), jnp.exp(-mt))
        h_ref[:, hs] = _rms(hh, wmh_ref[:, hs]) * _sigmoid(og_ref[:, hs])
        n_ref[:, h, :] = wc * n_old + kws
        m_out = jnp.where(lane == h, m_new, m_out)
    m_ref[...] = m_out


def _mlstm_step(main, gates, b_i, b_f, w_mh, c0, n0, m0):
    nb = main.shape[0]
    qkw, vw = H_A * DK_A, H_A * DV_A
    bias = jnp.concatenate([b_i, b_f, jnp.zeros((GATE_LANES - 2 * H_A,), F32)]).reshape(1, GATE_LANES)
    full = lambda shape: pl.BlockSpec(shape, lambda i: (0,) * len(shape))
    return pl.pallas_call(
        _mlstm_step_kernel,
        grid=(1,),
        in_specs=[pl.BlockSpec((nb, qkw), lambda i: (0, 0)),
                  pl.BlockSpec((nb, qkw), lambda i: (0, 1)),
                  pl.BlockSpec((nb, vw), lambda i: (0, 1)),
                  pl.BlockSpec((nb, vw), lambda i: (0, 2)),
                  full((nb, GATE_LANES)), full((1, GATE_LANES)), full((1, vw)),
                  full((nb, H_A, DK_A, DV_A)), full((nb, H_A, DK_A)), full((nb, H_A))],
        out_specs=[full((nb, vw)), full((nb, H_A, DK_A, DV_A)), full((nb, H_A, DK_A)), full((nb, GATE_LANES))],
        out_shape=[jax.ShapeDtypeStruct((nb, vw), F32),
                   jax.ShapeDtypeStruct((nb, H_A, DK_A, DV_A), F32),
                   jax.ShapeDtypeStruct((nb, H_A, DK_A), F32),
                   jax.ShapeDtypeStruct((nb, GATE_LANES), F32)],
        compiler_params=_params("arbitrary"),
        name="mlstm_step",
    )(main, main, main, main, gates, bias, w_mh.reshape(1, vw), c0, n0, m0)


def _gdn_step_kernel(x_ref, t_ref, buf_ref, cw_ref, al_ref, dt_ref, wn_ref, s0_ref, o_ref, cn_ref, s_ref):
    x = x_ref[...]
    xc = x[:, :CONV_DIM_C]
    buf = buf_ref[...]
    cw = cw_ref[...]
    y = buf[0:1] * cw[0:1]
    for i in range(1, CONV_C - 1):
        y = y + buf[i:i + 1] * cw[i:i + 1]
    y = _silu(y + xc * cw[CONV_C - 1:CONV_C])
    cn_ref[...] = jnp.concatenate([buf[1:], xc], axis=0)
    t = t_ref[...]
    lane = lax.broadcasted_iota(jnp.int32, t.shape, 1)
    gts = jnp.where(lane < H_V_C, _sigmoid(t), -jnp.exp(al_ref[...]) * _softplus(t + dt_ref[...]))
    wn = wn_ref[...]
    k0, v0 = H_QK_C * DK_C, 2 * H_QK_C * DK_C
    for j in range(H_QK_C):
        qj = y[:, j * DK_C:(j + 1) * DK_C]
        kj = y[:, k0 + j * DK_C:k0 + (j + 1) * DK_C]
        qj = qj * lax.rsqrt(jnp.sum(qj * qj, axis=-1, keepdims=True) + EPS) * (DK_C ** -0.5)
        kj = kj * lax.rsqrt(jnp.sum(kj * kj, axis=-1, keepdims=True) + EPS)
        qcol, kcol = _to_col(qj), _to_col(kj)
        qk = jnp.sum(qj * kj, axis=1, keepdims=True)
        for e in range(GDN_REP):
            hv = j * GDN_REP + e
            hs = slice(hv * DV_C, (hv + 1) * DV_C)
            beta = gts[:, hv:hv + 1]
            eg = jnp.exp(gts[:, H_V_C + hv:H_V_C + hv + 1])
            s_old = s0_ref[hv]
            ks = jnp.sum(kcol * s_old, axis=0, keepdims=True)
            qs = jnp.sum(qcol * s_old, axis=0, keepdims=True)
            v_new = y[:, v0 + hv * DV_C:v0 + (hv + 1) * DV_C] * beta - (beta * eg) * ks
            o = eg * qs + qk * v_new
            s_ref[hv] = eg * s_old + kcol * v_new
            o_ref[:, hs] = _rms(o, wn) * _silu(x[:, CONV_DIM_C + hv * DV_C:CONV_DIM_C + (hv + 1) * DV_C])


def _gdn_step(main, tail, conv_buf, conv_w, a_log, dt_bias, w_norm, s0):
    nb, n_main = main.shape
    zeros = jnp.zeros((H_V_C,), F32)
    al = jnp.concatenate([zeros, a_log]).reshape(1, 2 * H_V_C)
    dt = jnp.concatenate([zeros, dt_bias]).reshape(1, 2 * H_V_C)
    hw = H_V_C * DV_C
    const = lambda shape: pl.BlockSpec(shape, lambda b: (0,) * len(shape))
    o, conv_new, s = pl.pallas_call(
        _gdn_step_kernel,
        grid=(nb,),
        in_specs=[pl.BlockSpec((None, 1, n_main), lambda b: (b, 0, 0)),
                  pl.BlockSpec((None, 1, 2 * H_V_C), lambda b: (b, 0, 0)),
                  pl.BlockSpec((None, CONV_C - 1, CONV_DIM_C), lambda b: (b, 0, 0)),
                  const((CONV_C, CONV_DIM_C)), const((1, 2 * H_V_C)), const((1, 2 * H_V_C)), const((1, DV_C)),
                  pl.BlockSpec((None, H_V_C, DK_C, DV_C), lambda b: (b, 0, 0, 0))],
        out_specs=[pl.BlockSpec((None, 1, hw), lambda b: (b, 0, 0)),
                   pl.BlockSpec((None, CONV_C - 1, CONV_DIM_C), lambda b: (b, 0, 0)),
                   pl.BlockSpec((None, H_V_C, DK_C, DV_C), lambda b: (b, 0, 0, 0))],
        out_shape=[jax.ShapeDtypeStruct((nb, 1, hw), F32),
                   jax.ShapeDtypeStruct((nb, CONV_C - 1, CONV_DIM_C), F32),
                   jax.ShapeDtypeStruct((nb, H_V_C, DK_C, DV_C), F32)],
        compiler_params=_params("parallel"),
        name="gdn_step",
    )(main.reshape(nb, 1, n_main), tail.reshape(nb, 1, 2 * H_V_C), conv_buf, conv_w, al, dt,
      w_norm.reshape(1, DV_C), s0)
    return o.reshape(nb, hw), s, conv_new


SCAN_PAGES = 8
PAGES_PER_BLOCK = MOBA_BLOCK // PAGE_SIZE


def _rope_rows_kernel(q_ref, k_ref, v_ref, cos_ref, sina_ref, sinb_ref, qr_ref, kr_ref, vr_ref):
    cosf, sina, sinb = cos_ref[...], sina_ref[...], sinb_ref[...]
    for h in range(H_B):
        sl = slice(h * HD_B, (h + 1) * HD_B)
        qr_ref[:, sl] = _rope_head(q_ref[:, sl], cosf, sina, sinb)
        kr_ref[:, sl] = _rope_head(k_ref[:, sl], cosf, sina, sinb)
    vr_ref[...] = v_ref[...]


def _rope_rows(main, pos):
    nb = main.shape[0]
    hw = H_B * HD_B
    c0 = A_COLS // hw
    cosf, sina, sinb = _rope_tables(jnp.full((1,), pos, jnp.int32))
    tab = pl.BlockSpec((1, HD_B), lambda i: (0, 0))
    out = pl.BlockSpec((nb, hw), lambda i: (0, 0))
    return pl.pallas_call(
        _rope_rows_kernel,
        grid=(1,),
        in_specs=[pl.BlockSpec((nb, hw), lambda i: (0, c0)),
                  pl.BlockSpec((nb, hw), lambda i: (0, c0 + 1)),
                  pl.BlockSpec((nb, hw), lambda i: (0, c0 + 2)), tab, tab, tab],
        out_specs=[out, out, out],
        out_shape=[jax.ShapeDtypeStruct((nb, hw), F32)] * 3,
        compiler_params=_params("arbitrary"),
        name="rope_rows",
    )(main, main, main, cosf, sina, sinb)


def _moba_scan_kernel(pt_ref, q_ref, *refs):
    pages, sel_ref, g_ref = refs[:SCAN_PAGES], refs[SCAN_PAGES], refs[SCAN_PAGES + 1]
    gi = pl.program_id(1)

    @pl.when(gi == 0)
    def _():
        g_ref[...] = jnp.full(g_ref.shape, NEG, F32)

    q = q_ref[...]
    lane = lax.broadcasted_iota(jnp.int32, g_ref.shape, 1)
    g = g_ref[...]
    for p in range(SCAN_PAGES // PAGES_PER_BLOCK):
        ksum = jnp.sum(pages[PAGES_PER_BLOCK * p][...], axis=0)
        for i in range(1, PAGES_PER_BLOCK):
            ksum = ksum + jnp.sum(pages[PAGES_PER_BLOCK * p + i][...], axis=0)
        val = jnp.sum(q * (ksum * (1.0 / MOBA_BLOCK)), axis=1, keepdims=True)
        g = jnp.where(lane == gi * (SCAN_PAGES // PAGES_PER_BLOCK) + p, val, g)
    g_ref[...] = g

    @pl.when(gi == pl.num_programs(1) - 1)
    def _():
        lanef = lane.astype(F32)
        gg = g
        out = jnp.zeros(g.shape, F32)
        for r in range(MOBA_TOPK):
            mx = jnp.max(gg, axis=1, keepdims=True)
            idx = jnp.min(jnp.where(gg == mx, lanef, float(g.shape[1])), axis=1, keepdims=True)
            out = jnp.where(lane == r, idx, out)
            gg = jnp.where(lanef == idx, -jnp.inf, gg)
        sel_ref[...] = out.astype(jnp.int32)


def _moba_scan(q_rot, cache_k, layer, page_table):
    nb = q_rot.shape[0]
    hw = H_B * HD_B
    n_pages = page_table.shape[1]
    assert n_pages % SCAN_PAGES == 0 and n_pages // PAGES_PER_BLOCK <= 128
    def page_spec(i):
        return pl.BlockSpec((None, None, PAGE_SIZE, H_B, HD_B),
                            lambda b, g, pt: (layer, pt[b * n_pages + g * SCAN_PAGES + i], 0, 0, 0))

    grid_spec = pltpu.PrefetchScalarGridSpec(
        num_scalar_prefetch=1,
        grid=(nb, n_pages // SCAN_PAGES),
        in_specs=[pl.BlockSpec((None, H_B, HD_B), lambda b, g, pt: (b, 0, 0))]
        + [page_spec(i) for i in range(SCAN_PAGES)],
        out_specs=pl.BlockSpec((None, H_B, 128), lambda b, g, pt: (b, 0, 0)),
        scratch_shapes=[pltpu.VMEM((H_B, 128), F32)],
    )
    return pl.pallas_call(
        _moba_scan_kernel,
        grid_spec=grid_spec,
        out_shape=jax.ShapeDtypeStruct((nb, H_B, 128), jnp.int32),
        compiler_params=_params("parallel", "arbitrary"),
        name="moba_scan",
    )(page_table.reshape(-1), q_rot.reshape(nb, H_B, HD_B), *([cache_k] * SCAN_PAGES))


SEL_PAGES = MOBA_TOPK * PAGES_PER_BLOCK


def _moba_decode_kernel(pg_ref, q_ref, kn_ref, vn_ref, *refs):
    ks, vs, o_ref = refs[:SEL_PAGES], refs[SEL_PAGES:2 * SEL_PAGES], refs[2 * SEL_PAGES]
    q = q_ref[...]
    scale = HD_B ** -0.5
    rid = lax.broadcasted_iota(jnp.int32, (PAGE_SIZE * H_B, 1), 0)
    mine = rid % H_B == pl.program_id(1)
    logits = [jnp.where(mine, jnp.sum(k[...] * q, axis=1, keepdims=True) * scale, NEG) for k in ks]
    own = jnp.sum(q * kn_ref[...], axis=1, keepdims=True) * scale
    m = own
    for s in logits:
        m = jnp.maximum(m, jnp.max(s, axis=0, keepdims=True))
    p_own = jnp.exp(own - m)
    l = p_own
    acc = p_own * vn_ref[...]
    for s, v in zip(logits, vs):
        p = jnp.exp(s - m)
        l = l + jnp.sum(p, axis=0, keepdims=True)
        acc = acc + jnp.sum(p * v[...], axis=0, keepdims=True)
    o_ref[...] = acc / l


def _moba_decode(q_rot, k_new, v_new, cache_k, cache_v, layer, pages):
    nb = q_rot.shape[0]
    hw = H_B * HD_B
    n_layers, n_pool = cache_k.shape[:2]
    kpool = cache_k.reshape(n_layers, n_pool, PAGE_SIZE * H_B, HD_B)
    vpool = cache_v.reshape(n_layers, n_pool, PAGE_SIZE * H_B, HD_B)

    def page_spec(i):
        return pl.BlockSpec((None, None, PAGE_SIZE * H_B, HD_B),
                            lambda b, h, pg: (layer, pg[(b * H_B + h) * SEL_PAGES + i], 0, 0))

    row = pl.BlockSpec((None, 1, HD_B), lambda b, h, pg: (b, 0, h))
    grid_spec = pltpu.PrefetchScalarGridSpec(
        num_scalar_prefetch=1,
        grid=(nb, H_B),
        in_specs=[row, row, row] + [page_spec(i) for i in range(SEL_PAGES)] * 2,
        out_specs=row,
    )
    r3 = lambda a: a.reshape(nb, 1, hw)
    out = pl.pallas_call(
        _moba_decode_kernel,
        grid_spec=grid_spec,
        out_shape=jax.ShapeDtypeStruct((nb, 1, hw), F32),
        compiler_params=_params("parallel", "arbitrary"),
        name="moba_decode",
    )(pages, r3(q_rot), r3(k_new), r3(v_new), *([kpool] * SEL_PAGES), *([vpool] * SEL_PAGES))
    return out.reshape(nb, hw)


def _ffn_mid_kernel(u_ref, buf_ref, cw_ref, cb_ref, act_ref, nb_ref):
    u = u_ref[...]
    cw = cw_ref[...]
    y = buf_ref[:, 0, :] * cw[0:1]
    for i in range(1, FFN_CONV - 1):
        y = y + buf_ref[:, i, :] * cw[i:i + 1]
        nb_ref[:, i - 1, :] = buf_ref[:, i, :]
    y = y + u * cw[FFN_CONV - 1:FFN_CONV] + cb_ref[...]
    nb_ref[:, FFN_CONV - 2, :] = u
    act_ref[...] = _silu(y[:, :D_FF]) * y[:, D_FF:]


def _ffn_mid_step(u, buf, conv_w, conv_b):
    nb = u.shape[0]
    full = lambda shape: pl.BlockSpec(shape, lambda i: (0,) * len(shape))
    return pl.pallas_call(
        _ffn_mid_kernel,
        grid=(1,),
        in_specs=[full((nb, 2 * D_FF)), full((nb, FFN_CONV - 1, 2 * D_FF)), full((FFN_CONV, 2 * D_FF)),
                  full((1, 2 * D_FF))],
        out_specs=[full((nb, D_FF)), full((nb, FFN_CONV - 1, 2 * D_FF))],
        out_shape=[jax.ShapeDtypeStruct((nb, D_FF), F32), jax.ShapeDtypeStruct((nb, FFN_CONV - 1, 2 * D_FF), F32)],
        compiler_params=_params("arbitrary"),
        name="ffn_mid_step",
    )(u, buf, conv_w, conv_b.reshape(1, 2 * D_FF))
```

```python
import functools
import math

import jax
import jax.numpy as jnp
from jax import lax
from jax.experimental import pallas as pl
from jax.experimental.pallas import tpu as pltpu

D_MODEL = 2048
DEPTH = 4
PAGE_SIZE = 128
H_A, DK_A, DV_A = 8, 64, 128
H_B, HD_B = 8, 128
MOBA_BLOCK, MOBA_TOPK = 256, 3
ROPE_DIMS = HD_B // 4
ROPE_THETA = 500000.0
H_QK_C, H_V_C, DK_C, DV_C, CONV_C = 16, 32, 128, 128, 4
CONV_DIM_C = 2 * H_QK_C * DK_C + H_V_C * DV_C
D_FF = 5504
FFN_CONV = 3
CHUNK = 64
EPS = 1e-6
NEG = -1e30
A_COLS = 2 * H_A * DK_A + 2 * H_A * DV_A
B_COLS = 3 * H_B * HD_B
GATE_COLS = 2 * H_A

F32 = jnp.float32
BF16 = jnp.bfloat16
VMEM_LIMIT_BYTES = 56 * 1024 * 1024
HALO = 16


def _params(*sem):
    return pltpu.CompilerParams(dimension_semantics=sem, vmem_limit_bytes=VMEM_LIMIT_BYTES)


def _rms(x, w):
    return x * lax.rsqrt(jnp.mean(x * x, axis=-1, keepdims=True) + EPS) * w


def _bdot(a, b):
    return jnp.dot(a, b, preferred_element_type=F32)


def _nmm_kernel(x_ref, nw_ref, *refs, split, transposed):
    wt_ref, o_ref, ot_ref, xn_ref = refs[-4:]
    j = pl.program_id(1)
    dims = (((1,), (1,)), ((), ())) if transposed else (((1,), (0,)), ((), ()))

    def mm(a, w_ref):
        return lax.dot_general(a, w_ref[...].astype(BF16), dims, preferred_element_type=F32)

    @pl.when(j == 0)
    def _():
        xn = _rms(x_ref[...], nw_ref[...]).astype(BF16)
        xn_ref[...] = xn
        ot_ref[...] = mm(xn, wt_ref)

    if split is None:
        o_ref[...] = mm(xn_ref[...], refs[0])
    else:
        @pl.when(j < split)
        def _():
            o_ref[...] = mm(xn_ref[...], refs[0])

        @pl.when(j >= split)
        def _():
            o_ref[...] = mm(xn_ref[...], refs[1])


def _norm_matmul(x, nw, sources, layer, tail, *, tm, tn, transposed=False):
    m, k = x.shape
    tiles = [n for _, n in sources]
    n_main = sum(tiles) * tn
    w_tail, nt, bt = tail

    def wspec(width, index):
        if transposed:
            return pl.BlockSpec((None, width, k), lambda i, j: (layer, index(j), 0))
        return pl.BlockSpec((None, k, width), lambda i, j: (layer, 0, index(j)))

    if len(sources) == 1:
        split = None
        w_specs = [wspec(tn, lambda j: j)]
    else:
        split = tiles[0]
        w_specs = [wspec(tn, lambda j: jnp.minimum(j, split - 1)), wspec(tn, lambda j: jnp.maximum(j - split, 0))]
    return pl.pallas_call(
        functools.partial(_nmm_kernel, split=split, transposed=transposed),
        grid=(m // tm, n_main // tn),
        in_specs=[pl.BlockSpec((tm, k), lambda i, j: (i, 0)),
                  pl.BlockSpec((1, k), lambda i, j: (0, 0))] + w_specs + [wspec(nt, lambda j: bt)],
        out_specs=[pl.BlockSpec((tm, tn), lambda i, j: (i, j)),
                   pl.BlockSpec((tm, nt), lambda i, j: (i, 0))],
        out_shape=[jax.ShapeDtypeStruct((m, n_main), F32), jax.ShapeDtypeStruct((m, nt), F32)],
        scratch_shapes=[pltpu.VMEM((tm, k), BF16)],
        compiler_params=_params("parallel", "arbitrary"),
        name="norm_matmul",
    )(x, nw.reshape(1, k), *[w for w, _ in sources], w_tail)


def _mmr_kernel(a_ref, w_ref, r_ref, o_ref):
    o_ref[...] = r_ref[...] + _bdot(a_ref[...].astype(BF16), w_ref[...].astype(BF16))


def _matmul_res(a, w, layer, res, *, tm, tn):
    m, k = a.shape
    n = w.shape[2]
    return pl.pallas_call(
        _mmr_kernel,
        grid=(m // tm, n // tn),
        in_specs=[pl.BlockSpec((tm, k), lambda i, j: (i, 0)),
                  pl.BlockSpec((None, k, tn), lambda i, j: (layer, 0, j)),
                  pl.BlockSpec((tm, tn), lambda i, j: (i, j))],
        out_specs=pl.BlockSpec((tm, tn), lambda i, j: (i, j)),
        out_shape=jax.ShapeDtypeStruct((m, n), F32),
        compiler_params=_params("parallel", "arbitrary"),
        name="matmul_res",
    )(a, w, res)


def _mm2r_kernel(a1_ref, a2_ref, w1_ref, w2_ref, r_ref, o_ref):
    o_ref[...] = (r_ref[...] + _bdot(a1_ref[...].astype(BF16), w1_ref[...].astype(BF16))
                  + _bdot(a2_ref[...].astype(BF16), w2_ref[...].astype(BF16)))


def _matmul2_res(a1, a2, w, layer, res, *, tm, tn):
    m, k = a1.shape
    n = w.shape[2]
    return pl.pallas_call(
        _mm2r_kernel,
        grid=(m // tm, n // tn),
        in_specs=[pl.BlockSpec((tm, k), lambda i, j: (i, 0)),
                  pl.BlockSpec((tm, k), lambda i, j: (i, 0)),
                  pl.BlockSpec((None, k, tn), lambda i, j: (layer, 0, j)),
                  pl.BlockSpec((None, k, tn), lambda i, j: (layer, 1, j)),
                  pl.BlockSpec((tm, tn), lambda i, j: (i, j))],
        out_specs=pl.BlockSpec((tm, tn), lambda i, j: (i, j)),
        out_shape=jax.ShapeDtypeStruct((m, n), F32),
        compiler_params=_params("parallel", "arbitrary"),
        name="matmul2_res",
    )(a1, a2, w, w, res)


def _final_norm_kernel(x_ref, w_ref, o_ref):
    o_ref[...] = _rms(x_ref[...], w_ref[...])


def _final_norm(x, w, *, tm):
    m, k = x.shape
    return pl.pallas_call(
        _final_norm_kernel,
        grid=(m // tm,),
        in_specs=[pl.BlockSpec((tm, k), lambda i: (i, 0)), pl.BlockSpec((1, k), lambda i: (0, 0))],
        out_specs=pl.BlockSpec((tm, k), lambda i: (i, 0)),
        out_shape=jax.ShapeDtypeStruct((m, k), F32),
        compiler_params=_params("parallel"),
        name="final_norm",
    )(x, w.reshape(1, k))


LANES = 128


def _ffn_up_kernel(xh_ref, x_ref, nw_ref, *refs, tiles_per_seq, n_sub):
    wg_ref, wu_refs = refs[0], refs[1:1 + n_sub]
    cwg_ref, cwu_ref, cbg_ref, cbu_ref, act_ref, lastg_ref, lastu_ref, xn_ref, wb_ref = refs[1 + n_sub:]
    i = pl.program_id(0)
    tn = n_sub * LANES

    @pl.when(pl.program_id(1) == 0)
    def _():
        nw = nw_ref[...]
        xn_ref[HALO:, :] = _rms(x_ref[...], nw).astype(BF16)
        halo = jnp.where(i % tiles_per_seq == 0, 0.0, _rms(xh_ref[...], nw))
        xn_ref[:HALO, :] = halo.astype(BF16)

    wb_ref[:, :tn] = wg_ref[...].astype(BF16)
    for s, w_ref in enumerate(wu_refs):
        wb_ref[:, tn + s * LANES:tn + (s + 1) * LANES] = w_ref[...].astype(BF16)
    v = _bdot(xn_ref[...], wb_ref[...])

    def conv(v, cw_ref, cb_ref, last_ref):
        last_ref[...] = v[v.shape[0] - 8:, :]
        cw = cw_ref[...]
        y = pltpu.roll(v, 2, 0) * cw[0:1] + pltpu.roll(v, 1, 0) * cw[1:2] + v * cw[2:3] + cb_ref[...]
        return y[HALO:, :]

    g = conv(v[:, :tn], cwg_ref, cbg_ref, lastg_ref)
    u = conv(v[:, tn:], cwu_ref, cbu_ref, lastu_ref)
    act_ref[...] = (g / (1.0 + jnp.exp(-g)) * u).astype(BF16)


def _ffn_up(x, nw, w_up, layer, conv_w, conv_b, seq_len, *, tm, tn):
    m, k = x.shape
    cwg, cwu = conv_w[:, :D_FF], conv_w[:, D_FF:]
    cbg, cbu = conv_b[:D_FF].reshape(1, D_FF), conv_b[D_FF:].reshape(1, D_FF)
    n_i, n_j = m // tm, pl.cdiv(D_FF, tn)
    hb = tm // HALO
    n_sub = tn // LANES
    half_blocks = D_FF // LANES

    gate_spec = pl.BlockSpec((None, k, tn), lambda i, j: (layer, 0, j))

    def up_spec(s):
        return pl.BlockSpec((None, k, LANES), lambda i, j: (
            layer, 0, half_blocks + jnp.minimum(j * n_sub + s, half_blocks - 1)))

    cwspec = pl.BlockSpec((FFN_CONV, tn), lambda i, j: (0, j))
    cbspec = pl.BlockSpec((1, tn), lambda i, j: (0, j))
    lastspec = pl.BlockSpec((None, 8, tn), lambda i, j: (i, 0, j))
    act, lastg, lastu = pl.pallas_call(
        functools.partial(_ffn_up_kernel, tiles_per_seq=seq_len // tm, n_sub=n_sub),
        grid=(n_i, n_j),
        in_specs=[pl.BlockSpec((HALO, k), lambda i, j: (jnp.maximum(i * hb - 1, 0), 0)),
                  pl.BlockSpec((tm, k), lambda i, j: (i, 0)),
                  pl.BlockSpec((1, k), lambda i, j: (0, 0))]
        + [gate_spec] + [up_spec(s) for s in range(n_sub)]
        + [cwspec, cwspec, cbspec, cbspec],
        out_specs=[pl.BlockSpec((tm, tn), lambda i, j: (i, j)), lastspec, lastspec],
        out_shape=[jax.ShapeDtypeStruct((m, D_FF), BF16),
                   jax.ShapeDtypeStruct((n_i, 8, D_FF), F32),
                   jax.ShapeDtypeStruct((n_i, 8, D_FF), F32)],
        scratch_shapes=[pltpu.VMEM((HALO + tm, k), BF16), pltpu.VMEM((k, 2 * tn), BF16)],
        compiler_params=_params("parallel", "arbitrary"),
        name="ffn_up",
    )(x, x, nw.reshape(1, k), *([w_up] * (1 + n_sub)), cwg, cwu, cbg, cbu)
    return act, lastg, lastu


def _trunk(x, pos0, c0, n0, m0, s0, gconv0, fconv0, kv_pages, params, *, tm):
    (norm_mix, norm_ffn, norm_final, ab_w_in, ab_b_i, ab_b_f, ab_w_mh, ab_w_out,
     c_w_in, c_conv_w, c_a_log, c_dt_bias, c_w_norm, c_w_out,
     ffn_w_up, ffn_conv_w, ffn_conv_b, ffn_w_down) = params
    ab_w_in_t = jnp.swapaxes(ab_w_in, 1, 2)
    ab_w_moba_t = ab_w_in_t[:, A_COLS + GATE_COLS:, :]
    c_w_in_t = jnp.swapaxes(c_w_in, 1, 2)
    b, t, d = x.shape
    prompt = kv_pages is None
    rows_k, rows_v, cs, ns, ms, ss, gcs, fcs = [], [], [], [], [], [], [], []
    h = x.reshape(b * t, d)
    for l in range(DEPTH):
        j = l // 2
        if l % 2 == 0:
            main, gates = _norm_matmul(h, norm_mix[l], [(ab_w_in_t, A_COLS // 512), (ab_w_moba_t, B_COLS // 512)], j,
                                       (ab_w_in_t, GATE_LANES, A_COLS // GATE_LANES), tm=tm, tn=512, transposed=True)
            if prompt:
                ha, c, n, m = _mlstm_prompt(main, gates, ab_b_i[j], ab_b_f[j], ab_w_mh[j], b, t)
                q_rot, kn, vn, k_bf, v_bf, km = _moba_prep(main, t)
                kmean = km[:, 0, :].reshape(b, t // MOBA_BLOCK, H_B * HD_B)
                ob = _moba_attention_prompt(q_rot, k_bf, v_bf, kmean, b, t)
                m = m[:, :, 0]
                kn = kn.reshape(b, t, H_B, HD_B)
                vn = vn.reshape(b, t, H_B, HD_B)
                h = _matmul2_res(ha, ob, ab_w_out, j, h, tm=tm, tn=512)
            else:
                cache_k, cache_v, page_table = kv_pages
                ha, c, n, m = _mlstm_step(main, gates, ab_b_i[j], ab_b_f[j], ab_w_mh[j], c0[j], n0[j], m0[j])
                m = m[:, :H_A]
                q_rot, kn, vn = _rope_rows(main, pos0)
                sel = _moba_scan(q_rot, cache_k, j, page_table)[:, :, :MOBA_TOPK]
                page_idx = sel[..., None] * PAGES_PER_BLOCK + jnp.arange(PAGES_PER_BLOCK, dtype=jnp.int32)
                pages = jnp.take_along_axis(page_table[:, None, :], page_idx.reshape(b, H_B, SEL_PAGES), axis=2)
                ob = _moba_decode(q_rot, kn, vn, cache_k, cache_v, j, pages.reshape(-1))
                kn = kn.reshape(b, t, H_B, HD_B)
                vn = vn.reshape(b, t, H_B, HD_B)
                h = _matmul2_res(ha, ob, ab_w_out, j, h, tm=tm, tn=512)
            rows_k.append(kn)
            rows_v.append(vn)
            cs.append(c)
            ns.append(n)
            ms.append(m)
        else:
            n_main = CONV_DIM_C + H_V_C * DV_C
            main, tail = _norm_matmul(h, norm_mix[l], [(c_w_in_t, n_main // 512)], j,
                                      (c_w_in_t, 2 * H_V_C, n_main // (2 * H_V_C)), tm=tm, tn=512, transposed=True)
            if prompt:
                o, s = _gdn_prompt(main, tail, c_conv_w[j], c_a_log[j], c_dt_bias[j], c_w_norm[j], b, t)
                gc = main.reshape(b, t, n_main)[:, t - (CONV_C - 1):, :CONV_DIM_C]
            else:
                o, s, gc = _gdn_step(main, tail, gconv0[j], c_conv_w[j], c_a_log[j], c_dt_bias[j], c_w_norm[j], s0[j])
            ss.append(s)
            gcs.append(gc)
            h = _matmul_res(o, c_w_out, j, h, tm=tm, tn=256)
        if prompt:
            act, lastg, lastu = _ffn_up(h, norm_ffn[l], ffn_w_up, l, ffn_conv_w[l], ffn_conv_b[l], t, tm=tm, tn=512)
            tps = t // tm
            last = jnp.concatenate([lastg[tps - 1::tps, 8 - (FFN_CONV - 1):], lastu[tps - 1::tps, 8 - (FFN_CONV - 1):]],
                                   axis=-1)
            fcs.append(last)
        else:
            n_tiles = (2 * D_FF) // 512
            n_tail = 2 * D_FF - n_tiles * 512
            main, tail = _norm_matmul(h, norm_ffn[l], [(ffn_w_up, n_tiles)], l,
                                      (ffn_w_up, n_tail, n_tiles * 512 // n_tail), tm=tm, tn=512)
            act, fc = _ffn_mid_step(jnp.concatenate([main, tail], axis=-1), fconv0[l], ffn_conv_w[l], ffn_conv_b[l])
            fcs.append(fc)
        h = _matmul_res(act, ffn_w_down, l, h, tm=tm, tn=256)
    y = _final_norm(h, norm_final, tm=tm).reshape(b, t, d)
    return (y, jnp.stack(rows_k), jnp.stack(rows_v), jnp.stack(cs), jnp.stack(ns), jnp.stack(ms),
            jnp.stack(ss), jnp.stack(gcs), jnp.stack(fcs))


def kernel(x_prompt, x_sample, cache_k, cache_v, page_table, state_mlstm_c, state_mlstm_n, state_mlstm_m,
           state_gdn_s, state_gdn_conv, state_ffn_conv, norm_mix, norm_ffn, norm_final,
           ab_w_in, ab_b_i, ab_b_f, ab_w_mh, ab_w_out, c_w_in, c_conv_w, c_a_log, c_dt_bias, c_w_norm, c_w_out,
           ffn_w_up, ffn_conv_w, ffn_conv_b, ffn_w_down):
    params = (norm_mix, norm_ffn, norm_final, ab_w_in, ab_b_i, ab_b_f, ab_w_mh, ab_w_out,
              c_w_in, c_conv_w, c_a_log, c_dt_bias, c_w_norm, c_w_out,
              ffn_w_up, ffn_conv_w, ffn_conv_b, ffn_w_down)
    batch = x_prompt.shape[0]
    n_ab, n_c = ab_w_in.shape[0], c_w_in.shape[0]
    zc = jnp.zeros((n_ab, batch, H_A, DK_A, DV_A), F32)
    zn = jnp.zeros((n_ab, batch, H_A, DK_A), F32)
    zm = jnp.zeros((n_ab, batch, H_A), F32)
    zs = jnp.zeros((n_c, batch, H_V_C, DK_C, DV_C), F32)
    zg = jnp.zeros((n_c, batch, CONV_C - 1, CONV_DIM_C), F32)
    outs_p = _trunk(x_prompt, 0, zc, zn, zm, zs, zg, None, None, params, tm=1024)
    past_len = page_table.shape[1] * PAGE_SIZE
    outs_s = _trunk(x_sample, past_len, state_mlstm_c, state_mlstm_n, state_mlstm_m,
                    state_gdn_s, state_gdn_conv, state_ffn_conv, (cache_k, cache_v, page_table), params,
                    tm=x_sample.shape[0] * x_sample.shape[1])
    return (outs_p[0], outs_s[0]) + tuple(outs_p[1:]) + tuple(outs_s[1:])


def _rope_tables(pos):
    half = ROPE_DIMS // 2
    inv = ROPE_THETA ** (-jnp.arange(half, dtype=F32) / half)
    ang = pos.astype(F32)[:, None] * inv[None, :]
    cos, sin = jnp.cos(ang), jnp.sin(ang)
    t = pos.shape[0]
    cosf = jnp.concatenate([cos, cos, jnp.ones((t, HD_B - ROPE_DIMS), F32)], axis=-1)
    sina = jnp.concatenate([-sin, jnp.zeros((t, HD_B - half), F32)], axis=-1)
    sinb = jnp.concatenate([jnp.zeros((t, half), F32), sin, jnp.zeros((t, HD_B - ROPE_DIMS), F32)], axis=-1)
    return cosf, sina, sinb


def _rope_head(x, cosf, sina, sinb):
    half = ROPE_DIMS // 2
    return x * cosf + pltpu.roll(x, HD_B - half, 1) * sina + pltpu.roll(x, half, 1) * sinb


def _moba_prep_kernel(q_ref, k_ref, v_ref, cos_ref, sina_ref, sinb_ref,
                      qr_ref, kr_ref, vr_ref, kb_ref, vb_ref, km_ref):
    cosf, sina, sinb = cos_ref[...], sina_ref[...], sinb_ref[...]
    for h in range(H_B):
        sl = slice(h * HD_B, (h + 1) * HD_B)
        qr_ref[:, sl] = _rope_head(q_ref[:, sl], cosf, sina, sinb)
        kr = _rope_head(k_ref[:, sl], cosf, sina, sinb)
        kr_ref[:, sl] = kr
        kb_ref[:, sl] = kr.astype(BF16)
        km_ref[:, sl] = jnp.broadcast_to(jnp.mean(kr, axis=0, keepdims=True), (8, HD_B))
    v = v_ref[...]
    vr_ref[...] = v
    vb_ref[...] = v.astype(BF16)


def _moba_prep(main, seq_len):
    m = main.shape[0]
    hw = H_B * HD_B
    nblk = seq_len // MOBA_BLOCK
    cosf, sina, sinb = _rope_tables(jnp.arange(seq_len, dtype=jnp.int32))
    c0 = A_COLS // hw
    row = pl.BlockSpec((MOBA_BLOCK, hw), lambda i: (i, 0))
    tab = pl.BlockSpec((MOBA_BLOCK, HD_B), lambda i: (i % nblk, 0))
    return pl.pallas_call(
        _moba_prep_kernel,
        grid=(m // MOBA_BLOCK,),
        in_specs=[pl.BlockSpec((MOBA_BLOCK, hw), lambda i: (i, c0)),
                  pl.BlockSpec((MOBA_BLOCK, hw), lambda i: (i, c0 + 1)),
                  pl.BlockSpec((MOBA_BLOCK, hw), lambda i: (i, c0 + 2)),
                  tab, tab, tab],
        out_specs=[row, row, row, row, row, pl.BlockSpec((None, 8, hw), lambda i: (i, 0, 0))],
        out_shape=[jax.ShapeDtypeStruct((m, hw), F32), jax.ShapeDtypeStruct((m, hw), F32),
                   jax.ShapeDtypeStruct((m, hw), F32), jax.ShapeDtypeStruct((m, hw), BF16),
                   jax.ShapeDtypeStruct((m, hw), BF16), jax.ShapeDtypeStruct((m // MOBA_BLOCK, 8, hw), F32)],
        compiler_params=_params("parallel"),
        name="moba_prep",
    )(main, main, main, cosf, sina, sinb)


MOBA_TRIP = 4


def _moba_kernel(q_ref, k_ref, v_ref, km_ref, o_ref, *, nblk):
    assert nblk % MOBA_TRIP == 0
    qi = pl.program_id(2)
    q = q_ref[...]
    gate = lax.dot_general(km_ref[...], q, (((1,), (1,)), ((), ())),
                           precision=lax.Precision.HIGHEST, preferred_element_type=F32)
    blk_t = lax.broadcasted_iota(jnp.int32, gate.shape, 0)
    blkf = blk_t.astype(F32)
    past = blk_t < qi
    g = jnp.where(past, gate, NEG)
    sel_t = jnp.zeros(gate.shape, F32)
    for _ in range(min(MOBA_TOPK, nblk)):
        mx = jnp.max(g, axis=0, keepdims=True)
        hit = blkf == jnp.min(jnp.where(g == mx, blkf, float(nblk)), axis=0, keepdims=True)
        sel_t = jnp.where(hit, 1.0, sel_t)
        g = jnp.where(hit, -jnp.inf, g)
    sel_t = jnp.where(past, sel_t, 0.0)
    sel = jnp.concatenate([sel_t, jnp.zeros((LANES - nblk, MOBA_BLOCK), F32)], axis=0).T
    blk = lax.broadcasted_iota(jnp.int32, sel.shape, 1)

    qb = q.astype(BF16)
    scale = HD_B ** -0.5

    def blocks(j, n):
        rows = pl.ds(pl.multiple_of(j * MOBA_BLOCK, MOBA_BLOCK), n * MOBA_BLOCK)
        s = lax.dot_general(qb, k_ref[rows, :], (((1,), (1,)), ((), ())), preferred_element_type=F32) * scale
        return s, v_ref[rows, :]

    s, vj = blocks(qi, 1)
    r = lax.broadcasted_iota(jnp.int32, s.shape, 0)
    c = lax.broadcasted_iota(jnp.int32, s.shape, 1)
    s = jnp.where(c <= r, s, NEG)
    m0 = jnp.max(s, axis=1, keepdims=True)
    p = jnp.exp(s - m0)
    l0 = jnp.sum(p, axis=1, keepdims=True)
    acc0 = _bdot(p.astype(BF16), vj)
    sub = lax.broadcasted_iota(jnp.int32, (MOBA_BLOCK, MOBA_TRIP * MOBA_BLOCK), 1) // MOBA_BLOCK

    def body(jj, carry):
        m, l, acc = carry
        keep = jnp.zeros(sub.shape, F32)
        for i in range(MOBA_TRIP):
            sel_i = jnp.sum(jnp.where(blk == MOBA_TRIP * jj + i, sel, 0.0), axis=1, keepdims=True)
            keep = jnp.where(sub == i, sel_i, keep)
        s, vj = blocks(MOBA_TRIP * jj, MOBA_TRIP)
        s = jnp.where(keep > 0.5, s, NEG)
        m_new = jnp.maximum(m, jnp.max(s, axis=1, keepdims=True))
        alpha = jnp.exp(m - m_new)
        p = jnp.exp(s - m_new)
        return m_new, alpha * l + jnp.sum(p, axis=1, keepdims=True), alpha * acc + _bdot(p.astype(BF16), vj)

    m, l, acc = lax.fori_loop(0, (qi + MOBA_TRIP - 1) // MOBA_TRIP, body, (m0, l0, acc0))
    o_ref[...] = (acc / l).astype(BF16)


def _moba_attention_prompt(q_rot, k_bf, v_bf, kmean, n_seq, seq_len):
    m = q_rot.shape[0]
    nblk = seq_len // MOBA_BLOCK
    return pl.pallas_call(
        functools.partial(_moba_kernel, nblk=nblk),
        grid=(n_seq, H_B, nblk),
        in_specs=[pl.BlockSpec((MOBA_BLOCK, HD_B), lambda b, h, i: (b * nblk + i, h)),
                  pl.BlockSpec((seq_len, HD_B), lambda b, h, i: (b, h)),
                  pl.BlockSpec((seq_len, HD_B), lambda b, h, i: (b, h)),
                  pl.BlockSpec((None, nblk, HD_B), lambda b, h, i: (b, 0, h))],
        out_specs=pl.BlockSpec((MOBA_BLOCK, HD_B), lambda b, h, i: (b * nblk + i, h)),
        out_shape=jax.ShapeDtypeStruct((m, H_B * HD_B), BF16),
        compiler_params=_params("parallel", "parallel", "arbitrary"),
        name="moba_attention",
    )(q_rot, k_bf, v_bf, kmean)


MLSTM_ROWS = 128
GATE_LANES = 128


def _mlstm_kernel(q_ref, k_ref, v_ref, og_ref, g_ref, bias_ref, wmh_ref, h_ref, c_ref, n_ref, m_ref,
                  cs_ref, ns_ref, ms_ref):
    r = pl.program_id(1)
    L = q_ref.shape[0]

    @pl.when(r == 0)
    def _():
        cs_ref[...] = jnp.zeros(cs_ref.shape, F32)
        ns_ref[...] = jnp.zeros(ns_ref.shape, F32)
        ms_ref[...] = jnp.zeros(ms_ref.shape, F32)

    gts = g_ref[...] + bias_ref[...]
    lane = lax.broadcasted_iota(jnp.int32, gts.shape, 1)
    lg = jnp.where(lane < H_A, gts, -_softplus(-gts))
    ti = lax.broadcasted_iota(jnp.int32, (L, L), 0)
    si = lax.broadcasted_iota(jnp.int32, (L, L), 1)
    incl = si <= ti
    bmat = jnp.dot(jnp.where(incl, 1.0, 0.0), lg, precision=lax.Precision.HIGHEST, preferred_element_type=F32)
    lgt, bt = lg.T, bmat.T
    nt = (((1,), (1,)), ((), ()))
    tn = (((0,), (0,)), ((), ()))

    for h in range(H_A):
        bcol, brow = bmat[:, H_A + h:H_A + h + 1], bt[H_A + h:H_A + h + 1, :]
        licol, lirow = lg[:, h:h + 1], lgt[h:h + 1, :]
        m_prev = ms_ref[h:h + 1, 0:1]
        d = jnp.where(incl, bcol - brow + lirow, NEG)
        inter = bcol + m_prev
        mt = jnp.maximum(inter, jnp.max(d, axis=1, keepdims=True))
        qh = q_ref[:, h * DK_A:(h + 1) * DK_A]
        kh = k_ref[:, h * DK_A:(h + 1) * DK_A] * (DK_A ** -0.5)
        vb = v_ref[:, h * DV_A:(h + 1) * DV_A].astype(BF16)
        qb = qh.astype(BF16)
        s = lax.dot_general(qb, kh.astype(BF16), nt, preferred_element_type=F32) * jnp.exp(d - mt)
        wi = jnp.exp(inter - mt)
        c_old = cs_ref[h]
        n_old = ns_ref[h:h + 1, :]
        num = wi * _bdot(qb, c_old.astype(BF16)) + _bdot(s.astype(BF16), vb)
        den = wi * jnp.sum(qh * n_old, axis=1, keepdims=True) + jnp.sum(s, axis=1, keepdims=True)
        hh = num / jnp.maximum(jnp.abs(den), jnp.exp(-mt))
        hs = slice(h * DV_A, (h + 1) * DV_A)
        og = og_ref[:, hs]
        h_ref[:, hs] = (_rms(hh, wmh_ref[:, hs]) / (1.0 + jnp.exp(-og))).astype(BF16)
        b_last = bcol[L - 1:L]
        gcol = b_last - bcol + licol
        m_new = jnp.maximum(b_last + m_prev, jnp.max(gcol, axis=0, keepdims=True))
        wc = jnp.exp(b_last + m_prev - m_new)
        kws = kh * jnp.exp(gcol - m_new)
        cs_ref[h] = wc * c_old + lax.dot_general(kws.astype(BF16), vb, tn, preferred_element_type=F32)
        ns_ref[h:h + 1, :] = wc * n_old + jnp.sum(kws, axis=0, keepdims=True)
        ms_ref[h:h + 1, :] = jnp.broadcast_to(m_new, (1, ms_ref.shape[1]))

    @pl.when(r == pl.num_programs(1) - 1)
    def _():
        c_ref[...] = cs_ref[...]
        n_ref[...] = ns_ref[...]
        m_ref[...] = ms_ref[...]


def _mlstm_prompt(main, gates, b_i, b_f, w_mh, n_seq, seq_len):
    m = main.shape[0]
    rows = MLSTM_ROWS
    nr = seq_len // rows
    qkw, vw = H_A * DK_A, H_A * DV_A
    bias = jnp.concatenate([b_i, b_f, jnp.zeros((GATE_LANES - 2 * H_A,), F32)]).reshape(1, GATE_LANES)
    return pl.pallas_call(
        _mlstm_kernel,
        grid=(n_seq, nr),
        in_specs=[pl.BlockSpec((rows, qkw), lambda b, r: (b * nr + r, 0)),
                  pl.BlockSpec((rows, qkw), lambda b, r: (b * nr + r, 1)),
                  pl.BlockSpec((rows, vw), lambda b, r: (b * nr + r, 1)),
                  pl.BlockSpec((rows, vw), lambda b, r: (b * nr + r, 2)),
                  pl.BlockSpec((rows, GATE_LANES), lambda b, r: (b * nr + r, 0)),
                  pl.BlockSpec((1, GATE_LANES), lambda b, r: (0, 0)),
                  pl.BlockSpec((1, vw), lambda b, r: (0, 0))],
        out_specs=[pl.BlockSpec((rows, vw), lambda b, r: (b * nr + r, 0)),
                   pl.BlockSpec((None, H_A, DK_A, DV_A), lambda b, r: (b, 0, 0, 0)),
                   pl.BlockSpec((None, H_A, DK_A), lambda b, r: (b, 0, 0)),
                   pl.BlockSpec((None, H_A, GATE_LANES), lambda b, r: (b, 0, 0))],
        out_shape=[jax.ShapeDtypeStruct((m, vw), BF16),
                   jax.ShapeDtypeStruct((n_seq, H_A, DK_A, DV_A), F32),
                   jax.ShapeDtypeStruct((n_seq, H_A, DK_A), F32),
                   jax.ShapeDtypeStruct((n_seq, H_A, GATE_LANES), F32)],
        scratch_shapes=[pltpu.VMEM((H_A, DK_A, DV_A), F32), pltpu.VMEM((H_A, DK_A), F32),
                        pltpu.VMEM((H_A, GATE_LANES), F32)],
        compiler_params=_params("parallel", "arbitrary"),
        name="mlstm_prompt",
    )(main, main, main, main, gates, bias, w_mh.reshape(1, vw))


GDN_ROWS = 256
GDN_REP = H_V_C // H_QK_C
GDN_QK = 8
GDN_HEADS = GDN_QK * GDN_REP
CONV_HALO = 8


def _silu(x):
    return x / (1.0 + jnp.exp(-x))


def _softplus(x):
    return jnp.maximum(x, 0.0) + jnp.log1p(jnp.exp(-jnp.abs(x)))


def _conv_silu(halo, x, w):
    ext = jnp.concatenate([halo, x], axis=0)
    y = pltpu.roll(ext, CONV_C - 1, 0) * w[0:1]
    for i in range(1, CONV_C - 1):
        y = y + pltpu.roll(ext, CONV_C - 1 - i, 0) * w[i:i + 1]
    y = y + ext * w[CONV_C - 1:CONV_C]
    return _silu(y[halo.shape[0]:])


def _gdn_kernel(qh_ref, kh_ref, vh_ref, q_ref, k_ref, v_ref, z_ref, t_ref, cwq_ref, cwk_ref, cwv_ref,
                al_ref, dt_ref, wn_ref, o_ref, s_ref, st_ref):
    j = pl.program_id(1)
    r = pl.program_id(2)
    L = CHUNK
    rows = q_ref.shape[0]

    @pl.when(r == 0)
    def _():
        st_ref[...] = jnp.zeros(st_ref.shape, F32)

    fresh = r == 0

    def halo(ref):
        return jnp.where(fresh, 0.0, ref[...])

    qs = _conv_silu(halo(qh_ref), q_ref[...], cwq_ref[...])
    ks = _conv_silu(halo(kh_ref), k_ref[...], cwk_ref[...])
    vs = _conv_silu(halo(vh_ref), v_ref[...], cwv_ref[...])
    q, k = [], []
    for a in range(GDN_QK):
        qa, ka = qs[:, a * DK_C:(a + 1) * DK_C], ks[:, a * DK_C:(a + 1) * DK_C]
        q.append(qa * lax.rsqrt(jnp.sum(qa * qa, axis=-1, keepdims=True) + EPS) * (DK_C ** -0.5))
        k.append(ka * lax.rsqrt(jnp.sum(ka * ka, axis=-1, keepdims=True) + EPS))

    t = t_ref[...]
    lane = lax.broadcasted_iota(jnp.int32, t.shape, 1)
    y = jnp.where(lane < H_V_C, 1.0 / (1.0 + jnp.exp(-t)), -jnp.exp(al_ref[...]) * _softplus(t + dt_ref[...]))
    lane128 = lax.broadcasted_iota(jnp.int32, (rows, 128), 1)
    cols = jnp.zeros((rows, 128), F32)
    betas = []
    for e in range(GDN_HEADS):
        hv = j * GDN_HEADS + e
        betas.append(jnp.sum(jnp.where(lane == hv, y, 0.0), axis=1, keepdims=True))
        g_e = jnp.sum(jnp.where(lane == H_V_C + hv, y, 0.0), axis=1, keepdims=True)
        hi = g_e.astype(BF16).astype(F32)
        mid = (g_e - hi).astype(BF16).astype(F32)
        for piece, val in enumerate((hi, mid, g_e - hi - mid)):
            cols = jnp.where(lane128 == piece * GDN_HEADS + e, val, cols)
    ri = lax.broadcasted_iota(jnp.int32, (rows, rows), 0)
    ci = lax.broadcasted_iota(jnp.int32, (rows, rows), 1)
    cum = jnp.where((ri // L == ci // L) & (ci <= ri), 1.0, 0.0).astype(BF16)
    pieces = _bdot(cum, cols.astype(BF16))
    dcols = pieces + pltpu.roll(pieces, 128 - GDN_HEADS, 1) + pltpu.roll(pieces, 128 - 2 * GDN_HEADS, 1)
    drows = dcols.T

    ti = lax.broadcasted_iota(jnp.int32, (L, L), 0)
    si = lax.broadcasted_iota(jnp.int32, (L, L), 1)
    strict, incl = si < ti, si <= ti
    eye = jnp.where(si == ti, 1.0, 0.0)
    wn = wn_ref[...]
    nt = (((1,), (1,)), ((), ()))
    tn = (((0,), (0,)), ((), ()))
    n_chunks = rows // L
    assert L == 64
    units = [(c, e) for c in range(n_chunks) for e in range(GDN_HEADS)]

    def mm(a, b):
        return _bdot(a.astype(BF16), b.astype(BF16))

    kk, qk = {}, {}
    for c in range(n_chunks):
        for a in range(GDN_QK):
            kcb = k[a][c * L:(c + 1) * L].astype(BF16)
            kk[c, a] = lax.dot_general(kcb, kcb, nt, preferred_element_type=F32)
            qk[c, a] = lax.dot_general(q[a][c * L:(c + 1) * L].astype(BF16), kcb, nt, preferred_element_type=F32)
    dcol, dlast, bcol, attn, p1 = {}, {}, {}, {}, {}
    for c, e in units:
        cs = slice(c * L, (c + 1) * L)
        dcol[c, e] = dcols[cs, e:e + 1]
        dlast[c, e] = dcols[(c + 1) * L - 1:(c + 1) * L, e:e + 1]
        bcol[c, e] = betas[e][cs]
        ex = jnp.exp(jnp.where(incl, dcol[c, e] - drows[e:e + 1, cs], 0.0))
        p1[c, e] = jnp.where(strict, -(bcol[c, e] * kk[c, e // GDN_REP] * ex), 0.0)
        attn[c, e] = jnp.where(incl, qk[c, e // GDN_REP] * ex, 0.0).astype(BF16)
    p2 = {u_: mm(p1[u_], p1[u_]) for u_ in units}
    p4 = {u_: mm(p2[u_], p2[u_]) for u_ in units}
    a0 = {u_: eye + p1[u_] + p2[u_] + mm(p1[u_], p2[u_]) for u_ in units}
    p8 = {u_: mm(p4[u_], p4[u_]) for u_ in units}
    p16 = {u_: mm(p8[u_], p8[u_]) for u_ in units}
    a1 = {u_: eye + p4[u_] + p8[u_] + mm(p4[u_], p8[u_]) for u_ in units}
    p32 = {u_: mm(p16[u_], p16[u_]) for u_ in units}
    a01 = {u_: mm(a0[u_], a1[u_]) for u_ in units}
    a2 = {u_: eye + p16[u_] + p32[u_] + mm(p16[u_], p32[u_]) for u_ in units}
    tinv = {u_: mm(a01[u_], a2[u_]) for u_ in units}
    sol, qd = {}, {}
    for c, e in units:
        cs = slice(c * L, (c + 1) * L)
        edc = jnp.exp(dcol[c, e])
        kc = k[e // GDN_REP][cs]
        rhs = jnp.concatenate([vs[cs, e * DV_C:(e + 1) * DV_C] * bcol[c, e], kc * (bcol[c, e] * edc)], axis=1)
        sol[c, e] = mm(tinv[c, e], rhs)
        qd[c, e] = q[e // GDN_REP][cs] * edc
    o_const, q_eff, s_mat, s_add = {}, {}, {}, {}
    for c, e in units:
        solb = sol[c, e].astype(BF16)
        au = _bdot(attn[c, e], solb)
        o_const[c, e] = au[:, :DV_C]
        q_eff[c, e] = (qd[c, e] - au[:, DV_C:]).astype(BF16)
        kd = (k[e // GDN_REP][c * L:(c + 1) * L] * jnp.exp(dlast[c, e] - dcol[c, e])).astype(BF16)
        ksol = lax.dot_general(kd, solb, tn, preferred_element_type=F32)
        s_add[c, e] = ksol[:, :DV_C]
        s_mat[c, e] = ksol[:, DV_C:].astype(BF16)

    s = [st_ref[e] for e in range(GDN_HEADS)]
    for c in range(n_chunks):
        for e in range(GDN_HEADS):
            sb = s[e].astype(BF16)
            o = _bdot(q_eff[c, e], sb) + o_const[c, e]
            s[e] = jnp.exp(dlast[c, e]) * s[e] - _bdot(s_mat[c, e], sb) + s_add[c, e]
            zc = z_ref[c * L:(c + 1) * L, e * DV_C:(e + 1) * DV_C]
            o_ref[c * L:(c + 1) * L, e * DV_C:(e + 1) * DV_C] = (_rms(o, wn) * _silu(zc)).astype(BF16)
    for e in range(GDN_HEADS):
        st_ref[e] = s[e]

    @pl.when(r == pl.num_programs(2) - 1)
    def _():
        s_ref[...] = st_ref[...]


def _gdn_prompt(main, tail, conv_w, a_log, dt_bias, w_norm, n_seq, seq_len):
    m = main.shape[0]
    rows = GDN_ROWS
    nr = seq_len // rows
    hb = rows // CONV_HALO
    qkw, vw = GDN_QK * DK_C, GDN_HEADS * DV_C
    k0 = H_QK_C * DK_C // qkw
    v0 = 2 * H_QK_C * DK_C // vw
    z0 = CONV_DIM_C // vw
    zeros = jnp.zeros((H_V_C,), F32)
    al = jnp.concatenate([zeros, a_log]).reshape(1, 2 * H_V_C)
    dt = jnp.concatenate([zeros, dt_bias]).reshape(1, 2 * H_V_C)

    def row_idx(b, r):
        return b * nr + r

    def halo_idx(b, r):
        return jnp.maximum(row_idx(b, r) * hb - 1, 0)

    return pl.pallas_call(
        _gdn_kernel,
        grid=(n_seq, H_QK_C // GDN_QK, nr),
        in_specs=[pl.BlockSpec((CONV_HALO, qkw), lambda b, j, r: (halo_idx(b, r), j)),
                  pl.BlockSpec((CONV_HALO, qkw), lambda b, j, r: (halo_idx(b, r), k0 + j)),
                  pl.BlockSpec((CONV_HALO, vw), lambda b, j, r: (halo_idx(b, r), v0 + j)),
                  pl.BlockSpec((rows, qkw), lambda b, j, r: (row_idx(b, r), j)),
                  pl.BlockSpec((rows, qkw), lambda b, j, r: (row_idx(b, r), k0 + j)),
                  pl.BlockSpec((rows, vw), lambda b, j, r: (row_idx(b, r), v0 + j)),
                  pl.BlockSpec((rows, vw), lambda b, j, r: (row_idx(b, r), z0 + j)),
                  pl.BlockSpec((rows, 2 * H_V_C), lambda b, j, r: (row_idx(b, r), 0)),
                  pl.BlockSpec((CONV_C, qkw), lambda b, j, r: (0, j)),
                  pl.BlockSpec((CONV_C, qkw), lambda b, j, r: (0, k0 + j)),
                  pl.BlockSpec((CONV_C, vw), lambda b, j, r: (0, v0 + j)),
                  pl.BlockSpec((1, 2 * H_V_C), lambda b, j, r: (0, 0)),
                  pl.BlockSpec((1, 2 * H_V_C), lambda b, j, r: (0, 0)),
                  pl.BlockSpec((1, DV_C), lambda b, j, r: (0, 0))],
        out_specs=[pl.BlockSpec((rows, vw), lambda b, j, r: (row_idx(b, r), j)),
                   pl.BlockSpec((None, GDN_HEADS, DK_C, DV_C), lambda b, j, r: (b, j, 0, 0))],
        out_shape=[jax.ShapeDtypeStruct((m, H_V_C * DV_C), BF16),
                   jax.ShapeDtypeStruct((n_seq, H_V_C, DK_C, DV_C), F32)],
        scratch_shapes=[pltpu.VMEM((GDN_HEADS, DK_C, DV_C), F32)],
        compiler_params=_params("parallel", "parallel", "arbitrary"),
        name="gdn_prompt",
    )(main, main, main, main, main, main, main, tail, conv_w, conv_w, conv_w, al, dt, w_norm.reshape(1, DV_C))


def _to_col(row):
    n = row.shape[1]
    i = lax.broadcasted_iota(jnp.int32, (n, n), 0)
    j = lax.broadcasted_iota(jnp.int32, (n, n), 1)
    return jnp.sum(jnp.where(i == j, jnp.broadcast_to(row, (n, n)), 0.0), axis=1, keepdims=True)


def _sigmoid(x):
    return 1.0 / (1.0 + jnp.exp(-x))


def _mlstm_step_kernel(q_ref, k_ref, v_ref, og_ref, g_ref, bias_ref, wmh_ref, c0_ref, n0_ref, m0_ref,
                       h_ref, c_ref, n_ref, m_ref):
    nb = q_ref.shape[0]
    gts = g_ref[...] + bias_ref[...]
    lane = lax.broadcasted_iota(jnp.int32, gts.shape, 1)
    m_out = jnp.zeros(gts.shape, F32)
    for h in range(H_A):
        li = gts[:, h:h + 1]
        lf = -_softplus(-gts[:, H_A + h:H_A + h + 1])
        m_prev = m0_ref[:, h:h + 1]
        inter = lf + m_prev
        mt = jnp.maximum(inter, li)
        qh = q_ref[:, h * DK_A:(h + 1) * DK_A]
        kh = k_ref[:, h * DK_A:(h + 1) * DK_A] * (DK_A ** -0.5)
        hs = slice(h * DV_A, (h + 1) * DV_A)
        vh = v_ref[:, hs]
        s = jnp.sum(qh * kh, axis=1, keepdims=True) * jnp.exp(li - mt)
        wi = jnp.exp(inter - mt)
        m_new = jnp.maximum(lf + m_prev, li)
        wc = jnp.exp(lf + m_prev - m_new)
        kws = kh * jnp.exp(li - m_new)
        n_old = n0_ref[:, h, :]
        qc_rows = []
        for b in range(nb):
            c_old = c0_ref[b, h]
            qc_rows.append(jnp.sum(_to_col(qh[b:b + 1]) * c_old, axis=0, keepdims=True))
            c_ref[b, h] = wc[b:b + 1] * c_old + _to_col(kws[b:b + 1]) * vh[b:b + 1]
        num = wi * jnp.concatenate(qc_rows, axis=0) + s * vh
        den = wi * jnp.sum(qh * n_old, axis=1, keepdims=True) + s
        hh = num / jnp.maximum(jnp.abs(den), jnp.exp(-mt))
        h_ref[:, hs] = _rms(hh, wmh_ref[:, hs]) * _sigmoid(og_ref[:, hs])
        n_ref[:, h, :] = wc * n_old + kws
        m_out = jnp.where(lane == h, m_new, m_out)
    m_ref[...] = m_out


def _mlstm_step(main, gates, b_i, b_f, w_mh, c0, n0, m0):
    nb = main.shape[0]
    qkw, vw = H_A * DK_A, H_A * DV_A
    bias = jnp.concatenate([b_i, b_f, jnp.zeros((GATE_LANES - 2 * H_A,), F32)]).reshape(1, GATE_LANES)
    full = lambda shape: pl.BlockSpec(shape, lambda i: (0,) * len(shape))
    return pl.pallas_call(
        _mlstm_step_kernel,
        grid=(1,),
        in_specs=[pl.BlockSpec((nb, qkw), lambda i: (0, 0)),
                  pl.BlockSpec((nb, qkw), lambda i: (0, 1)),
                  pl.BlockSpec((nb, vw), lambda i: (0, 1)),
                  pl.BlockSpec((nb, vw), lambda i: (0, 2)),
                  full((nb, GATE_LANES)), full((1, GATE_LANES)), full((1, vw)),
                  full((nb, H_A, DK_A, DV_A)), full((nb, H_A, DK_A)), full((nb, H_A))],
        out_specs=[full((nb, vw)), full((nb, H_A, DK_A, DV_A)), full((nb, H_A, DK_A)), full((nb, GATE_LANES))],
        out_shape=[jax.ShapeDtypeStruct((nb, vw), F32),
                   jax.ShapeDtypeStruct((nb, H_A, DK_A, DV_A), F32),
                   jax.ShapeDtypeStruct((nb, H_A, DK_A), F32),
                   jax.ShapeDtypeStruct((nb, GATE_LANES), F32)],
        compiler_params=_params("arbitrary"),
        name="mlstm_step",
    )(main, main, main, main, gates, bias, w_mh.reshape(1, vw), c0, n0, m0)


def _gdn_step_kernel(x_ref, t_ref, buf_ref, cw_ref, al_ref, dt_ref, wn_ref, s0_ref, o_ref, cn_ref, s_ref):
    x = x_ref[...]
    xc = x[:, :CONV_DIM_C]
    buf = buf_ref[...]
    cw = cw_ref[...]
    y = buf[0:1] * cw[0:1]
    for i in range(1, CONV_C - 1):
        y = y + buf[i:i + 1] * cw[i:i + 1]
    y = _silu(y + xc * cw[CONV_C - 1:CONV_C])
    cn_ref[...] = jnp.concatenate([buf[1:], xc], axis=0)
    t = t_ref[...]
    lane = lax.broadcasted_iota(jnp.int32, t.shape, 1)
    gts = jnp.where(lane < H_V_C, _sigmoid(t), -jnp.exp(al_ref[...]) * _softplus(t + dt_ref[...]))
    wn = wn_ref[...]
    k0, v0 = H_QK_C * DK_C, 2 * H_QK_C * DK_C
    for j in range(H_QK_C):
        qj = y[:, j * DK_C:(j + 1) * DK_C]
        kj = y[:, k0 + j * DK_C:k0 + (j + 1) * DK_C]
        qj = qj * lax.rsqrt(jnp.sum(qj * qj, axis=-1, keepdims=True) + EPS) * (DK_C ** -0.5)
        kj = kj * lax.rsqrt(jnp.sum(kj * kj, axis=-1, keepdims=True) + EPS)
        qcol, kcol = _to_col(qj), _to_col(kj)
        qk = jnp.sum(qj * kj, axis=1, keepdims=True)
        for e in range(GDN_REP):
            hv = j * GDN_REP + e
            hs = slice(hv * DV_C, (hv + 1) * DV_C)
            beta = gts[:, hv:hv + 1]
            eg = jnp.exp(gts[:, H_V_C + hv:H_V_C + hv + 1])
            s_old = s0_ref[hv]
            ks = jnp.sum(kcol * s_old, axis=0, keepdims=True)
            qs = jnp.sum(qcol * s_old, axis=0, keepdims=True)
            v_new = y[:, v0 + hv * DV_C:v0 + (hv + 1) * DV_C] * beta - (beta * eg) * ks
            o = eg * qs + qk * v_new
            s_ref[hv] = eg * s_old + kcol * v_new
            o_ref[:, hs] = _rms(o, wn) * _silu(x[:, CONV_DIM_C + hv * DV_C:CONV_DIM_C + (hv + 1) * DV_C])


def _gdn_step(main, tail, conv_buf, conv_w, a_log, dt_bias, w_norm, s0):
    nb, n_main = main.shape
    zeros = jnp.zeros((H_V_C,), F32)
    al = jnp.concatenate([zeros, a_log]).reshape(1, 2 * H_V_C)
    dt = jnp.concatenate([zeros, dt_bias]).reshape(1, 2 * H_V_C)
    hw = H_V_C * DV_C
    const = lambda shape: pl.BlockSpec(shape, lambda b: (0,) * len(shape))
    o, conv_new, s = pl.pallas_call(
        _gdn_step_kernel,
        grid=(nb,),
        in_specs=[pl.BlockSpec((None, 1, n_main), lambda b: (b, 0, 0)),
                  pl.BlockSpec((None, 1, 2 * H_V_C), lambda b: (b, 0, 0)),
                  pl.BlockSpec((None, CONV_C - 1, CONV_DIM_C), lambda b: (b, 0, 0)),
                  const((CONV_C, CONV_DIM_C)), const((1, 2 * H_V_C)), const((1, 2 * H_V_C)), const((1, DV_C)),
                  pl.BlockSpec((None, H_V_C, DK_C, DV_C), lambda b: (b, 0, 0, 0))],
        out_specs=[pl.BlockSpec((None, 1, hw), lambda b: (b, 0, 0)),
                   pl.BlockSpec((None, CONV_C - 1, CONV_DIM_C), lambda b: (b, 0, 0)),
                   pl.BlockSpec((None, H_V_C, DK_C, DV_C), lambda b: (b, 0, 0, 0))],
        out_shape=[jax.ShapeDtypeStruct((nb, 1, hw), F32),
                   jax.ShapeDtypeStruct((nb, CONV_C - 1, CONV_DIM_C), F32),
                   jax.ShapeDtypeStruct((nb, H_V_C, DK_C, DV_C), F32)],
        compiler_params=_params("parallel"),
        name="gdn_step",
    )(main.reshape(nb, 1, n_main), tail.reshape(nb, 1, 2 * H_V_C), conv_buf, conv_w, al, dt,
      w_norm.reshape(1, DV_C), s0)
    return o.reshape(nb, hw), s, conv_new


SCAN_PAGES = 8
PAGES_PER_BLOCK = MOBA_BLOCK // PAGE_SIZE


def _rope_rows_kernel(q_ref, k_ref, v_ref, cos_ref, sina_ref, sinb_ref, qr_ref, kr_ref, vr_ref):
    cosf, sina, sinb = cos_ref[...], sina_ref[...], sinb_ref[...]
    for h in range(H_B):
        sl = slice(h * HD_B, (h + 1) * HD_B)
        qr_ref[:, sl] = _rope_head(q_ref[:, sl], cosf, sina, sinb)
        kr_ref[:, sl] = _rope_head(k_ref[:, sl], cosf, sina, sinb)
    vr_ref[...] = v_ref[...]


def _rope_rows(main, pos):
    nb = main.shape[0]
    hw = H_B * HD_B
    c0 = A_COLS // hw
    cosf, sina, sinb = _rope_tables(jnp.full((1,), pos, jnp.int32))
    tab = pl.BlockSpec((1, HD_B), lambda i: (0, 0))
    out = pl.BlockSpec((nb, hw), lambda i: (0, 0))
    return pl.pallas_call(
        _rope_rows_kernel,
        grid=(1,),
        in_specs=[pl.BlockSpec((nb, hw), lambda i: (0, c0)),
                  pl.BlockSpec((nb, hw), lambda i: (0, c0 + 1)),
                  pl.BlockSpec((nb, hw), lambda i: (0, c0 + 2)), tab, tab, tab],
        out_specs=[out, out, out],
        out_shape=[jax.ShapeDtypeStruct((nb, hw), F32)] * 3,
        compiler_params=_params("arbitrary"),
        name="rope_rows",
    )(main, main, main, cosf, sina, sinb)


def _moba_scan_kernel(pt_ref, q_ref, *refs):
    pages, sel_ref, g_ref = refs[:SCAN_PAGES], refs[SCAN_PAGES], refs[SCAN_PAGES + 1]
    gi = pl.program_id(1)

    @pl.when(gi == 0)
    def _():
        g_ref[...] = jnp.full(g_ref.shape, NEG, F32)

    q = q_ref[...]
    lane = lax.broadcasted_iota(jnp.int32, g_ref.shape, 1)
    g = g_ref[...]
    for p in range(SCAN_PAGES // PAGES_PER_BLOCK):
        ksum = jnp.sum(pages[PAGES_PER_BLOCK * p][...], axis=0)
        for i in range(1, PAGES_PER_BLOCK):
            ksum = ksum + jnp.sum(pages[PAGES_PER_BLOCK * p + i][...], axis=0)
        val = jnp.sum(q * (ksum * (1.0 / MOBA_BLOCK)), axis=1, keepdims=True)
        g = jnp.where(lane == gi * (SCAN_PAGES // PAGES_PER_BLOCK) + p, val, g)
    g_ref[...] = g

    @pl.when(gi == pl.num_programs(1) - 1)
    def _():
        lanef = lane.astype(F32)
        gg = g
        out = jnp.zeros(g.shape, F32)
        for r in range(MOBA_TOPK):
            mx = jnp.max(gg, axis=1, keepdims=True)
            idx = jnp.min(jnp.where(gg == mx, lanef, float(g.shape[1])), axis=1, keepdims=True)
            out = jnp.where(lane == r, idx, out)
            gg = jnp.where(lanef == idx, -jnp.inf, gg)
        sel_ref[...] = out.astype(jnp.int32)


def _moba_scan(q_rot, cache_k, layer, page_table):
    nb = q_rot.shape[0]
    hw = H_B * HD_B
    n_pages = page_table.shape[1]
    assert n_pages % SCAN_PAGES == 0 and n_pages // PAGES_PER_BLOCK <= 128
    def page_spec(i):
        return pl.BlockSpec((None, None, PAGE_SIZE, H_B, HD_B),
                            lambda b, g, pt: (layer, pt[b * n_pages + g * SCAN_PAGES + i], 0, 0, 0))

    grid_spec = pltpu.PrefetchScalarGridSpec(
        num_scalar_prefetch=1,
        grid=(nb, n_pages // SCAN_PAGES),
        in_specs=[pl.BlockSpec((None, H_B, HD_B), lambda b, g, pt: (b, 0, 0))]
        + [page_spec(i) for i in range(SCAN_PAGES)],
        out_specs=pl.BlockSpec((None, H_B, 128), lambda b, g, pt: (b, 0, 0)),
        scratch_shapes=[pltpu.VMEM((H_B, 128), F32)],
    )
    return pl.pallas_call(
        _moba_scan_kernel,
        grid_spec=grid_spec,
        out_shape=jax.ShapeDtypeStruct((nb, H_B, 128), jnp.int32),
        compiler_params=_params("parallel", "arbitrary"),
        name="moba_scan",
    )(page_table.reshape(-1), q_rot.reshape(nb, H_B, HD_B), *([cache_k] * SCAN_PAGES))


SEL_PAGES = MOBA_TOPK * PAGES_PER_BLOCK


def _moba_decode_kernel(pg_ref, q_ref, kn_ref, vn_ref, *refs):
    ks, vs, o_ref = refs[:SEL_PAGES], refs[SEL_PAGES:2 * SEL_PAGES], refs[2 * SEL_PAGES]
    q = q_ref[...]
    scale = HD_B ** -0.5
    rid = lax.broadcasted_iota(jnp.int32, (PAGE_SIZE * H_B, 1), 0)
    mine = rid % H_B == pl.program_id(1)
    logits = [jnp.where(mine, jnp.sum(k[...] * q, axis=1, keepdims=True) * scale, NEG) for k in ks]
    own = jnp.sum(q * kn_ref[...], axis=1, keepdims=True) * scale
    m = own
    for s in logits:
        m = jnp.maximum(m, jnp.max(s, axis=0, keepdims=True))
    p_own = jnp.exp(own - m)
    l = p_own
    acc = p_own * vn_ref[...]
    for s, v in zip(logits, vs):
        p = jnp.exp(s - m)
        l = l + jnp.sum(p, axis=0, keepdims=True)
        acc = acc + jnp.sum(p * v[...], axis=0, keepdims=True)
    o_ref[...] = acc / l


def _moba_decode(q_rot, k_new, v_new, cache_k, cache_v, layer, pages):
    nb = q_rot.shape[0]
    hw = H_B * HD_B
    n_layers, n_pool = cache_k.shape[:2]
    kpool = cache_k.reshape(n_layers, n_pool, PAGE_SIZE * H_B, HD_B)
    vpool = cache_v.reshape(n_layers, n_pool, PAGE_SIZE * H_B, HD_B)

    def page_spec(i):
        return pl.BlockSpec((None, None, PAGE_SIZE * H_B, HD_B),
                            lambda b, h, pg: (layer, pg[(b * H_B + h) * SEL_PAGES + i], 0, 0))

    row = pl.BlockSpec((None, 1, HD_B), lambda b, h, pg: (b, 0, h))
    grid_spec = pltpu.PrefetchScalarGridSpec(
        num_scalar_prefetch=1,
        grid=(nb, H_B),
        in_specs=[row, row, row] + [page_spec(i) for i in range(SEL_PAGES)] * 2,
        out_specs=row,
    )
    r3 = lambda a: a.reshape(nb, 1, hw)
    out = pl.pallas_call(
        _moba_decode_kernel,
        grid_spec=grid_spec,
        out_shape=jax.ShapeDtypeStruct((nb, 1, hw), F32),
        compiler_params=_params("parallel", "arbitrary"),
        name="moba_decode",
    )(pages, r3(q_rot), r3(k_new), r3(v_new), *([kpool] * SEL_PAGES), *([vpool] * SEL_PAGES))
    return out.reshape(nb, hw)


def _ffn_mid_kernel(u_ref, buf_ref, cw_ref, cb_ref, act_ref, nb_ref):
    u = u_ref[...]
    cw = cw_ref[...]
    y = buf_ref[:, 0, :] * cw[0:1]
    for i in range(1, FFN_CONV - 1):
        y = y + buf_ref[:, i, :] * cw[i:i + 1]
        nb_ref[:, i - 1, :] = buf_ref[:, i, :]
    y = y + u * cw[FFN_CONV - 1:FFN_CONV] + cb_ref[...]
    nb_ref[:, FFN_CONV - 2, :] = u
    act_ref[...] = _silu(y[:, :D_FF]) * y[:, D_FF:]


def _ffn_mid_step(u, buf, conv_w, conv_b):
    nb = u.shape[0]
    full = lambda shape: pl.BlockSpec(shape, lambda i: (0,) * len(shape))
    return pl.pallas_call(
        _ffn_mid_kernel,
        grid=(1,),
        in_specs=[full((nb, 2 * D_FF)), full((nb, FFN_CONV - 1, 2 * D_FF)), full((FFN_CONV, 2 * D_FF)),
                  full((1, 2 * D_FF))],
        out_specs=[full((nb, D_FF)), full((nb, FFN_CONV - 1, 2 * D_FF))],
        out_shape=[jax.ShapeDtypeStruct((nb, D_FF), F32), jax.ShapeDtypeStruct((nb, FFN_CONV - 1, 2 * D_FF), F32)],
        compiler_params=_params("arbitrary"),
        name="ffn_mid_step",
    )(u, buf, conv_w, conv_b.reshape(1, 2 * D_FF))
```

```python
import functools
import math

import jax
import jax.numpy as jnp
from jax import lax
from jax.experimental import pallas as pl
from jax.experimental.pallas import tpu as pltpu

D_MODEL = 2048
DEPTH = 4
PAGE_SIZE = 128
H_A, DK_A, DV_A = 8, 64, 128
H_B, HD_B = 8, 128
MOBA_BLOCK, MOBA_TOPK = 256, 3
ROPE_DIMS = HD_B // 4
ROPE_THETA = 500000.0
H_QK_C, H_V_C, DK_C, DV_C, CONV_C = 16, 32, 128, 128, 4
CONV_DIM_C = 2 * H_QK_C * DK_C + H_V_C * DV_C
D_FF = 5504
FFN_CONV = 3
CHUNK = 64
EPS = 1e-6
NEG = -1e30
A_COLS = 2 * H_A * DK_A + 2 * H_A * DV_A
B_COLS = 3 * H_B * HD_B
GATE_COLS = 2 * H_A

F32 = jnp.float32
BF16 = jnp.bfloat16
VMEM_LIMIT_BYTES = 56 * 1024 * 1024
HALO = 16


def _params(*sem):
    return pltpu.CompilerParams(dimension_semantics=sem, vmem_limit_bytes=VMEM_LIMIT_BYTES)


def _rms(x, w):
    return x * lax.rsqrt(jnp.mean(x * x, axis=-1, keepdims=True) + EPS) * w


def _bdot(a, b):
    return jnp.dot(a, b, preferred_element_type=F32)


def _nmm_kernel(x_ref, nw_ref, *refs, split, transposed):
    wt_ref, o_ref, ot_ref, xn_ref = refs[-4:]
    j = pl.program_id(1)
    dims = (((1,), (1,)), ((), ())) if transposed else (((1,), (0,)), ((), ()))

    def mm(a, w_ref):
        return lax.dot_general(a, w_ref[...].astype(BF16), dims, preferred_element_type=F32)

    @pl.when(j == 0)
    def _():
        xn = _rms(x_ref[...], nw_ref[...]).astype(BF16)
        xn_ref[...] = xn
        ot_ref[...] = mm(xn, wt_ref)

    if split is None:
        o_ref[...] = mm(xn_ref[...], refs[0])
    else:
        @pl.when(j < split)
        def _():
            o_ref[...] = mm(xn_ref[...], refs[0])

        @pl.when(j >= split)
        def _():
            o_ref[...] = mm(xn_ref[...], refs[1])


def _norm_matmul(x, nw, sources, layer, tail, *, tm, tn, transposed=False):
    m, k = x.shape
    tiles = [n for _, n in sources]
    n_main = sum(tiles) * tn
    w_tail, nt, bt = tail

    def wspec(width, index):
        if transposed:
            return pl.BlockSpec((None, width, k), lambda i, j: (layer, index(j), 0))
        return pl.BlockSpec((None, k, width), lambda i, j: (layer, 0, index(j)))

    if len(sources) == 1:
        split = None
        w_specs = [wspec(tn, lambda j: j)]
    else:
        split = tiles[0]
        w_specs = [wspec(tn, lambda j: jnp.minimum(j, split - 1)), wspec(tn, lambda j: jnp.maximum(j - split, 0))]
    return pl.pallas_call(
        functools.partial(_nmm_kernel, split=split, transposed=transposed),
        grid=(m // tm, n_main // tn),
        in_specs=[pl.BlockSpec((tm, k), lambda i, j: (i, 0)),
                  pl.BlockSpec((1, k), lambda i, j: (0, 0))] + w_specs + [wspec(nt, lambda j: bt)],
        out_specs=[pl.BlockSpec((tm, tn), lambda i, j: (i, j)),
                   pl.BlockSpec((tm, nt), lambda i, j: (i, 0))],
        out_shape=[jax.ShapeDtypeStruct((m, n_main), F32), jax.ShapeDtypeStruct((m, nt), F32)],
        scratch_shapes=[pltpu.VMEM((tm, k), BF16)],
        compiler_params=_params("parallel", "arbitrary"),
        name="norm_matmul",
    )(x, nw.reshape(1, k), *[w for w, _ in sources], w_tail)


def _mmr_kernel(a_ref, w_ref, r_ref, o_ref):
    o_ref[...] = r_ref[...] + _bdot(a_ref[...].astype(BF16), w_ref[...].astype(BF16))


def _matmul_res(a, w, layer, res, *, tm, tn):
    m, k = a.shape
    n = w.shape[2]
    return pl.pallas_call(
        _mmr_kernel,
        grid=(m // tm, n // tn),
        in_specs=[pl.BlockSpec((tm, k), lambda i, j: (i, 0)),
                  pl.BlockSpec((None, k, tn), lambda i, j: (layer, 0, j)),
                  pl.BlockSpec((tm, tn), lambda i, j: (i, j))],
        out_specs=pl.BlockSpec((tm, tn), lambda i, j: (i, j)),
        out_shape=jax.ShapeDtypeStruct((m, n), F32),
        compiler_params=_params("parallel", "arbitrary"),
        name="matmul_res",
    )(a, w, res)


def _mm2r_kernel(a1_ref, a2_ref, w1_ref, w2_ref, r_ref, o_ref):
    o_ref[...] = (r_ref[...] + _bdot(a1_ref[...].astype(BF16), w1_ref[...].astype(BF16))
                  + _bdot(a2_ref[...].astype(BF16), w2_ref[...].astype(BF16)))


def _matmul2_res(a1, a2, w, layer, res, *, tm, tn):
    m, k = a1.shape
    n = w.shape[2]
    return pl.pallas_call(
        _mm2r_kernel,
        grid=(m // tm, n // tn),
        in_specs=[pl.BlockSpec((tm, k), lambda i, j: (i, 0)),
                  pl.BlockSpec((tm, k), lambda i, j: (i, 0)),
                  pl.BlockSpec((None, k, tn), lambda i, j: (layer, 0, j)),
                  pl.BlockSpec((None, k, tn), lambda i, j: (layer, 1, j)),
                  pl.BlockSpec((tm, tn), lambda i, j: (i, j))],
        out_specs=pl.BlockSpec((tm, tn), lambda i, j: (i, j)),
        out_shape=jax.ShapeDtypeStruct((m, n), F32),
        compiler_params=_params("parallel", "arbitrary"),
        name="matmul2_res",
    )(a1, a2, w, w, res)


def _final_norm_kernel(x_ref, w_ref, o_ref):
    o_ref[...] = _rms(x_ref[...], w_ref[...])


def _final_norm(x, w, *, tm):
    m, k = x.shape
    return pl.pallas_call(
        _final_norm_kernel,
        grid=(m // tm,),
        in_specs=[pl.BlockSpec((tm, k), lambda i: (i, 0)), pl.BlockSpec((1, k), lambda i: (0, 0))],
        out_specs=pl.BlockSpec((tm, k), lambda i: (i, 0)),
        out_shape=jax.ShapeDtypeStruct((m, k), F32),
        compiler_params=_params("parallel"),
        name="final_norm",
    )(x, w.reshape(1, k))


LANES = 128


def _ffn_up_kernel(xh_ref, x_ref, nw_ref, *refs, tiles_per_seq, n_sub):
    wg_ref, wu_refs = refs[0], refs[1:1 + n_sub]
    cwg_ref, cwu_ref, cbg_ref, cbu_ref, act_ref, lastg_ref, lastu_ref, xn_ref, wb_ref = refs[1 + n_sub:]
    i = pl.program_id(0)
    tn = n_sub * LANES

    @pl.when(pl.program_id(1) == 0)
    def _():
        nw = nw_ref[...]
        xn_ref[HALO:, :] = _rms(x_ref[...], nw).astype(BF16)
        halo = jnp.where(i % tiles_per_seq == 0, 0.0, _rms(xh_ref[...], nw))
        xn_ref[:HALO, :] = halo.astype(BF16)

    wb_ref[:, :tn] = wg_ref[...].astype(BF16)
    for s, w_ref in enumerate(wu_refs):
        wb_ref[:, tn + s * LANES:tn + (s + 1) * LANES] = w_ref[...].astype(BF16)
    v = _bdot(xn_ref[...], wb_ref[...])

    def conv(v, cw_ref, cb_ref, last_ref):
        last_ref[...] = v[v.shape[0] - 8:, :]
        cw = cw_ref[...]
        y = pltpu.roll(v, 2, 0) * cw[0:1] + pltpu.roll(v, 1, 0) * cw[1:2] + v * cw[2:3] + cb_ref[...]
        return y[HALO:, :]

    g = conv(v[:, :tn], cwg_ref, cbg_ref, lastg_ref)
    u = conv(v[:, tn:], cwu_ref, cbu_ref, lastu_ref)
    act_ref[...] = (g / (1.0 + jnp.exp(-g)) * u).astype(BF16)


def _ffn_up(x, nw, w_up, layer, conv_w, conv_b, seq_len, *, tm, tn):
    m, k = x.shape
    cwg, cwu = conv_w[:, :D_FF], conv_w[:, D_FF:]
    cbg, cbu = conv_b[:D_FF].reshape(1, D_FF), conv_b[D_FF:].reshape(1, D_FF)
    n_i, n_j = m // tm, pl.cdiv(D_FF, tn)
    hb = tm // HALO
    n_sub = tn // LANES
    half_blocks = D_FF // LANES

    gate_spec = pl.BlockSpec((None, k, tn), lambda i, j: (layer, 0, j))

    def up_spec(s):
        return pl.BlockSpec((None, k, LANES), lambda i, j: (
            layer, 0, half_blocks + jnp.minimum(j * n_sub + s, half_blocks - 1)))

    cwspec = pl.BlockSpec((FFN_CONV, tn), lambda i, j: (0, j))
    cbspec = pl.BlockSpec((1, tn), lambda i, j: (0, j))
    lastspec = pl.BlockSpec((None, 8, tn), lambda i, j: (i, 0, j))
    act, lastg, lastu = pl.pallas_call(
        functools.partial(_ffn_up_kernel, tiles_per_seq=seq_len // tm, n_sub=n_sub),
        grid=(n_i, n_j),
        in_specs=[pl.BlockSpec((HALO, k), lambda i, j: (jnp.maximum(i * hb - 1, 0), 0)),
                  pl.BlockSpec((tm, k), lambda i, j: (i, 0)),
                  pl.BlockSpec((1, k), lambda i, j: (0, 0))]
        + [gate_spec] + [up_spec(s) for s in range(n_sub)]
        + [cwspec, cwspec, cbspec, cbspec],
        out_specs=[pl.BlockSpec((tm, tn), lambda i, j: (i, j)), lastspec, lastspec],
        out_shape=[jax.ShapeDtypeStruct((m, D_FF), BF16),
                   jax.ShapeDtypeStruct((n_i, 8, D_FF), F32),
                   jax.ShapeDtypeStruct((n_i, 8, D_FF), F32)],
        scratch_shapes=[pltpu.VMEM((HALO + tm, k), BF16), pltpu.VMEM((k, 2 * tn), BF16)],
        compiler_params=_params("parallel", "arbitrary"),
        name="ffn_up",
    )(x, x, nw.reshape(1, k), *([w_up] * (1 + n_sub)), cwg, cwu, cbg, cbu)
    return act, lastg, lastu


def _trunk(x, pos0, c0, n0, m0, s0, gconv0, fconv0, kv_pages, params, *, tm):
    (norm_mix, norm_ffn, norm_final, ab_w_in, ab_b_i, ab_b_f, ab_w_mh, ab_w_out,
     c_w_in, c_conv_w, c_a_log, c_dt_bias, c_w_norm, c_w_out,
     ffn_w_up, ffn_conv_w, ffn_conv_b, ffn_w_down) = params
    ab_w_in_t = jnp.swapaxes(ab_w_in, 1, 2)
    ab_w_moba_t = ab_w_in_t[:, A_COLS + GATE_COLS:, :]
    c_w_in_t = jnp.swapaxes(c_w_in, 1, 2)
    b, t, d = x.shape
    prompt = kv_pages is None
    rows_k, rows_v, cs, ns, ms, ss, gcs, fcs = [], [], [], [], [], [], [], []
    h = x.reshape(b * t, d)
    for l in range(DEPTH):
        j = l // 2
        if l % 2 == 0:
            main, gates = _norm_matmul(h, norm_mix[l], [(ab_w_in_t, A_COLS // 512), (ab_w_moba_t, B_COLS // 512)], j,
                                       (ab_w_in_t, GATE_LANES, A_COLS // GATE_LANES), tm=tm, tn=512, transposed=True)
            if prompt:
                ha, c, n, m = _mlstm_prompt(main, gates, ab_b_i[j], ab_b_f[j], ab_w_mh[j], b, t)
                q_rot, kn, vn, k_bf, v_bf, km = _moba_prep(main, t)
                kmean = km[:, 0, :].reshape(b, t // MOBA_BLOCK, H_B * HD_B)
                ob = _moba_attention_prompt(q_rot, k_bf, v_bf, kmean, b, t)
                m = m[:, :, 0]
                kn = kn.reshape(b, t, H_B, HD_B)
                vn = vn.reshape(b, t, H_B, HD_B)
                h = _matmul2_res(ha, ob, ab_w_out, j, h, tm=tm, tn=512)
            else:
                cache_k, cache_v, page_table = kv_pages
                ha, c, n, m = _mlstm_step(main, gates, ab_b_i[j], ab_b_f[j], ab_w_mh[j], c0[j], n0[j], m0[j])
                m = m[:, :H_A]
                q_rot, kn, vn = _rope_rows(main, pos0)
                sel = _moba_scan(q_rot, cache_k, j, page_table)[:, :, :MOBA_TOPK]
                page_idx = sel[..., None] * PAGES_PER_BLOCK + jnp.arange(PAGES_PER_BLOCK, dtype=jnp.int32)
                pages = jnp.take_along_axis(page_table[:, None, :], page_idx.reshape(b, H_B, SEL_PAGES), axis=2)
                ob = _moba_decode(q_rot, kn, vn, cache_k, cache_v, j, pages.reshape(-1))
                kn = kn.reshape(b, t, H_B, HD_B)
                vn = vn.reshape(b, t, H_B, HD_B)
                h = _matmul2_res(ha, ob, ab_w_out, j, h, tm=tm, tn=512)
            rows_k.append(kn)
            rows_v.append(vn)
            cs.append(c)
            ns.append(n)
            ms.append(m)
        else:
            n_main = CONV_DIM_C + H_V_C * DV_C
            main, tail = _norm_matmul(h, norm_mix[l], [(c_w_in_t, n_main // 1024)], j,
                                      (c_w_in_t, 2 * H_V_C, n_main // (2 * H_V_C)), tm=tm, tn=1024, transposed=True)
            if prompt:
                o, s = _gdn_prompt(main, tail, c_conv_w[j], c_a_log[j], c_dt_bias[j], c_w_norm[j], b, t)
                gc = main.reshape(b, t, n_main)[:, t - (CONV_C - 1):, :CONV_DIM_C]
            else:
                o, s, gc = _gdn_step(main, tail, gconv0[j], c_conv_w[j], c_a_log[j], c_dt_bias[j], c_w_norm[j], s0[j])
            ss.append(s)
            gcs.append(gc)
            h = _matmul_res(o, c_w_out, j, h, tm=tm, tn=512)
        if prompt:
            act, lastg, lastu = _ffn_up(h, norm_ffn[l], ffn_w_up, l, ffn_conv_w[l], ffn_conv_b[l], t, tm=tm, tn=512)
            tps = t // tm
            last = jnp.concatenate([lastg[tps - 1::tps, 8 - (FFN_CONV - 1):], lastu[tps - 1::tps, 8 - (FFN_CONV - 1):]],
                                   axis=-1)
            fcs.append(last)
        else:
            n_tiles = (2 * D_FF) // 512
            n_tail = 2 * D_FF - n_tiles * 512
            main, tail = _norm_matmul(h, norm_ffn[l], [(ffn_w_up, n_tiles)], l,
                                      (ffn_w_up, n_tail, n_tiles * 512 // n_tail), tm=tm, tn=512)
            act, fc = _ffn_mid_step(jnp.concatenate([main, tail], axis=-1), fconv0[l], ffn_conv_w[l], ffn_conv_b[l])
            fcs.append(fc)
        h = _matmul_res(act, ffn_w_down, l, h, tm=tm, tn=256)
    y = _final_norm(h, norm_final, tm=tm).reshape(b, t, d)
    return (y, jnp.stack(rows_k), jnp.stack(rows_v), jnp.stack(cs), jnp.stack(ns), jnp.stack(ms),
            jnp.stack(ss), jnp.stack(gcs), jnp.stack(fcs))


def kernel(x_prompt, x_sample, cache_k, cache_v, page_table, state_mlstm_c, state_mlstm_n, state_mlstm_m,
           state_gdn_s, state_gdn_conv, state_ffn_conv, norm_mix, norm_ffn, norm_final,
           ab_w_in, ab_b_i, ab_b_f, ab_w_mh, ab_w_out, c_w_in, c_conv_w, c_a_log, c_dt_bias, c_w_norm, c_w_out,
           ffn_w_up, ffn_conv_w, ffn_conv_b, ffn_w_down):
    params = (norm_mix, norm_ffn, norm_final, ab_w_in, ab_b_i, ab_b_f, ab_w_mh, ab_w_out,
              c_w_in, c_conv_w, c_a_log, c_dt_bias, c_w_norm, c_w_out,
              ffn_w_up, ffn_conv_w, ffn_conv_b, ffn_w_down)
    batch = x_prompt.shape[0]
    n_ab, n_c = ab_w_in.shape[0], c_w_in.shape[0]
    zc = jnp.zeros((n_ab, batch, H_A, DK_A, DV_A), F32)
    zn = jnp.zeros((n_ab, batch, H_A, DK_A), F32)
    zm = jnp.zeros((n_ab, batch, H_A), F32)
    zs = jnp.zeros((n_c, batch, H_V_C, DK_C, DV_C), F32)
    zg = jnp.zeros((n_c, batch, CONV_C - 1, CONV_DIM_C), F32)
    outs_p = _trunk(x_prompt, 0, zc, zn, zm, zs, zg, None, None, params, tm=1024)
    past_len = page_table.shape[1] * PAGE_SIZE
    outs_s = _trunk(x_sample, past_len, state_mlstm_c, state_mlstm_n, state_mlstm_m,
                    state_gdn_s, state_gdn_conv, state_ffn_conv, (cache_k, cache_v, page_table), params,
                    tm=x_sample.shape[0] * x_sample.shape[1])
    return (outs_p[0], outs_s[0]) + tuple(outs_p[1:]) + tuple(outs_s[1:])


def _rope_tables(pos):
    half = ROPE_DIMS // 2
    inv = ROPE_THETA ** (-jnp.arange(half, dtype=F32) / half)
    ang = pos.astype(F32)[:, None] * inv[None, :]
    cos, sin = jnp.cos(ang), jnp.sin(ang)
    t = pos.shape[0]
    cosf = jnp.concatenate([cos, cos, jnp.ones((t, HD_B - ROPE_DIMS), F32)], axis=-1)
    sina = jnp.concatenate([-sin, jnp.zeros((t, HD_B - half), F32)], axis=-1)
    sinb = jnp.concatenate([jnp.zeros((t, half), F32), sin, jnp.zeros((t, HD_B - ROPE_DIMS), F32)], axis=-1)
    return cosf, sina, sinb


def _rope_head(x, cosf, sina, sinb):
    half = ROPE_DIMS // 2
    return x * cosf + pltpu.roll(x, HD_B - half, 1) * sina + pltpu.roll(x, half, 1) * sinb


def _moba_prep_kernel(q_ref, k_ref, v_ref, cos_ref, sina_ref, sinb_ref,
                      qr_ref, kr_ref, vr_ref, kb_ref, vb_ref, km_ref):
    cosf, sina, sinb = cos_ref[...], sina_ref[...], sinb_ref[...]
    for h in range(H_B):
        sl = slice(h * HD_B, (h + 1) * HD_B)
        qr_ref[:, sl] = _rope_head(q_ref[:, sl], cosf, sina, sinb)
        kr = _rope_head(k_ref[:, sl], cosf, sina, sinb)
        kr_ref[:, sl] = kr
        kb_ref[:, sl] = kr.astype(BF16)
        km_ref[:, sl] = jnp.broadcast_to(jnp.mean(kr, axis=0, keepdims=True), (8, HD_B))
    v = v_ref[...]
    vr_ref[...] = v
    vb_ref[...] = v.astype(BF16)


def _moba_prep(main, seq_len):
    m = main.shape[0]
    hw = H_B * HD_B
    nblk = seq_len // MOBA_BLOCK
    cosf, sina, sinb = _rope_tables(jnp.arange(seq_len, dtype=jnp.int32))
    c0 = A_COLS // hw
    row = pl.BlockSpec((MOBA_BLOCK, hw), lambda i: (i, 0))
    tab = pl.BlockSpec((MOBA_BLOCK, HD_B), lambda i: (i % nblk, 0))
    return pl.pallas_call(
        _moba_prep_kernel,
        grid=(m // MOBA_BLOCK,),
        in_specs=[pl.BlockSpec((MOBA_BLOCK, hw), lambda i: (i, c0)),
                  pl.BlockSpec((MOBA_BLOCK, hw), lambda i: (i, c0 + 1)),
                  pl.BlockSpec((MOBA_BLOCK, hw), lambda i: (i, c0 + 2)),
                  tab, tab, tab],
        out_specs=[row, row, row, row, row, pl.BlockSpec((None, 8, hw), lambda i: (i, 0, 0))],
        out_shape=[jax.ShapeDtypeStruct((m, hw), F32), jax.ShapeDtypeStruct((m, hw), F32),
                   jax.ShapeDtypeStruct((m, hw), F32), jax.ShapeDtypeStruct((m, hw), BF16),
                   jax.ShapeDtypeStruct((m, hw), BF16), jax.ShapeDtypeStruct((m // MOBA_BLOCK, 8, hw), F32)],
        compiler_params=_params("parallel"),
        name="moba_prep",
    )(main, main, main, cosf, sina, sinb)


MOBA_TRIP = 4


def _moba_kernel(q_ref, k_ref, v_ref, km_ref, o_ref, *, nblk):
    assert nblk % MOBA_TRIP == 0
    qi = pl.program_id(2)
    q = q_ref[...]
    gate = lax.dot_general(km_ref[...], q, (((1,), (1,)), ((), ())),
                           precision=lax.Precision.HIGHEST, preferred_element_type=F32)
    blk_t = lax.broadcasted_iota(jnp.int32, gate.shape, 0)
    blkf = blk_t.astype(F32)
    past = blk_t < qi
    g = jnp.where(past, gate, NEG)
    sel_t = jnp.zeros(gate.shape, F32)
    for _ in range(min(MOBA_TOPK, nblk)):
        mx = jnp.max(g, axis=0, keepdims=True)
        hit = blkf == jnp.min(jnp.where(g == mx, blkf, float(nblk)), axis=0, keepdims=True)
        sel_t = jnp.where(hit, 1.0, sel_t)
        g = jnp.where(hit, -jnp.inf, g)
    sel_t = jnp.where(past, sel_t, 0.0)
    sel = jnp.concatenate([sel_t, jnp.zeros((LANES - nblk, MOBA_BLOCK), F32)], axis=0).T
    blk = lax.broadcasted_iota(jnp.int32, sel.shape, 1)

    qb = q.astype(BF16)
    scale = HD_B ** -0.5

    def blocks(j, n):
        rows = pl.ds(pl.multiple_of(j * MOBA_BLOCK, MOBA_BLOCK), n * MOBA_BLOCK)
        s = lax.dot_general(qb, k_ref[rows, :], (((1,), (1,)), ((), ())), preferred_element_type=F32) * scale
        return s, v_ref[rows, :]

    s, vj = blocks(qi, 1)
    r = lax.broadcasted_iota(jnp.int32, s.shape, 0)
    c = lax.broadcasted_iota(jnp.int32, s.shape, 1)
    s = jnp.where(c <= r, s, NEG)
    m0 = jnp.max(s, axis=1, keepdims=True)
    p = jnp.exp(s - m0)
    l0 = jnp.sum(p, axis=1, keepdims=True)
    acc0 = _bdot(p.astype(BF16), vj)
    sub = lax.broadcasted_iota(jnp.int32, (MOBA_BLOCK, MOBA_TRIP * MOBA_BLOCK), 1) // MOBA_BLOCK

    def body(jj, carry):
        m, l, acc = carry
        keep = jnp.zeros(sub.shape, F32)
        for i in range(MOBA_TRIP):
            sel_i = jnp.sum(jnp.where(blk == MOBA_TRIP * jj + i, sel, 0.0), axis=1, keepdims=True)
            keep = jnp.where(sub == i, sel_i, keep)
        s, vj = blocks(MOBA_TRIP * jj, MOBA_TRIP)
        s = jnp.where(keep > 0.5, s, NEG)
        m_new = jnp.maximum(m, jnp.max(s, axis=1, keepdims=True))
        alpha = jnp.exp(m - m_new)
        p = jnp.exp(s - m_new)
        return m_new, alpha * l + jnp.sum(p, axis=1, keepdims=True), alpha * acc + _bdot(p.astype(BF16), vj)

    m, l, acc = lax.fori_loop(0, (qi + MOBA_TRIP - 1) // MOBA_TRIP, body, (m0, l0, acc0))
    o_ref[...] = (acc / l).astype(BF16)


def _moba_attention_prompt(q_rot, k_bf, v_bf, kmean, n_seq, seq_len):
    m = q_rot.shape[0]
    nblk = seq_len // MOBA_BLOCK
    return pl.pallas_call(
        functools.partial(_moba_kernel, nblk=nblk),
        grid=(n_seq, H_B, nblk),
        in_specs=[pl.BlockSpec((MOBA_BLOCK, HD_B), lambda b, h, i: (b * nblk + i, h)),
                  pl.BlockSpec((seq_len, HD_B), lambda b, h, i: (b, h)),
                  pl.BlockSpec((seq_len, HD_B), lambda b, h, i: (b, h)),
                  pl.BlockSpec((None, nblk, HD_B), lambda b, h, i: (b, 0, h))],
        out_specs=pl.BlockSpec((MOBA_BLOCK, HD_B), lambda b, h, i: (b * nblk + i, h)),
        out_shape=jax.ShapeDtypeStruct((m, H_B * HD_B), BF16),
        compiler_params=_params("parallel", "parallel", "arbitrary"),
        name="moba_attention",
    )(q_rot, k_bf, v_bf, kmean)


MLSTM_ROWS = 128
GATE_LANES = 128


def _mlstm_kernel(q_ref, k_ref, v_ref, og_ref, g_ref, bias_ref, wmh_ref, h_ref, c_ref, n_ref, m_ref,
                  cs_ref, ns_ref, ms_ref):
    r = pl.program_id(1)
    L = q_ref.shape[0]

    @pl.when(r == 0)
    def _():
        cs_ref[...] = jnp.zeros(cs_ref.shape, F32)
        ns_ref[...] = jnp.zeros(ns_ref.shape, F32)
        ms_ref[...] = jnp.zeros(ms_ref.shape, F32)

    gts = g_ref[...] + bias_ref[...]
    lane = lax.broadcasted_iota(jnp.int32, gts.shape, 1)
    lg = jnp.where(lane < H_A, gts, -_softplus(-gts))
    ti = lax.broadcasted_iota(jnp.int32, (L, L), 0)
    si = lax.broadcasted_iota(jnp.int32, (L, L), 1)
    incl = si <= ti
    bmat = jnp.dot(jnp.where(incl, 1.0, 0.0), lg, precision=lax.Precision.HIGHEST, preferred_element_type=F32)
    lgt, bt = lg.T, bmat.T
    nt = (((1,), (1,)), ((), ()))
    tn = (((0,), (0,)), ((), ()))

    for h in range(H_A):
        bcol, brow = bmat[:, H_A + h:H_A + h + 1], bt[H_A + h:H_A + h + 1, :]
        licol, lirow = lg[:, h:h + 1], lgt[h:h + 1, :]
        m_prev = ms_ref[h:h + 1, 0:1]
        d = jnp.where(incl, bcol - brow + lirow, NEG)
        inter = bcol + m_prev
        mt = jnp.maximum(inter, jnp.max(d, axis=1, keepdims=True))
        qh = q_ref[:, h * DK_A:(h + 1) * DK_A]
        kh = k_ref[:, h * DK_A:(h + 1) * DK_A] * (DK_A ** -0.5)
        vb = v_ref[:, h * DV_A:(h + 1) * DV_A].astype(BF16)
        qb = qh.astype(BF16)
        s = lax.dot_general(qb, kh.astype(BF16), nt, preferred_element_type=F32) * jnp.exp(d - mt)
        wi = jnp.exp(inter - mt)
        c_old = cs_ref[h]
        n_old = ns_ref[h:h + 1, :]
        num = wi * _bdot(qb, c_old.astype(BF16)) + _bdot(s.astype(BF16), vb)
        den = wi * jnp.sum(qh * n_old, axis=1, keepdims=True) + jnp.sum(s, axis=1, keepdims=True)
        hh = num / jnp.maximum(jnp.abs(den), jnp.exp(-mt))
        hs = slice(h * DV_A, (h + 1) * DV_A)
        og = og_ref[:, hs]
        h_ref[:, hs] = (_rms(hh, wmh_ref[:, hs]) / (1.0 + jnp.exp(-og))).astype(BF16)
        b_last = bcol[L - 1:L]
        gcol = b_last - bcol + licol
        m_new = jnp.maximum(b_last + m_prev, jnp.max(gcol, axis=0, keepdims=True))
        wc = jnp.exp(b_last + m_prev - m_new)
        kws = kh * jnp.exp(gcol - m_new)
        cs_ref[h] = wc * c_old + lax.dot_general(kws.astype(BF16), vb, tn, preferred_element_type=F32)
        ns_ref[h:h + 1, :] = wc * n_old + jnp.sum(kws, axis=0, keepdims=True)
        ms_ref[h:h + 1, :] = jnp.broadcast_to(m_new, (1, ms_ref.shape[1]))

    @pl.when(r == pl.num_programs(1) - 1)
    def _():
        c_ref[...] = cs_ref[...]
        n_ref[...] = ns_ref[...]
        m_ref[...] = ms_ref[...]


def _mlstm_prompt(main, gates, b_i, b_f, w_mh, n_seq, seq_len):
    m = main.shape[0]
    rows = MLSTM_ROWS
    nr = seq_len // rows
    qkw, vw = H_A * DK_A, H_A * DV_A
    bias = jnp.concatenate([b_i, b_f, jnp.zeros((GATE_LANES - 2 * H_A,), F32)]).reshape(1, GATE_LANES)
    return pl.pallas_call(
        _mlstm_kernel,
        grid=(n_seq, nr),
        in_specs=[pl.BlockSpec((rows, qkw), lambda b, r: (b * nr + r, 0)),
                  pl.BlockSpec((rows, qkw), lambda b, r: (b * nr + r, 1)),
                  pl.BlockSpec((rows, vw), lambda b, r: (b * nr + r, 1)),
                  pl.BlockSpec((rows, vw), lambda b, r: (b * nr + r, 2)),
                  pl.BlockSpec((rows, GATE_LANES), lambda b, r: (b * nr + r, 0)),
                  pl.BlockSpec((1, GATE_LANES), lambda b, r: (0, 0)),
                  pl.BlockSpec((1, vw), lambda b, r: (0, 0))],
        out_specs=[pl.BlockSpec((rows, vw), lambda b, r: (b * nr + r, 0)),
                   pl.BlockSpec((None, H_A, DK_A, DV_A), lambda b, r: (b, 0, 0, 0)),
                   pl.BlockSpec((None, H_A, DK_A), lambda b, r: (b, 0, 0)),
                   pl.BlockSpec((None, H_A, GATE_LANES), lambda b, r: (b, 0, 0))],
        out_shape=[jax.ShapeDtypeStruct((m, vw), BF16),
                   jax.ShapeDtypeStruct((n_seq, H_A, DK_A, DV_A), F32),
                   jax.ShapeDtypeStruct((n_seq, H_A, DK_A), F32),
                   jax.ShapeDtypeStruct((n_seq, H_A, GATE_LANES), F32)],
        scratch_shapes=[pltpu.VMEM((H_A, DK_A, DV_A), F32), pltpu.VMEM((H_A, DK_A), F32),
                        pltpu.VMEM((H_A, GATE_LANES), F32)],
        compiler_params=_params("parallel", "arbitrary"),
        name="mlstm_prompt",
    )(main, main, main, main, gates, bias, w_mh.reshape(1, vw))


GDN_ROWS = 256
GDN_REP = H_V_C // H_QK_C
GDN_QK = 8
GDN_HEADS = GDN_QK * GDN_REP
CONV_HALO = 8


def _silu(x):
    return x / (1.0 + jnp.exp(-x))


def _softplus(x):
    return jnp.maximum(x, 0.0) + jnp.log1p(jnp.exp(-jnp.abs(x)))


def _conv_silu(halo, x, w):
    ext = jnp.concatenate([halo, x], axis=0)
    y = pltpu.roll(ext, CONV_C - 1, 0) * w[0:1]
    for i in range(1, CONV_C - 1):
        y = y + pltpu.roll(ext, CONV_C - 1 - i, 0) * w[i:i + 1]
    y = y + ext * w[CONV_C - 1:CONV_C]
    return _silu(y[halo.shape[0]:])


def _gdn_kernel(qh_ref, kh_ref, vh_ref, q_ref, k_ref, v_ref, z_ref, t_ref, cwq_ref, cwk_ref, cwv_ref,
                al_ref, dt_ref, wn_ref, o_ref, s_ref, st_ref):
    j = pl.program_id(1)
    r = pl.program_id(2)
    L = CHUNK
    rows = q_ref.shape[0]

    @pl.when(r == 0)
    def _():
        st_ref[...] = jnp.zeros(st_ref.shape, F32)

    fresh = r == 0

    def halo(ref):
        return jnp.where(fresh, 0.0, ref[...])

    qs = _conv_silu(halo(qh_ref), q_ref[...], cwq_ref[...])
    ks = _conv_silu(halo(kh_ref), k_ref[...], cwk_ref[...])
    vs = _conv_silu(halo(vh_ref), v_ref[...], cwv_ref[...])
    q, k = [], []
    for a in range(GDN_QK):
        qa, ka = qs[:, a * DK_C:(a + 1) * DK_C], ks[:, a * DK_C:(a + 1) * DK_C]
        q.append(qa * lax.rsqrt(jnp.sum(qa * qa, axis=-1, keepdims=True) + EPS) * (DK_C ** -0.5))
        k.append(ka * lax.rsqrt(jnp.sum(ka * ka, axis=-1, keepdims=True) + EPS))

    t = t_ref[...]
    lane = lax.broadcasted_iota(jnp.int32, t.shape, 1)
    y = jnp.where(lane < H_V_C, 1.0 / (1.0 + jnp.exp(-t)), -jnp.exp(al_ref[...]) * _softplus(t + dt_ref[...]))
    lane128 = lax.broadcasted_iota(jnp.int32, (rows, 128), 1)
    cols = jnp.zeros((rows, 128), F32)
    betas = []
    for e in range(GDN_HEADS):
        hv = j * GDN_HEADS + e
        betas.append(jnp.sum(jnp.where(lane == hv, y, 0.0), axis=1, keepdims=True))
        g_e = jnp.sum(jnp.where(lane == H_V_C + hv, y, 0.0), axis=1, keepdims=True)
        hi = g_e.astype(BF16).astype(F32)
        mid = (g_e - hi).astype(BF16).astype(F32)
        for piece, val in enumerate((hi, mid, g_e - hi - mid)):
            cols = jnp.where(lane128 == piece * GDN_HEADS + e, val, cols)
    ri = lax.broadcasted_iota(jnp.int32, (rows, rows), 0)
    ci = lax.broadcasted_iota(jnp.int32, (rows, rows), 1)
    cum = jnp.where((ri // L == ci // L) & (ci <= ri), 1.0, 0.0).astype(BF16)
    pieces = _bdot(cum, cols.astype(BF16))
    dcols = pieces + pltpu.roll(pieces, 128 - GDN_HEADS, 1) + pltpu.roll(pieces, 128 - 2 * GDN_HEADS, 1)
    drows = dcols.T

    ti = lax.broadcasted_iota(jnp.int32, (L, L), 0)
    si = lax.broadcasted_iota(jnp.int32, (L, L), 1)
    strict, incl = si < ti, si <= ti
    eye = jnp.where(si == ti, 1.0, 0.0)
    wn = wn_ref[...]
    nt = (((1,), (1,)), ((), ()))
    tn = (((0,), (0,)), ((), ()))
    n_chunks = rows // L
    assert L == 64
    units = [(c, e) for c in range(n_chunks) for e in range(GDN_HEADS)]

    def mm(a, b):
        return _bdot(a.astype(BF16), b.astype(BF16))

    kk, qk = {}, {}
    for c in range(n_chunks):
        for a in range(GDN_QK):
            kcb = k[a][c * L:(c + 1) * L].astype(BF16)
            kk[c, a] = lax.dot_general(kcb, kcb, nt, preferred_element_type=F32)
            qk[c, a] = lax.dot_general(q[a][c * L:(c + 1) * L].astype(BF16), kcb, nt, preferred_element_type=F32)
    dcol, dlast, bcol, attn, p1 = {}, {}, {}, {}, {}
    for c, e in units:
        cs = slice(c * L, (c + 1) * L)
        dcol[c, e] = dcols[cs, e:e + 1]
        dlast[c, e] = dcols[(c + 1) * L - 1:(c + 1) * L, e:e + 1]
        bcol[c, e] = betas[e][cs]
        ex = jnp.exp(jnp.where(incl, dcol[c, e] - drows[e:e + 1, cs], 0.0))
        p1[c, e] = jnp.where(strict, -(bcol[c, e] * kk[c, e // GDN_REP] * ex), 0.0)
        attn[c, e] = jnp.where(incl, qk[c, e // GDN_REP] * ex, 0.0).astype(BF16)
    p2 = {u_: mm(p1[u_], p1[u_]) for u_ in units}
    p4 = {u_: mm(p2[u_], p2[u_]) for u_ in units}
    a0 = {u_: eye + p1[u_] + p2[u_] + mm(p1[u_], p2[u_]) for u_ in units}
    p8 = {u_: mm(p4[u_], p4[u_]) for u_ in units}
    p16 = {u_: mm(p8[u_], p8[u_]) for u_ in units}
    a1 = {u_: eye + p4[u_] + p8[u_] + mm(p4[u_], p8[u_]) for u_ in units}
    p32 = {u_: mm(p16[u_], p16[u_]) for u_ in units}
    a01 = {u_: mm(a0[u_], a1[u_]) for u_ in units}
    a2 = {u_: eye + p16[u_] + p32[u_] + mm(p16[u_], p32[u_]) for u_ in units}
    tinv = {u_: mm(a01[u_], a2[u_]) for u_ in units}
    sol, qd = {}, {}
    for c, e in units:
        cs = slice(c * L, (c + 1) * L)
        edc = jnp.exp(dcol[c, e])
        kc = k[e // GDN_REP][cs]
        rhs = jnp.concatenate([vs[cs, e * DV_C:(e + 1) * DV_C] * bcol[c, e], kc * (bcol[c, e] * edc)], axis=1)
        sol[c, e] = mm(tinv[c, e], rhs)
        qd[c, e] = q[e // GDN_REP][cs] * edc
    o_const, q_eff, s_mat, s_add = {}, {}, {}, {}
    for c, e in units:
        solb = sol[c, e].astype(BF16)
        au = _bdot(attn[c, e], solb)
        o_const[c, e] = au[:, :DV_C]
        q_eff[c, e] = (qd[c, e] - au[:, DV_C:]).astype(BF16)
        kd = (k[e // GDN_REP][c * L:(c + 1) * L] * jnp.exp(dlast[c, e] - dcol[c, e])).astype(BF16)
        ksol = lax.dot_general(kd, solb, tn, preferred_element_type=F32)
        s_add[c, e] = ksol[:, :DV_C]
        s_mat[c, e] = ksol[:, DV_C:].astype(BF16)

    s = [st_ref[e] for e in range(GDN_HEADS)]
    for c in range(n_chunks):
        for e in range(GDN_HEADS):
            sb = s[e].astype(BF16)
            o = _bdot(q_eff[c, e], sb) + o_const[c, e]
            s[e] = jnp.exp(dlast[c, e]) * s[e] - _bdot(s_mat[c, e], sb) + s_add[c, e]
            zc = z_ref[c * L:(c + 1) * L, e * DV_C:(e + 1) * DV_C]
            o_ref[c * L:(c + 1) * L, e * DV_C:(e + 1) * DV_C] = (_rms(o, wn) * _silu(zc)).astype(BF16)
    for e in range(GDN_HEADS):
        st_ref[e] = s[e]

    @pl.when(r == pl.num_programs(2) - 1)
    def _():
        s_ref[...] = st_ref[...]


def _gdn_prompt(main, tail, conv_w, a_log, dt_bias, w_norm, n_seq, seq_len):
    m = main.shape[0]
    rows = GDN_ROWS
    nr = seq_len // rows
    hb = rows // CONV_HALO
    qkw, vw = GDN_QK * DK_C, GDN_HEADS * DV_C
    k0 = H_QK_C * DK_C // qkw
    v0 = 2 * H_QK_C * DK_C // vw
    z0 = CONV_DIM_C // vw
    zeros = jnp.zeros((H_V_C,), F32)
    al = jnp.concatenate([zeros, a_log]).reshape(1, 2 * H_V_C)
    dt = jnp.concatenate([zeros, dt_bias]).reshape(1, 2 * H_V_C)

    def row_idx(b, r):
        return b * nr + r

    def halo_idx(b, r):
        return jnp.maximum(row_idx(b, r) * hb - 1, 0)

    return pl.pallas_call(
        _gdn_kernel,
        grid=(n_seq, H_QK_C // GDN_QK, nr),
        in_specs=[pl.BlockSpec((CONV_HALO, qkw), lambda b, j, r: (halo_idx(b, r), j)),
                  pl.BlockSpec((CONV_HALO, qkw), lambda b, j, r: (halo_idx(b, r), k0 + j)),
                  pl.BlockSpec((CONV_HALO, vw), lambda b, j, r: (halo_idx(b, r), v0 + j)),
                  pl.BlockSpec((rows, qkw), lambda b, j, r: (row_idx(b, r), j)),
                  pl.BlockSpec((rows, qkw), lambda b, j, r: (row_idx(b, r), k0 + j)),
                  pl.BlockSpec((rows, vw), lambda b, j, r: (row_idx(b, r), v0 + j)),
                  pl.BlockSpec((rows, vw), lambda b, j, r: (row_idx(b, r), z0 + j)),
                  pl.BlockSpec((rows, 2 * H_V_C), lambda b, j, r: (row_idx(b, r), 0)),
                  pl.BlockSpec((CONV_C, qkw), lambda b, j, r: (0, j)),
                  pl.BlockSpec((CONV_C, qkw), lambda b, j, r: (0, k0 + j)),
                  pl.BlockSpec((CONV_C, vw), lambda b, j, r: (0, v0 + j)),
                  pl.BlockSpec((1, 2 * H_V_C), lambda b, j, r: (0, 0)),
                  pl.BlockSpec((1, 2 * H_V_C), lambda b, j, r: (0, 0)),
                  pl.BlockSpec((1, DV_C), lambda b, j, r: (0, 0))],
        out_specs=[pl.BlockSpec((rows, vw), lambda b, j, r: (row_idx(b, r), j)),
                   pl.BlockSpec((None, GDN_HEADS, DK_C, DV_C), lambda b, j, r: (b, j, 0, 0))],
        out_shape=[jax.ShapeDtypeStruct((m, H_V_C * DV_C), BF16),
                   jax.ShapeDtypeStruct((n_seq, H_V_C, DK_C, DV_C), F32)],
        scratch_shapes=[pltpu.VMEM((GDN_HEADS, DK_C, DV_C), F32)],
        compiler_params=_params("parallel", "parallel", "arbitrary"),
        name="gdn_prompt",
    )(main, main, main, main, main, main, main, tail, conv_w, conv_w, conv_w, al, dt, w_norm.reshape(1, DV_C))


def _to_col(row):
    n = row.shape[1]
    i = lax.broadcasted_iota(jnp.int32, (n, n), 0)
    j = lax.broadcasted_iota(jnp.int32, (n, n), 1)
    return jnp.sum(jnp.where(i == j, jnp.broadcast_to(row, (n, n)), 0.0), axis=1, keepdims=True)


def _sigmoid(x):
    return 1.0 / (1.0 + jnp.exp(-x))


def _mlstm_step_kernel(q_ref, k_ref, v_ref, og_ref, g_ref, bias_ref, wmh_ref, c0_ref, n0_ref, m0_ref,
                       h_ref, c_ref, n_ref, m_ref):
    nb = q_ref.shape[0]
    gts = g_ref[...] + bias_ref[...]
    lane = lax.broadcasted_iota(jnp.int32, gts.shape, 1)
    m_out = jnp.zeros(gts.shape, F32)
    for h in range(H_A):
        li = gts[:, h:h + 1]
        lf = -_softplus(-gts[:, H_A + h:H_A + h + 1])
        m_prev = m0_ref[:, h:h + 1]
        inter = lf + m_prev
        mt = jnp.maximum(inter, li)
        qh = q_ref[:, h * DK_A:(h + 1) * DK_A]
        kh = k_ref[:, h * DK_A:(h + 1) * DK_A] * (DK_A ** -0.5)
        hs = slice(h * DV_A, (h + 1) * DV_A)
        vh = v_ref[:, hs]
        s = jnp.sum(qh * kh, axis=1, keepdims=True) * jnp.exp(li - mt)
        wi = jnp.exp(inter - mt)
        m_new = jnp.maximum(lf + m_prev, li)
        wc = jnp.exp(lf + m_prev - m_new)
        kws = kh * jnp.exp(li - m_new)
        n_old = n0_ref[:, h, :]
        qc_rows = []
        for b in range(nb):
            c_old = c0_ref[b, h]
            qc_rows.append(jnp.sum(_to_col(qh[b:b + 1]) * c_old, axis=0, keepdims=True))
            c_ref[b, h] = wc[b:b + 1] * c_old + _to_col(kws[b:b + 1]) * vh[b:b + 1]
        num = wi * jnp.concatenate(qc_rows, axis=0) + s * vh
        den = wi * jnp.sum(qh * n_old, axis=1, keepdims=True) + s
        hh = num / jnp.maximum(jnp.abs(den), jnp.exp(-mt))
        h_ref[:, hs] = _rms(hh, wmh_ref[:, hs]) * _sigmoid(og_ref[:, hs])
        n_ref[:, h, :] = wc * n_old + kws
        m_out = jnp.where(lane == h, m_new, m_out)
    m_ref[...] = m_out


def _mlstm_step(main, gates, b_i, b_f, w_mh, c0, n0, m0):
    nb = main.shape[0]
    qkw, vw = H_A * DK_A, H_A * DV_A
    bias = jnp.concatenate([b_i, b_f, jnp.zeros((GATE_LANES - 2 * H_A,), F32)]).reshape(1, GATE_LANES)
    full = lambda shape: pl.BlockSpec(shape, lambda i: (0,) * len(shape))
    return pl.pallas_call(
        _mlstm_step_kernel,
        grid=(1,),
        in_specs=[pl.BlockSpec((nb, qkw), lambda i: (0, 0)),
                  pl.BlockSpec((nb, qkw), lambda i: (0, 1)),
                  pl.BlockSpec((nb, vw), lambda i: (0, 1)),
                  pl.BlockSpec((nb, vw), lambda i: (0, 2)),
                  full((nb, GATE_LANES)), full((1, GATE_LANES)), full((1, vw)),
                  full((nb, H_A, DK_A, DV_A)), full((nb, H_A, DK_A)), full((nb, H_A))],
        out_specs=[full((nb, vw)), full((nb, H_A, DK_A, DV_A)), full((nb, H_A, DK_A)), full((nb, GATE_LANES))],
        out_shape=[jax.ShapeDtypeStruct((nb, vw), F32),
                   jax.ShapeDtypeStruct((nb, H_A, DK_A, DV_A), F32),
                   jax.ShapeDtypeStruct((nb, H_A, DK_A), F32),
                   jax.ShapeDtypeStruct((nb, GATE_LANES), F32)],
        compiler_params=_params("arbitrary"),
        name="mlstm_step",
    )(main, main, main, main, gates, bias, w_mh.reshape(1, vw), c0, n0, m0)


def _gdn_step_kernel(x_ref, t_ref, buf_ref, cw_ref, al_ref, dt_ref, wn_ref, s0_ref, o_ref, cn_ref, s_ref):
    x = x_ref[...]
    xc = x[:, :CONV_DIM_C]
    buf = buf_ref[...]
    cw = cw_ref[...]
    y = buf[0:1] * cw[0:1]
    for i in range(1, CONV_C - 1):
        y = y + buf[i:i + 1] * cw[i:i + 1]
    y = _silu(y + xc * cw[CONV_C - 1:CONV_C])
    cn_ref[...] = jnp.concatenate([buf[1:], xc], axis=0)
    t = t_ref[...]
    lane = lax.broadcasted_iota(jnp.int32, t.shape, 1)
    gts = jnp.where(lane < H_V_C, _sigmoid(t), -jnp.exp(al_ref[...]) * _softplus(t + dt_ref[...]))
    wn = wn_ref[...]
    k0, v0 = H_QK_C * DK_C, 2 * H_QK_C * DK_C
    for j in range(H_QK_C):
        qj = y[:, j * DK_C:(j + 1) * DK_C]
        kj = y[:, k0 + j * DK_C:k0 + (j + 1) * DK_C]
        qj = qj * lax.rsqrt(jnp.sum(qj * qj, axis=-1, keepdims=True) + EPS) * (DK_C ** -0.5)
        kj = kj * lax.rsqrt(jnp.sum(kj * kj, axis=-1, keepdims=True) + EPS)
        qcol, kcol = _to_col(qj), _to_col(kj)
        qk = jnp.sum(qj * kj, axis=1, keepdims=True)
        for e in range(GDN_REP):
            hv = j * GDN_REP + e
            hs = slice(hv * DV_C, (hv + 1) * DV_C)
            beta = gts[:, hv:hv + 1]
            eg = jnp.exp(gts[:, H_V_C + hv:H_V_C + hv + 1])
            s_old = s0_ref[hv]
            ks = jnp.sum(kcol * s_old, axis=0, keepdims=True)
            qs = jnp.sum(qcol * s_old, axis=0, keepdims=True)
            v_new = y[:, v0 + hv * DV_C:v0 + (hv + 1) * DV_C] * beta - (beta * eg) * ks
            o = eg * qs + qk * v_new
            s_ref[hv] = eg * s_old + kcol * v_new
            o_ref[:, hs] = _rms(o, wn) * _silu(x[:, CONV_DIM_C + hv * DV_C:CONV_DIM_C + (hv + 1) * DV_C])


def _gdn_step(main, tail, conv_buf, conv_w, a_log, dt_bias, w_norm, s0):
    nb, n_main = main.shape
    zeros = jnp.zeros((H_V_C,), F32)
    al = jnp.concatenate([zeros, a_log]).reshape(1, 2 * H_V_C)
    dt = jnp.concatenate([zeros, dt_bias]).reshape(1, 2 * H_V_C)
    hw = H_V_C * DV_C
    const = lambda shape: pl.BlockSpec(shape, lambda b: (0,) * len(shape))
    o, conv_new, s = pl.pallas_call(
        _gdn_step_kernel,
        grid=(nb,),
        in_specs=[pl.BlockSpec((None, 1, n_main), lambda b: (b, 0, 0)),
                  pl.BlockSpec((None, 1, 2 * H_V_C), lambda b: (b, 0, 0)),
                  pl.BlockSpec((None, CONV_C - 1, CONV_DIM_C), lambda b: (b, 0, 0)),
                  const((CONV_C, CONV_DIM_C)), const((1, 2 * H_V_C)), const((1, 2 * H_V_C)), const((1, DV_C)),
                  pl.BlockSpec((None, H_V_C, DK_C, DV_C), lambda b: (b, 0, 0, 0))],
        out_specs=[pl.BlockSpec((None, 1, hw), lambda b: (b, 0, 0)),
                   pl.BlockSpec((None, CONV_C - 1, CONV_DIM_C), lambda b: (b, 0, 0)),
                   pl.BlockSpec((None, H_V_C, DK_C, DV_C), lambda b: (b, 0, 0, 0))],
        out_shape=[jax.ShapeDtypeStruct((nb, 1, hw), F32),
                   jax.ShapeDtypeStruct((nb, CONV_C - 1, CONV_DIM_C), F32),
                   jax.ShapeDtypeStruct((nb, H_V_C, DK_C, DV_C), F32)],
        compiler_params=_params("parallel"),
        name="gdn_step",
    )(main.reshape(nb, 1, n_main), tail.reshape(nb, 1, 2 * H_V_C), conv_buf, conv_w, al, dt,
      w_norm.reshape(1, DV_C), s0)
    return o.reshape(nb, hw), s, conv_new


SCAN_PAGES = 8
PAGES_PER_BLOCK = MOBA_BLOCK // PAGE_SIZE


def _rope_rows_kernel(q_ref, k_ref, v_ref, cos_ref, sina_ref, sinb_ref, qr_ref, kr_ref, vr_ref):
    cosf, sina, sinb = cos_ref[...], sina_ref[...], sinb_ref[...]
    for h in range(H_B):
        sl = slice(h * HD_B, (h + 1) * HD_B)
        qr_ref[:, sl] = _rope_head(q_ref[:, sl], cosf, sina, sinb)
        kr_ref[:, sl] = _rope_head(k_ref[:, sl], cosf, sina, sinb)
    vr_ref[...] = v_ref[...]


def _rope_rows(main, pos):
    nb = main.shape[0]
    hw = H_B * HD_B
    c0 = A_COLS // hw
    cosf, sina, sinb = _rope_tables(jnp.full((1,), pos, jnp.int32))
    tab = pl.BlockSpec((1, HD_B), lambda i: (0, 0))
    out = pl.BlockSpec((nb, hw), lambda i: (0, 0))
    return pl.pallas_call(
        _rope_rows_kernel,
        grid=(1,),
        in_specs=[pl.BlockSpec((nb, hw), lambda i: (0, c0)),
                  pl.BlockSpec((nb, hw), lambda i: (0, c0 + 1)),
                  pl.BlockSpec((nb, hw), lambda i: (0, c0 + 2)), tab, tab, tab],
        out_specs=[out, out, out],
        out_shape=[jax.ShapeDtypeStruct((nb, hw), F32)] * 3,
        compiler_params=_params("arbitrary"),
        name="rope_rows",
    )(main, main, main, cosf, sina, sinb)


def _moba_scan_kernel(pt_ref, q_ref, *refs):
    pages, sel_ref, g_ref = refs[:SCAN_PAGES], refs[SCAN_PAGES], refs[SCAN_PAGES + 1]
    gi = pl.program_id(1)

    @pl.when(gi == 0)
    def _():
        g_ref[...] = jnp.full(g_ref.shape, NEG, F32)

    q = q_ref[...]
    lane = lax.broadcasted_iota(jnp.int32, g_ref.shape, 1)
    g = g_ref[...]
    for p in range(SCAN_PAGES // PAGES_PER_BLOCK):
        ksum = jnp.sum(pages[PAGES_PER_BLOCK * p][...], axis=0)
        for i in range(1, PAGES_PER_BLOCK):
            ksum = ksum + jnp.sum(pages[PAGES_PER_BLOCK * p + i][...], axis=0)
        val = jnp.sum(q * (ksum * (1.0 / MOBA_BLOCK)), axis=1, keepdims=True)
        g = jnp.where(lane == gi * (SCAN_PAGES // PAGES_PER_BLOCK) + p, val, g)
    g_ref[...] = g

    @pl.when(gi == pl.num_programs(1) - 1)
    def _():
        lanef = lane.astype(F32)
        gg = g
        out = jnp.zeros(g.shape, F32)
        for r in range(MOBA_TOPK):
            mx = jnp.max(gg, axis=1, keepdims=True)
            idx = jnp.min(jnp.where(gg == mx, lanef, float(g.shape[1])), axis=1, keepdims=True)
            out = jnp.where(lane == r, idx, out)
            gg = jnp.where(lanef == idx, -jnp.inf, gg)
        sel_ref[...] = out.astype(jnp.int32)


def _moba_scan(q_rot, cache_k, layer, page_table):
    nb = q_rot.shape[0]
    hw = H_B * HD_B
    n_pages = page_table.shape[1]
    assert n_pages % SCAN_PAGES == 0 and n_pages // PAGES_PER_BLOCK <= 128
    def page_spec(i):
        return pl.BlockSpec((None, None, PAGE_SIZE, H_B, HD_B),
                            lambda b, g, pt: (layer, pt[b * n_pages + g * SCAN_PAGES + i], 0, 0, 0))

    grid_spec = pltpu.PrefetchScalarGridSpec(
        num_scalar_prefetch=1,
        grid=(nb, n_pages // SCAN_PAGES),
        in_specs=[pl.BlockSpec((None, H_B, HD_B), lambda b, g, pt: (b, 0, 0))]
        + [page_spec(i) for i in range(SCAN_PAGES)],
        out_specs=pl.BlockSpec((None, H_B, 128), lambda b, g, pt: (b, 0, 0)),
        scratch_shapes=[pltpu.VMEM((H_B, 128), F32)],
    )
    return pl.pallas_call(
        _moba_scan_kernel,
        grid_spec=grid_spec,
        out_shape=jax.ShapeDtypeStruct((nb, H_B, 128), jnp.int32),
        compiler_params=_params("parallel", "arbitrary"),
        name="moba_scan",
    )(page_table.reshape(-1), q_rot.reshape(nb, H_B, HD_B), *([cache_k] * SCAN_PAGES))


SEL_PAGES = MOBA_TOPK * PAGES_PER_BLOCK


def _moba_decode_kernel(pg_ref, q_ref, kn_ref, vn_ref, *refs):
    ks, vs, o_ref = refs[:SEL_PAGES], refs[SEL_PAGES:2 * SEL_PAGES], refs[2 * SEL_PAGES]
    q = q_ref[...]
    scale = HD_B ** -0.5
    rid = lax.broadcasted_iota(jnp.int32, (PAGE_SIZE * H_B, 1), 0)
    mine = rid % H_B == pl.program_id(1)
    logits = [jnp.where(mine, jnp.sum(k[...] * q, axis=1, keepdims=True) * scale, NEG) for k in ks]
    own = jnp.sum(q * kn_ref[...], axis=1, keepdims=True) * scale
    m = own
    for s in logits:
        m = jnp.maximum(m, jnp.max(s, axis=0, keepdims=True))
    p_own = jnp.exp(own - m)
    l = p_own
    acc = p_own * vn_ref[...]
    for s, v in zip(logits, vs):
        p = jnp.exp(s - m)
        l = l + jnp.sum(p, axis=0, keepdims=True)
        acc = acc + jnp.sum(p * v[...], axis=0, keepdims=True)
    o_ref[...] = acc / l


def _moba_decode(q_rot, k_new, v_new, cache_k, cache_v, layer, pages):
    nb = q_rot.shape[0]
    hw = H_B * HD_B
    n_layers, n_pool = cache_k.shape[:2]
    kpool = cache_k.reshape(n_layers, n_pool, PAGE_SIZE * H_B, HD_B)
    vpool = cache_v.reshape(n_layers, n_pool, PAGE_SIZE * H_B, HD_B)

    def page_spec(i):
        return pl.BlockSpec((None, None, PAGE_SIZE * H_B, HD_B),
                            lambda b, h, pg: (layer, pg[(b * H_B + h) * SEL_PAGES + i], 0, 0))

    row = pl.BlockSpec((None, 1, HD_B), lambda b, h, pg: (b, 0, h))
    grid_spec = pltpu.PrefetchScalarGridSpec(
        num_scalar_prefetch=1,
        grid=(nb, H_B),
        in_specs=[row, row, row] + [page_spec(i) for i in range(SEL_PAGES)] * 2,
        out_specs=row,
    )
    r3 = lambda a: a.reshape(nb, 1, hw)
    out = pl.pallas_call(
        _moba_decode_kernel,
        grid_spec=grid_spec,
        out_shape=jax.ShapeDtypeStruct((nb, 1, hw), F32),
        compiler_params=_params("parallel", "arbitrary"),
        name="moba_decode",
    )(pages, r3(q_rot), r3(k_new), r3(v_new), *([kpool] * SEL_PAGES), *([vpool] * SEL_PAGES))
    return out.reshape(nb, hw)


def _ffn_mid_kernel(u_ref, buf_ref, cw_ref, cb_ref, act_ref, nb_ref):
    u = u_ref[...]
    cw = cw_ref[...]
    y = buf_ref[:, 0, :] * cw[0:1]
    for i in range(1, FFN_CONV - 1):
        y = y + buf_ref[:, i, :] * cw[i:i + 1]
        nb_ref[:, i - 1, :] = buf_ref[:, i, :]
    y = y + u * cw[FFN_CONV - 1:FFN_CONV] + cb_ref[...]
    nb_ref[:, FFN_CONV - 2, :] = u
    act_ref[...] = _silu(y[:, :D_FF]) * y[:, D_FF:]


def _ffn_mid_step(u, buf, conv_w, conv_b):
    nb = u.shape[0]
    full = lambda shape: pl.BlockSpec(shape, lambda i: (0,) * len(shape))
    return pl.pallas_call(
        _ffn_mid_kernel,
        grid=(1,),
        in_specs=[full((nb, 2 * D_FF)), full((nb, FFN_CONV - 1, 2 * D_FF)), full((FFN_CONV, 2 * D_FF)),
                  full((1, 2 * D_FF))],
        out_specs=[full((nb, D_FF)), full((nb, FFN_CONV - 1, 2 * D_FF))],
        out_shape=[jax.ShapeDtypeStruct((nb, D_FF), F32), jax.ShapeDtypeStruct((nb, FFN_CONV - 1, 2 * D_FF), F32)],
        compiler_params=_params("arbitrary"),
        name="ffn_mid_step",
    )(u, buf, conv_w, conv_b.reshape(1, 2 * D_FF))
```
